```python
import jax, jax.numpy as jnp
from jax import lax
import numpy as np

D_MODEL = 2048
BATCH = 8
SEQ = 4096
DEPTH = 4
DEC_BATCH = 16
DEC_SEQ = 64
PAST_LEN = 1024

CHUNK = 64
Q_BLOCK = 128
ATT_HEADS = 8
ATT_HD = 128
ATT_KV = 2
ATT_REP = ATT_HEADS // ATT_KV
ATT_W = ATT_HEADS * ATT_HD
IDX_HEADS = 16
IDX_DIM = 64
IDX_TOPK = 256
IDX_SCALE = (IDX_DIM * IDX_HEADS) ** -0.5
RW_HEADS = 16
RW_HD = 64
RW_W = RW_HEADS * RW_HD
RW_DECAY_R = 96
RW_A_R = 96
RW_G_R = 256
RW_PROJ = 3 * RW_W + RW_DECAY_R + RW_A_R + RW_G_R
RW_DECAY_SCALE = 0.606531
RW_GN_EPS = 64e-5
GLA_HEADS = 4
GLA_DK = 128
GLA_DV = 256
GLA_KW = GLA_HEADS * GLA_DK
GLA_VW = GLA_HEADS * GLA_DV
GLA_GATE_R = 16
GLA_GATE_NORM = 16.0
GLA_NORM_EPS = 1e-5
FFN_HIDDEN = -(-8 * D_MODEL // (3 * 256)) * 256
DN_ALPHA = (2 * DEPTH) ** 0.25
DN_BETA = (8 * DEPTH) ** -0.25
LN_EPS = 1e-5
IN_SPLITS = (ATT_W, ATT_KV * ATT_HD, ATT_KV * ATT_HD, IDX_HEADS * IDX_DIM, IDX_DIM, IDX_HEADS,
             RW_PROJ,
             GLA_KW, GLA_KW, GLA_VW, GLA_GATE_R, GLA_VW,
             3 * D_MODEL)
IN_COLS = sum(IN_SPLITS)

kernel_name = 'hybrid_dsa_rwkv7_gla_streaming_step'


def _split(a, sizes):
    return jnp.split(a, np.cumsum(sizes)[:-1].tolist(), axis=-1)


def _layer_norm(x, g, b):
    x32 = x.astype(jnp.float32)
    mu = jnp.mean(x32, -1, keepdims=True)
    var = jnp.mean(jnp.square(x32 - mu), -1, keepdims=True)
    y = (x32 - mu) * lax.rsqrt(var + LN_EPS) * g.astype(jnp.float32) + b.astype(jnp.float32)
    return y.astype(x.dtype)


def _alibi_slopes(n):
    return jnp.exp2(-8.0 * jnp.arange(1, n + 1, dtype=jnp.float32) / n)


def _dsa_attention(q, qi, wi, k_all, v_all, ki_all, past):
    B, T = q.shape[0], q.shape[1]
    L = k_all.shape[1]
    topk = min(IDX_TOPK, L // 4)
    qb = min(Q_BLOCK, T)
    nb = T // qb
    key_chunk = jnp.arange(L) // CHUNK
    slopes = _alibi_slopes(ATT_HEADS).reshape(ATT_KV, ATT_REP)
    ki32 = ki_all.astype(jnp.float32)

    def blocks(a):
        return jnp.swapaxes(a.reshape((B, nb, qb) + a.shape[2:]), 0, 1)

    def one_block(args):
        q_b, qi_b, wi_b, pos_b = args
        rel = jax.nn.relu(jnp.einsum('bqhd,bsd->bqhs', qi_b.astype(jnp.float32), ki32))
        score = jnp.einsum('bqhs,bqh->bqs', rel, wi_b.astype(jnp.float32)) * IDX_SCALE
        adm = key_chunk[None, :] <= (pos_b // CHUNK)[:, None]
        score = jnp.where(adm[None], score, -jnp.inf)
        top_val, idx = lax.top_k(score, topk)
        valid = top_val > -jnp.inf
        k_sel = jax.vmap(lambda kb, ib: kb[ib])(k_all, idx)
        v_sel = jax.vmap(lambda vb, ib: vb[ib])(v_all, idx)
        qg = q_b.reshape(B, qb, ATT_KV, ATT_REP, ATT_HD)
        logits = jnp.einsum('bqgrd,bqkgd->bqgrk', qg, k_sel).astype(jnp.float32) * ATT_HD ** -0.5
        dist = jnp.abs(pos_b[None, :, None] - idx).astype(jnp.float32)
        logits = logits - slopes[None, None, :, :, None] * dist[:, :, None, None, :]
        logits = jnp.where(valid[:, :, None, None, :], logits, -jnp.inf)
        probs = jax.nn.softmax(logits, axis=-1).astype(v_sel.dtype)
        o = jnp.einsum('bqgrk,bqkgd->bqgrd', probs, v_sel)
        return o.reshape(B, qb, ATT_W)

    q_pos = (past + jnp.arange(T, dtype=jnp.int32)).reshape(nb, qb)
    out = lax.map(one_block, (blocks(q), blocks(qi), blocks(wi), q_pos))
    return jnp.swapaxes(out, 0, 1).reshape(B, T, ATT_W)


def _rwkv7(pr, shift_prev, S0, p):
    B, T, _ = pr.shape
    dt = pr.dtype
    f32 = jnp.float32
    prev = jnp.concatenate([shift_prev[:, None, :].astype(dt), pr[:, :-1]], axis=1)
    xs = pr + (prev - pr) * p['rw_mu']
    r, k, v, xw, xa, xg = _split(xs, (RW_W, RW_W, RW_W, RW_DECAY_R, RW_A_R, RW_G_R))
    w = jnp.exp(-RW_DECAY_SCALE * jax.nn.sigmoid((p['rw_w0'] + jnp.tanh(xw) @ p['rw_w2']).astype(f32)))
    a = jax.nn.sigmoid((p['rw_a0'] + xa @ p['rw_a2']).astype(f32))
    g = jax.nn.sigmoid(xg) @ p['rw_g2']
    r = r.astype(f32)
    k = k.astype(f32)
    v = v.astype(f32)
    kk = k * p['rw_kk'].astype(f32)
    k = k * (1.0 + (a - 1.0) * p['rw_ka'].astype(f32))
    shp = (B, T, RW_HEADS, RW_HD)
    kk = kk.reshape(shp)
    kk = kk / jnp.maximum(jnp.sqrt(jnp.sum(jnp.square(kk), -1, keepdims=True)), 1e-12)
    r, k, v, w, a = r.reshape(shp), k.reshape(shp), v.reshape(shp), w.reshape(shp), a.reshape(shp)

    def step(S, inp):
        r_t, w_t, k_t, v_t, kk_t, a_t = inp
        sa = jnp.einsum('bhvk,bhk->bhv', S, -kk_t)
        S = S * w_t[:, :, None, :] + sa[..., None] * (kk_t * a_t)[:, :, None, :] + v_t[..., None] * k_t[:, :, None, :]
        return S, jnp.einsum('bhvk,bhk->bhv', S, r_t)

    tm = lambda t: jnp.swapaxes(t, 0, 1)
    S_T, o = lax.scan(step, S0.astype(f32), (tm(r), tm(w), tm(k), tm(v), tm(kk), tm(a)))
    o = tm(o)
    mu = jnp.mean(o, -1, keepdims=True)
    var = jnp.mean(jnp.square(o - mu), -1, keepdims=True)
    o = (o - mu) * lax.rsqrt(var + RW_GN_EPS) * p['rw_ln_g'].astype(f32).reshape(RW_HEADS, RW_HD) \
        + p['rw_ln_b'].astype(f32).reshape(RW_HEADS, RW_HD)
    o = o + jnp.sum(r * k * p['rw_rk'].astype(f32).reshape(RW_HEADS, RW_HD), -1, keepdims=True) * v
    out = o.reshape(B, T, RW_W).astype(dt) * g
    return out, S_T.astype(S0.dtype), pr[:, -1]


def _gla(q, k, v, log_a, S0):
    B, T = q.shape[0], q.shape[1]
    c = CHUNK if T % CHUNK == 0 else T
    nc = T // c
    f32 = jnp.float32

    def chunks(t):
        return t.astype(f32).reshape(B, nc, c, GLA_HEADS, t.shape[-1]).transpose(1, 0, 3, 2, 4)

    causal = jnp.tril(jnp.ones((c, c), dtype=bool))[:, :, None]

    def step(S, inp):
        qc, kc, vc, lc = inp
        b = jnp.cumsum(lc, axis=2)
        inter = jnp.einsum('bhck,bhkv->bhcv', qc * jnp.exp(b), S)
        diff = b[:, :, :, None, :] - b[:, :, None, :, :]
        decay = jnp.where(causal, jnp.exp(jnp.where(causal, diff, 0.0)), 0.0)
        att = jnp.einsum('bhik,bhijk,bhjk->bhij', qc, decay, kc)
        o = inter + jnp.einsum('bhij,bhjv->bhiv', att, vc)
        b_last = b[:, :, -1:, :]
        S = jnp.exp(b_last[:, :, 0, :])[..., None] * S + jnp.einsum('bhck,bhcv->bhkv', kc * jnp.exp(b_last - b), vc)
        return S, o

    S_T, o = lax.scan(step, S0.astype(f32), (chunks(q), chunks(k), chunks(v), chunks(log_a)))
    o = o.transpose(1, 0, 3, 2, 4).reshape(B, T, GLA_HEADS, GLA_DV)
    return o, S_T.astype(S0.dtype)


def _trunk_layer(x, k_cache, v_cache, ki_cache, s_rwkv, s_shift, s_gla, p):
    B, T, _ = x.shape
    past = k_cache.shape[1]
    dt = x.dtype
    f32 = jnp.float32
    proj = x @ p['w_in']
    qa, ka, va, qi, ki, wi, pr, qc, kc, vc, gl, og, gates = _split(proj, IN_SPLITS)
    qa = qa.reshape(B, T, ATT_HEADS, ATT_HD)
    ka = ka.reshape(B, T, ATT_KV, ATT_HD)
    va = va.reshape(B, T, ATT_KV, ATT_HD)
    qi = qi.reshape(B, T, IDX_HEADS, IDX_DIM)
    k_all = jnp.concatenate([k_cache.astype(dt), ka], axis=1)
    v_all = jnp.concatenate([v_cache.astype(dt), va], axis=1)
    ki_all = jnp.concatenate([ki_cache.astype(dt), ki], axis=1)
    o_a = _dsa_attention(qa, qi, wi, k_all, v_all, ki_all, past)
    o_b, s_rwkv_new, s_shift_new = _rwkv7(pr, s_shift, s_rwkv, p)
    log_a = jax.nn.log_sigmoid((gl @ p['gla_w2'] + p['gla_b']).astype(f32)) / GLA_GATE_NORM
    o_c, s_gla_new = _gla(qc.reshape(B, T, GLA_HEADS, GLA_DK) * GLA_DK ** -0.5,
                          kc.reshape(B, T, GLA_HEADS, GLA_DK),
                          vc.reshape(B, T, GLA_HEADS, GLA_DV),
                          log_a.reshape(B, T, GLA_HEADS, GLA_DK), s_gla)
    o_c = o_c * lax.rsqrt(jnp.mean(jnp.square(o_c), -1, keepdims=True) + GLA_NORM_EPS) \
        * p['gla_norm_g'].astype(f32).reshape(GLA_HEADS, GLA_DV)
    o_c = o_c.reshape(B, T, GLA_VW).astype(dt) * jax.nn.silu(og)
    g_a, g_b, g_c = jnp.split(jax.nn.sigmoid(gates), 3, axis=-1)
    merged = g_a * (o_a @ p['w_branch_a']) + g_b * (o_b @ p['w_branch_b']) + g_c * (o_c @ p['w_branch_c'])
    h = _layer_norm(DN_ALPHA * x + merged @ p['w_out'], p['ln1_g'], p['ln1_b'])
    ff = (jax.nn.silu(h @ p['ffn_w1']) * (h @ p['ffn_w3'])) @ p['ffn_w2']
    y = _layer_norm(DN_ALPHA * h + ff, p['ln2_g'], p['ln2_b'])
    return y, (ka, va, ki, s_rwkv_new, s_shift_new, s_gla_new)


def setup_inputs(seed: int = 0) -> dict:
    key = jax.random.key(seed)
    ks = jax.random.split(key, 34)
    f32 = jnp.float32

    def nrm(i, shape, s):
        return jax.random.normal(ks[i], shape, f32) * s

    return {
        'x_prompt': nrm(0, (BATCH, SEQ, D_MODEL), 1.0),
        'x_sample': nrm(1, (DEC_BATCH, DEC_SEQ, D_MODEL), 1.0),
        'cache_k': nrm(2, (DEPTH, DEC_BATCH, PAST_LEN, ATT_KV, ATT_HD), 1.0),
        'cache_v': nrm(3, (DEPTH, DEC_BATCH, PAST_LEN, ATT_KV, ATT_HD), 1.0),
        'cache_kidx': nrm(4, (DEPTH, DEC_BATCH, PAST_LEN, IDX_DIM), 1.0),
        'state_rwkv': nrm(5, (DEPTH, DEC_BATCH, RW_HEADS, RW_HD, RW_HD), 0.5),
        'state_rwkv_shift': nrm(6, (DEPTH, DEC_BATCH, RW_PROJ), 1.0),
        'state_gla': nrm(7, (DEPTH, DEC_BATCH, GLA_HEADS, GLA_DK, GLA_DV), 0.5),
        'w_in': nrm(8, (DEPTH, D_MODEL, IN_COLS), D_MODEL ** -0.5),
        'rw_mu': jax.random.uniform(ks[9], (DEPTH, RW_PROJ), f32),
        'rw_w0': nrm(10, (DEPTH, RW_W), 0.5),
        'rw_w2': nrm(11, (DEPTH, RW_DECAY_R, RW_W), 0.5 * RW_DECAY_R ** -0.5),
        'rw_a0': nrm(12, (DEPTH, RW_W), 0.1),
        'rw_a2': nrm(13, (DEPTH, RW_A_R, RW_W), 0.5 * RW_A_R ** -0.5),
        'rw_g2': nrm(14, (DEPTH, RW_G_R, RW_W), RW_G_R ** -0.5),
        'rw_kk': 1.0 + nrm(15, (DEPTH, RW_W), 0.1),
        'rw_ka': 1.0 + nrm(16, (DEPTH, RW_W), 0.1),
        'rw_rk': nrm(17, (DEPTH, RW_W), 0.1),
        'rw_ln_g': 1.0 + nrm(18, (DEPTH, RW_W), 0.02),
        'rw_ln_b': nrm(19, (DEPTH, RW_W), 0.02),
        'gla_w2': nrm(20, (DEPTH, GLA_GATE_R, GLA_KW), GLA_GATE_R ** -0.5),
        'gla_b': nrm(21, (DEPTH, GLA_KW), 0.1),
        'gla_norm_g': 1.0 + nrm(22, (DEPTH, GLA_VW), 0.02),
        'w_branch_a': nrm(23, (DEPTH, ATT_W, D_MODEL), ATT_W ** -0.5 * DN_BETA),
        'w_branch_b': nrm(24, (DEPTH, RW_W, D_MODEL), RW_W ** -0.5 * DN_BETA),
        'w_branch_c': nrm(25, (DEPTH, GLA_VW, D_MODEL), GLA_VW ** -0.5 * DN_BETA),
        'w_out': nrm(26, (DEPTH, D_MODEL, D_MODEL), D_MODEL ** -0.5 * DN_BETA),
        'ln1_g': 1.0 + nrm(27, (DEPTH, D_MODEL), 0.02),
        'ln1_b': nrm(28, (DEPTH, D_MODEL), 0.02),
        'ln2_g': 1.0 + nrm(29, (DEPTH, D_MODEL), 0.02),
        'ln2_b': nrm(30, (DEPTH, D_MODEL), 0.02),
        'ffn_w1': nrm(31, (DEPTH, D_MODEL, FFN_HIDDEN), D_MODEL ** -0.5),
        'ffn_w3': nrm(32, (DEPTH, D_MODEL, FFN_HIDDEN), D_MODEL ** -0.5),
        'ffn_w2': nrm(33, (DEPTH, FFN_HIDDEN, D_MODEL), FFN_HIDDEN ** -0.5 * DN_BETA),
    }


def reference(x_prompt, x_sample, cache_k, cache_v, cache_kidx, state_rwkv, state_rwkv_shift, state_gla,
              w_in, rw_mu, rw_w0, rw_w2, rw_a0, rw_a2, rw_g2, rw_kk, rw_ka, rw_rk, rw_ln_g, rw_ln_b,
              gla_w2, gla_b, gla_norm_g, w_branch_a, w_branch_b, w_branch_c, w_out,
              ln1_g, ln1_b, ln2_g, ln2_b, ffn_w1, ffn_w3, ffn_w2):
    bp = x_prompt.shape[0]
    dt = x_prompt.dtype
    empty_kv = jnp.zeros((bp, 0, ATT_KV, ATT_HD), dt)
    empty_ki = jnp.zeros((bp, 0, IDX_DIM), dt)
    zero_rw = jnp.zeros((bp, RW_HEADS, RW_HD, RW_HD), dt)
    zero_shift = jnp.zeros((bp, RW_PROJ), dt)
    zero_gla = jnp.zeros((bp, GLA_HEADS, GLA_DK, GLA_DV), dt)
    yp, ys = x_prompt, x_sample
    new_p = [[] for _ in range(6)]
    new_s = [[] for _ in range(6)]
    for l in range(DEPTH):
        p = {'w_in': w_in[l], 'rw_mu': rw_mu[l], 'rw_w0': rw_w0[l], 'rw_w2': rw_w2[l],
             'rw_a0': rw_a0[l], 'rw_a2': rw_a2[l], 'rw_g2': rw_g2[l], 'rw_kk': rw_kk[l],
             'rw_ka': rw_ka[l], 'rw_rk': rw_rk[l], 'rw_ln_g': rw_ln_g[l], 'rw_ln_b': rw_ln_b[l],
             'gla_w2': gla_w2[l], 'gla_b': gla_b[l], 'gla_norm_g': gla_norm_g[l],
             'w_branch_a': w_branch_a[l], 'w_branch_b': w_branch_b[l], 'w_branch_c': w_branch_c[l],
             'w_out': w_out[l], 'ln1_g': ln1_g[l], 'ln1_b': ln1_b[l], 'ln2_g': ln2_g[l], 'ln2_b': ln2_b[l],
             'ffn_w1': ffn_w1[l], 'ffn_w3': ffn_w3[l], 'ffn_w2': ffn_w2[l]}
        yp, st_p = _trunk_layer(yp, empty_kv, empty_kv, empty_ki, zero_rw, zero_shift, zero_gla, p)
        ys, st_s = _trunk_layer(ys, cache_k[l], cache_v[l], cache_kidx[l], state_rwkv[l],
                                state_rwkv_shift[l], state_gla[l], p)
        for i in range(6):
            new_p[i].append(st_p[i])
            new_s[i].append(st_s[i])
    sp = [jnp.stack(a) for a in new_p]
    ss = [jnp.stack(a) for a in new_s]
    return (yp, ys, sp[0], sp[1], sp[2], sp[3], sp[4], sp[5], ss[0], ss[1], ss[2], ss[3], ss[4], ss[5])
```

```python
import functools

import numpy as np
import jax
import jax.numpy as jnp
from jax import lax
from jax.experimental import pallas as pl
from jax.experimental.pallas import tpu as pltpu

F32 = jnp.float32
BF16 = jnp.bfloat16
I32 = jnp.int32

D_MODEL = 2048
DEPTH = 4
CHUNK = 64
Q_BLOCK = 128
ATT_HEADS = 8
ATT_HD = 128
ATT_KV = 2
ATT_REP = ATT_HEADS // ATT_KV
ATT_W = ATT_HEADS * ATT_HD
IDX_HEADS = 16
IDX_DIM = 64
IDX_TOPK = 256
IDX_SCALE = (IDX_DIM * IDX_HEADS) ** -0.5
RW_HEADS = 16
RW_HD = 64
RW_W = RW_HEADS * RW_HD
RW_DECAY_R = 96
RW_A_R = 96
RW_G_R = 256
RW_PROJ = 3 * RW_W + RW_DECAY_R + RW_A_R + RW_G_R
RW_DECAY_SCALE = 0.606531
RW_GN_EPS = 64e-5
GLA_HEADS = 4
GLA_DK = 128
GLA_DV = 256
GLA_KW = GLA_HEADS * GLA_DK
GLA_VW = GLA_HEADS * GLA_DV
GLA_GATE_R = 16
GLA_GATE_NORM = 16.0
GLA_NORM_EPS = 1e-5
FFN_HIDDEN = -(-8 * D_MODEL // (3 * 256)) * 256
DN_ALPHA = (2 * DEPTH) ** 0.25
LN_EPS = 1e-5

LANES = 128
SUBLANES = 8
VMEM_LIMIT = 56 * 1024 * 1024

_SEGS = (
    ("qa", 0, 1024, 1024), ("qi", 1536, 1024, 1024), ("vc", 7184, 1024, 1024),
    ("og", 8224, 1024, 1024), ("gates", 9248, 6144, 6144),
    ("r", 2640, 1024, 1024), ("k", 3664, 1024, 1024), ("v", 4688, 1024, 1024),
    ("xw", 5712, 96, 128), ("xa", 5808, 96, 128), ("xg", 5904, 256, 256),
    ("qc", 6160, 512, 512), ("kc", 6672, 512, 512),
    ("ka", 1024, 256, 256), ("va", 1280, 256, 256),
    ("ki", 2560, 64, 128), ("wi", 2624, 16, 128), ("gl", 8208, 16, 128),
)
_OFF = {}
_o = 0
for _n, _s, _w, _p in _SEGS:
    _OFF[_n] = _o
    _o += _p
PROJ_W = _o
KEY_NEG_INF = -0x7F800000
NEG_BIG = -1e30


def _cparams(sem):
    return pltpu.CompilerParams(dimension_semantics=sem, vmem_limit_bytes=VMEM_LIMIT)


def _bdot(a, b):
    return jnp.dot(a.astype(BF16), b.astype(BF16), preferred_element_type=F32)


def _split2(x):
    hi = x.astype(BF16)
    lo = (x - hi.astype(F32)).astype(BF16)
    return hi, lo


def _dot3(a, b):
    ah, al = _split2(a)
    bh, bl = _split2(b)
    d = functools.partial(jnp.dot, preferred_element_type=F32)
    return d(ah, bh) + (d(al, bh) + d(ah, bl))


def _sigmoid(x):
    return 1.0 / (1.0 + jnp.exp(-x))


def _log_sigmoid(x):
    return jnp.minimum(x, 0.0) - jnp.log(1.0 + jnp.exp(-jnp.abs(x)))


def _layer_norm(x, g, b):
    mu = jnp.mean(x, -1, keepdims=True)
    xc = x - mu
    var = jnp.mean(xc * xc, -1, keepdims=True)
    return xc * lax.rsqrt(var + LN_EPS) * g + b


def _mm_kernel(a_ref, b_ref, o_ref):
    o_ref[...] = jnp.dot(a_ref[...], b_ref[...], preferred_element_type=F32)


def _matmul(a, b, tm, tn):
    m, k = a.shape
    n = b.shape[1]
    return pl.pallas_call(
        _mm_kernel,
        grid=(m // tm, pl.cdiv(n, tn)),
        in_specs=[pl.BlockSpec((tm, k), lambda i, j: (i, 0)),
                  pl.BlockSpec((k, tn), lambda i, j: (0, j))],
        out_specs=pl.BlockSpec((tm, tn), lambda i, j: (i, j)),
        out_shape=jax.ShapeDtypeStruct((m, n), F32),
        compiler_params=_cparams(("parallel", "arbitrary")),
        name="in_proj",
    )(a, b)


def _merge_kernel(oa_ref, ob_ref, g_ref, oc_ref, ga_ref, gb_ref, gc_ref,
                  wa_ref, wb_ref, wc_ref, o_ref):
    ob = (ob_ref[...] * g_ref[...]).astype(BF16)
    d = functools.partial(jnp.dot, preferred_element_type=F32)
    acc = _sigmoid(ga_ref[...]) * d(oa_ref[...], wa_ref[...])
    acc += _sigmoid(gb_ref[...]) * d(ob, wb_ref[...])
    acc += _sigmoid(gc_ref[...]) * d(oc_ref[...], wc_ref[...])
    o_ref[...] = acc.astype(BF16)


def _merge(oa, ob, g, oc, proj, wa, wb, wc, tm, tn):
    m = oa.shape[0]
    gblk = _OFF["gates"] // tn
    nj = D_MODEL // tn
    row = lambda i, j: (i, 0)
    wspec = pl.BlockSpec((ATT_W, tn), lambda i, j: (0, j))
    return pl.pallas_call(
        _merge_kernel,
        grid=(m // tm, nj),
        in_specs=[pl.BlockSpec((tm, ATT_W), row), pl.BlockSpec((tm, RW_W), row),
                  pl.BlockSpec((tm, RW_W), row), pl.BlockSpec((tm, GLA_VW), row),
                  pl.BlockSpec((tm, tn), lambda i, j: (i, gblk + j)),
                  pl.BlockSpec((tm, tn), lambda i, j: (i, gblk + nj + j)),
                  pl.BlockSpec((tm, tn), lambda i, j: (i, gblk + 2 * nj + j)),
                  wspec, wspec, wspec],
        out_specs=pl.BlockSpec((tm, tn), lambda i, j: (i, j)),
        out_shape=jax.ShapeDtypeStruct((m, D_MODEL), BF16),
        compiler_params=_cparams(("parallel", "arbitrary")),
        name="merge",
    )(oa, ob, g, oc, proj, proj, proj, wa, wb, wc)


def _hproj_kernel(m_ref, w_ref, x_ref, g_ref, b_ref, h_ref, hb_ref):
    z = DN_ALPHA * x_ref[...] + jnp.dot(m_ref[...], w_ref[...], preferred_element_type=F32)
    h = _layer_norm(z, g_ref[...], b_ref[...])
    h_ref[...] = h
    hb_ref[...] = h.astype(BF16)


def _hproj(merged, w_out, x, g, b, tm):
    m = x.shape[0]
    row = lambda i: (i, 0)
    fix = lambda i: (0, 0)
    return pl.pallas_call(
        _hproj_kernel,
        grid=(m // tm,),
        in_specs=[pl.BlockSpec((tm, D_MODEL), row), pl.BlockSpec((D_MODEL, D_MODEL), fix),
                  pl.BlockSpec((tm, D_MODEL), row), pl.BlockSpec((1, D_MODEL), fix),
                  pl.BlockSpec((1, D_MODEL), fix)],
        out_specs=[pl.BlockSpec((tm, D_MODEL), row), pl.BlockSpec((tm, D_MODEL), row)],
        out_shape=[jax.ShapeDtypeStruct((m, D_MODEL), F32),
                   jax.ShapeDtypeStruct((m, D_MODEL), BF16)],
        compiler_params=_cparams(("parallel",)),
        name="out_proj_ln",
    )(merged, w_out, x, g, b)


def _ffn_kernel(hb_ref, h_ref, w1_ref, w3_ref, w2_ref, g_ref, b_ref, y_ref, yb_ref, acc_ref):
    j = pl.program_id(1)

    @pl.when(j == 0)
    def _():
        acc_ref[...] = jnp.zeros_like(acc_ref)

    hb = hb_ref[...]
    u1 = jnp.dot(hb, w1_ref[...], preferred_element_type=F32)
    u3 = jnp.dot(hb, w3_ref[...], preferred_element_type=F32)
    u = (u1 * _sigmoid(u1) * u3).astype(BF16)
    acc_ref[...] += jnp.dot(u, w2_ref[...], preferred_element_type=F32)

    @pl.when(j == pl.num_programs(1) - 1)
    def _():
        y = _layer_norm(DN_ALPHA * h_ref[...] + acc_ref[...], g_ref[...], b_ref[...])
        y_ref[...] = y
        yb_ref[...] = y.astype(BF16)


def _ffn(hb, h, w1, w3, w2, g, b, tm, th):
    m = h.shape[0]
    row = lambda i, j: (i, 0)
    fix = lambda i, j: (0, 0)
    return pl.pallas_call(
        _ffn_kernel,
        grid=(m // tm, FFN_HIDDEN // th),
        in_specs=[pl.BlockSpec((tm, D_MODEL), row), pl.BlockSpec((tm, D_MODEL), row),
                  pl.BlockSpec((D_MODEL, th), lambda i, j: (0, j)),
                  pl.BlockSpec((D_MODEL, th), lambda i, j: (0, j)),
                  pl.BlockSpec((th, D_MODEL), lambda i, j: (j, 0)),
                  pl.BlockSpec((1, D_MODEL), fix), pl.BlockSpec((1, D_MODEL), fix)],
        out_specs=[pl.BlockSpec((tm, D_MODEL), row), pl.BlockSpec((tm, D_MODEL), row)],
        out_shape=[jax.ShapeDtypeStruct((m, D_MODEL), F32),
                   jax.ShapeDtypeStruct((m, D_MODEL), BF16)],
        scratch_shapes=[pltpu.VMEM((tm, D_MODEL), F32)],
        compiler_params=_cparams(("parallel", "arbitrary")),
        name="ffn_ln",
    )(hb, h, w1, w3, w2, g, b)


def _dsa_kernel(qa_ref, qi_ref, wi_ref, k_ref, vt_ref, ki_ref, o_ref,
                kcat_ref, keys_ref, wq_ref, qt_ref, m_ref, l_ref, acc_ref, sel_ref,
                *, past, seq, l_real, kblk, topk):
    qb = qa_ref.shape[0]
    l_pad = ki_ref.shape[0]
    i = pl.program_id(1)

    @pl.when(i == 0)
    def _():
        def prep(c, carry):
            r0 = pl.multiple_of(c * kblk, kblk)
            x = ki_ref[pl.ds(r0, kblk), :]
            hi = x.astype(BF16)
            lo = x - hi.astype(F32)
            a = (hi.astype(F32) + pltpu.roll(lo, IDX_DIM, 1)).astype(BF16)
            kcat_ref[pl.ds(r0, kblk), 0:LANES] = a
            kcat_ref[pl.ds(r0, kblk), LANES:2 * LANES] = hi
            return carry
        lax.fori_loop(0, l_pad // kblk, prep, 0)

    qit = qi_ref[...].T
    qhi = qit.astype(BF16)
    qlo = (qit - qhi.astype(F32)).astype(BF16)
    zero = jnp.zeros((IDX_DIM, qb), BF16)
    for h in range(IDX_HEADS):
        hs = slice(h * IDX_DIM, (h + 1) * IDX_DIM)
        cs = slice(h * qb, (h + 1) * qb)
        wq_ref[0:64, cs] = qhi[hs]
        wq_ref[64:128, cs] = qhi[hs]
        wq_ref[128:192, cs] = qlo[hs]
        wq_ref[192:256, cs] = zero
    wit = wi_ref[...].T
    qat = (qa_ref[...] * (ATT_HD ** -0.5)).T.astype(BF16)
    for h in range(ATT_HEADS):
        g, r = divmod(h, ATT_REP)
        qt_ref[g, :, r * qb:(r + 1) * qb] = qat[h * ATT_HD:(h + 1) * ATT_HD]

    q0 = past + i * qb
    pos = q0 + lax.broadcasted_iota(I32, (1, qb), 1)
    pos_chunk = lax.shift_right_logical(pos, 6)
    adm_keys = jnp.minimum(l_real, (lax.shift_right_logical(q0 + qb - 1, 6) + 1) * CHUNK)
    nkc = lax.div(adm_keys + (kblk - 1), kblk)

    def score_body(c, carry):
        r0 = pl.multiple_of(c * kblk, kblk)
        kc = kcat_ref[pl.ds(r0, kblk), :]
        acc = jnp.zeros((kblk, qb), F32)
        for hg in range(IDX_HEADS // 4):
            rel = jnp.dot(kc, wq_ref[:, hg * 4 * qb:(hg + 1) * 4 * qb],
                          preferred_element_type=F32)
            for hh in range(4):
                h = hg * 4 + hh
                acc = acc + jnp.maximum(rel[:, hh * qb:(hh + 1) * qb], 0.0) * wit[h:h + 1, :]
        score = acc * IDX_SCALE
        s_idx = r0 + lax.broadcasted_iota(I32, (kblk, qb), 0)
        adm = (lax.shift_right_logical(s_idx, 6) <= pos_chunk) & (s_idx < l_real)
        score = jnp.where(adm, score, -jnp.inf)
        bits = pltpu.bitcast(score, I32)
        keys_ref[pl.ds(r0, kblk), :] = jnp.where(bits < 0, -(bits & 0x7FFFFFFF), bits)
        return carry
    lax.fori_loop(0, nkc, score_body, 0)

    def count(pred_fn):
        def body(c, acc):
            r0 = pl.multiple_of(c * kblk, kblk)
            blk = keys_ref[pl.ds(r0, kblk), :]
            s_idx = r0 + lax.broadcasted_iota(I32, (kblk, qb), 0)
            return acc + jnp.sum(jnp.where(pred_fn(blk, s_idx), 1.0, 0.0), axis=0, keepdims=True)
        return lax.fori_loop(0, nkc, body, jnp.zeros((1, qb), F32))

    kf = float(topk)

    def bit_body(it, ans):
        step = lax.shift_left(jnp.int32(1), 31 - it)
        cand = ans + step
        ok = count(lambda blk, s: blk >= cand) >= kf
        return jnp.where(ok, cand, ans)
    thr = lax.fori_loop(0, 32, bit_body, jnp.full((1, qb), -2 ** 31, I32))

    n_gt = count(lambda blk, s: blk > thr)
    n_ge = count(lambda blk, s: blk >= thr)
    need = kf - n_gt
    tied = (n_ge > kf) & (thr > KEY_NEG_INF)
    sel_ref[0:1, :] = jnp.full((1, qb), l_pad, I32)

    @pl.when(jnp.max(jnp.where(tied, 1.0, 0.0)) > 0.0)
    def _():
        nbits = int(l_pad - 1).bit_length()

        def idx_body(it, p):
            cand = p + lax.shift_left(jnp.int32(1), nbits - 1 - it)
            few = count(lambda blk, s: (blk == thr) & (s < cand)) < need
            return jnp.where(few, cand, p)
        p = lax.fori_loop(0, nbits, idx_body, jnp.zeros((1, qb), I32))
        sel_ref[0:1, :] = jnp.where(tied, p, l_pad)
    p_last = sel_ref[0:1, :]

    m_ref[...] = jnp.full(m_ref.shape, -1e29, F32)
    l_ref[...] = jnp.zeros(l_ref.shape, F32)
    acc_ref[...] = jnp.zeros(acc_ref.shape, F32)
    slopes = [float(2.0 ** (-8.0 * (h + 1) / ATT_HEADS)) for h in range(ATT_HEADS)]

    def att_body(c, carry):
        r0 = pl.multiple_of(c * kblk, kblk)
        key = keys_ref[pl.ds(r0, kblk), :]
        s_idx = r0 + lax.broadcasted_iota(I32, (kblk, qb), 0)
        sel = ((key > thr) | ((key == thr) & (s_idx <= p_last))) & (key > KEY_NEG_INF)
        bias = jnp.where(sel, 0.0, NEG_BIG)
        dist = jnp.abs(pos - s_idx).astype(F32)
        kb = k_ref[pl.ds(r0, kblk), :]
        vt = vt_ref[:, pl.ds(r0, kblk)]
        for g in range(ATT_KV):
            logit = jnp.dot(kb[:, g * ATT_HD:(g + 1) * ATT_HD], qt_ref[g],
                            preferred_element_type=F32)
            ps = []
            alphas = []
            for r in range(ATT_REP):
                h = g * ATT_REP + r
                cs = slice(r * qb, (r + 1) * qb)
                lg = logit[:, cs] - slopes[h] * dist + bias
                m_old = m_ref[g, :, cs]
                m_new = jnp.maximum(m_old, jnp.max(lg, axis=0, keepdims=True))
                alpha = jnp.exp(m_old - m_new)
                pr = jnp.exp(lg - m_new)
                l_ref[g, :, cs] = alpha * l_ref[g, :, cs] + jnp.sum(pr, axis=0, keepdims=True)
                m_ref[g, :, cs] = m_new
                ps.append(pr.astype(BF16))
                alphas.append(alpha)
            pt = jnp.concatenate(ps, axis=1)
            al = jnp.concatenate(alphas, axis=1)
            pv = jnp.dot(vt[g * ATT_HD:(g + 1) * ATT_HD, :], pt, preferred_element_type=F32)
            acc_ref[g] = acc_ref[g] * al + pv
        return carry
    lax.fori_loop(0, nkc, att_body, 0)

    for h in range(ATT_HEADS):
        g, r = divmod(h, ATT_REP)
        cs = slice(r * qb, (r + 1) * qb)
        ot = acc_ref[g, :, cs] / l_ref[g, :, cs]
        o_ref[:, h * ATT_HD:(h + 1) * ATT_HD] = ot.T.astype(BF16)


def _dsa(q_ops, k_all, vt_all, ki_all, *, nb, nqb, past, seq, l_real, kblk):
    qb = Q_BLOCK
    l_pad = k_all.shape[1]
    topk = min(IDX_TOPK, l_real // 4)
    (qa, qa_map), (qi, qi_map), (wi, wi_map) = q_ops
    kern = functools.partial(_dsa_kernel, past=past, seq=seq, l_real=l_real, kblk=kblk, topk=topk)
    return pl.pallas_call(
        kern,
        grid=(nb, nqb),
        in_specs=[pl.BlockSpec((qb, ATT_W), qa_map), pl.BlockSpec((qb, IDX_HEADS * IDX_DIM), qi_map),
                  pl.BlockSpec((qb, LANES), wi_map),
                  pl.BlockSpec((None, l_pad, ATT_KV * ATT_HD), lambda b, i: (b, 0, 0)),
                  pl.BlockSpec((None, ATT_KV * ATT_HD, l_pad), lambda b, i: (b, 0, 0)),
                  pl.BlockSpec((None, l_pad, LANES), lambda b, i: (b, 0, 0))],
        out_specs=pl.BlockSpec((qb, ATT_W), lambda b, i: (b * nqb + i, 0)),
        out_shape=jax.ShapeDtypeStruct((nb * nqb * qb, ATT_W), BF16),
        scratch_shapes=[pltpu.VMEM((l_pad, 2 * LANES), BF16),
                        pltpu.VMEM((l_pad, qb), I32),
                        pltpu.VMEM((2 * LANES, IDX_HEADS * qb), BF16),
                        pltpu.VMEM((ATT_KV, ATT_HD, ATT_REP * qb), BF16),
                        pltpu.VMEM((ATT_KV, 1, ATT_REP * qb), F32),
                        pltpu.VMEM((ATT_KV, 1, ATT_REP * qb), F32),
                        pltpu.VMEM((ATT_KV, ATT_HD, ATT_REP * qb), F32),
                        pltpu.VMEM((SUBLANES, qb), I32)],
        compiler_params=_cparams(("parallel", "arbitrary")),
        name="dsa",
    )(qa, qi, wi, k_all, vt_all, ki_all)


def _rw_prep_kernel(r_ref, k_ref, v_ref, lr_ref, hr_ref, hk_ref, hv_ref, hlr_ref,
                    sr_ref, sk_ref, sv_ref, slr_ref, mr_ref, mk_ref, mv_ref, mlr_ref,
                    w0_ref, a0_ref, w2_ref, a2_ref, g2_ref,
                    ro_ref, ko_ref, vo_ref, wo_ref, ao_ref, go_ref, *, tiles_per_seq):
    first_tile = (pl.program_id(0) % tiles_per_seq) == 0

    def mix(x_ref, halo_ref, shift_ref, mu_ref):
        x = x_ref[...]
        tm = x.shape[0]
        first = jnp.where(first_tile, shift_ref[...], halo_ref[SUBLANES - 1:SUBLANES, :])
        rows = lax.broadcasted_iota(I32, x.shape, 0)
        prev = jnp.where(rows == 0, first, pltpu.roll(x, 1, 0))
        return x + (prev - x) * mu_ref[...]

    ro_ref[...] = mix(r_ref, hr_ref, sr_ref, mr_ref)
    ko_ref[...] = mix(k_ref, hk_ref, sk_ref, mk_ref)
    vo_ref[...] = mix(v_ref, hv_ref, sv_ref, mv_ref)
    lr = mix(lr_ref, hlr_ref, slr_ref, mlr_ref)
    xw = lr[:, 0:128]
    xa = lr[:, 128:256]
    xg = lr[:, 256:512]
    wpre = w0_ref[...] + _dot3(jnp.tanh(xw), w2_ref[...])
    wo_ref[...] = jnp.exp(-RW_DECAY_SCALE * _sigmoid(wpre))
    ao_ref[...] = _sigmoid(a0_ref[...] + _dot3(xa, a2_ref[...]))
    go_ref[...] = _dot3(_sigmoid(xg), g2_ref[...])


def _rw_prep(proj, row0, nb, seq, shifts, mus, w0, a0, w2p, a2p, g2, tm):
    m = nb * seq
    tps = seq // tm
    rb0 = row0 // tm
    hb = tm // SUBLANES
    cr, ck, cv = _OFF["r"] // RW_W, _OFF["k"] // RW_W, _OFF["v"] // RW_W
    clr = _OFF["xw"] // 512

    def main(c):
        return lambda i: (rb0 + i, c)

    def halo(c):
        return lambda i: (jnp.maximum((rb0 + i) * hb - 1, 0), c)

    bidx = lambda i: (i // tps, 0, 0)
    fix = lambda i: (0, 0)
    wide = [pl.BlockSpec((tm, RW_W), main(cr)), pl.BlockSpec((tm, RW_W), main(ck)),
            pl.BlockSpec((tm, RW_W), main(cv)), pl.BlockSpec((tm, 512), main(clr)),
            pl.BlockSpec((SUBLANES, RW_W), halo(cr)), pl.BlockSpec((SUBLANES, RW_W), halo(ck)),
            pl.BlockSpec((SUBLANES, RW_W), halo(cv)), pl.BlockSpec((SUBLANES, 512), halo(clr))]
    sh = [pl.BlockSpec((None, 1, RW_W), bidx)] * 3 + [pl.BlockSpec((None, 1, 512), bidx)]
    mu = [pl.BlockSpec((1, RW_W), fix)] * 3 + [pl.BlockSpec((1, 512), fix)]
    par = [pl.BlockSpec((1, RW_W), fix), pl.BlockSpec((1, RW_W), fix),
           pl.BlockSpec((128, RW_W), fix), pl.BlockSpec((128, RW_W), fix),
           pl.BlockSpec((RW_G_R, RW_W), fix)]
    out = jax.ShapeDtypeStruct((m, RW_W), F32)
    return pl.pallas_call(
        functools.partial(_rw_prep_kernel, tiles_per_seq=tps),
        grid=(m // tm,),
        in_specs=wide + sh + mu + par,
        out_specs=[pl.BlockSpec((tm, RW_W), lambda i: (i, 0))] * 6,
        out_shape=[out] * 6,
        compiler_params=_cparams(("parallel",)),
        name="rwkv_prep",
    )(proj, proj, proj, proj, proj, proj, proj, proj, *shifts, *mus, w0, a0, w2p, a2p, g2)


def _rw_scan_kernel(r_ref, k_ref, v_ref, w_ref, a_ref, kkp_ref, kap_ref, rkp_ref,
                    lng_ref, lnb_ref, s0_ref, o_ref, st_ref, s_scr, tmp_scr):
    tb = pl.program_id(1)
    nsteps = r_ref.shape[0]

    @pl.when(tb == 0)
    def _():
        s_scr[...] = s0_ref[...]

    kkp = kkp_ref[...]
    kap = kap_ref[...]
    rkp = rkp_ref[...]
    lng = lng_ref[...]
    lnb = lnb_ref[...]

    def step(t, carry):
        k = k_ref[t]
        a = a_ref[t]
        v = v_ref[t]
        r = r_ref[t]
        kk = k * kkp
        nrm = jnp.sqrt(jnp.sum(kk * kk, axis=0, keepdims=True))
        kk = kk / jnp.maximum(nrm, 1e-12)
        km = k * (1.0 + (a - 1.0) * kap)
        tmp_scr[0] = kk
        tmp_scr[1] = kk * a
        tmp_scr[2] = km
        parts = [jnp.zeros((RW_HD, LANES), F32) for _ in range(4)]
        for kx in range(RW_HD):
            parts[kx % 4] = parts[kx % 4] + s_scr[kx] * tmp_scr[0, kx:kx + 1, :]
        sa = -((parts[0] + parts[1]) + (parts[2] + parts[3]))
        outs = [jnp.zeros((RW_HD, LANES), F32) for _ in range(4)]
        for kx in range(RW_HD):
            s_new = (s_scr[kx] * w_ref[t, kx:kx + 1, :] + sa * tmp_scr[1, kx:kx + 1, :]
                     + v * tmp_scr[2, kx:kx + 1, :])
            s_scr[kx] = s_new
            outs[kx % 4] = outs[kx % 4] + s_new * r_ref[t, kx:kx + 1, :]
        o = (outs[0] + outs[1]) + (outs[2] + outs[3])
        mu = jnp.mean(o, axis=0, keepdims=True)
        oc = o - mu
        var = jnp.mean(oc * oc, axis=0, keepdims=True)
        on = oc * lax.rsqrt(var + RW_GN_EPS) * lng + lnb
        bonus = jnp.sum(r * km * rkp, axis=0, keepdims=True) * v
        o_ref[t] = on + bonus
        return carry
    lax.fori_loop(0, nsteps, step, 0)

    @pl.when(tb == pl.num_programs(1) - 1)
    def _():
        st_ref[...] = s_scr[...]


def _rw_scan(r, k, v, w, a, params, s0, tb):
    seq, _, pairs = r.shape
    assert pairs % LANES == 0, "batch * RW_HEADS must fill whole lane groups"
    seqb =pl.BlockSpec((tb, RW_HD, LANES), lambda g, t: (t, 0, g))
    par = pl.BlockSpec((RW_HD, LANES), lambda g, t: (0, 0))
    st = pl.BlockSpec((RW_HD, RW_HD, LANES), lambda g, t: (0, 0, g))
    return pl.pallas_call(
        _rw_scan_kernel,
        grid=(pairs // LANES, seq // tb),
        in_specs=[seqb] * 5 + [par] * 5 + [st],
        out_specs=[seqb, st],
        out_shape=[jax.ShapeDtypeStruct((seq, RW_HD, pairs), F32),
                   jax.ShapeDtypeStruct((RW_HD, RW_HD, pairs), F32)],
        scratch_shapes=[pltpu.VMEM((RW_HD, RW_HD, LANES), F32),
                        pltpu.VMEM((3, RW_HD, LANES), F32)],
        compiler_params=_cparams(("parallel", "arbitrary")),
        name="rwkv_scan",
    )(r, k, v, w, a, *params, s0)


GLA_SUB = 16


def _nt_dot(a, b):
    return lax.dot_general(a.astype(BF16), b.astype(BF16), (((1,), (1,)), ((), ())),
                           preferred_element_type=F32)


def _tn_dot(a, b):
    return lax.dot_general(a.astype(BF16), b.astype(BF16), (((0,), (0,)), ((), ())),
                           preferred_element_type=F32)


def _gla_kernel(q_ref, k_ref, v_ref, gl_ref, og_ref, w2_ref, gb_ref, ng_ref, s0_ref,
                o_ref, st_ref, s_scr, la_scr, att_scr):
    tblk = pl.program_id(2)
    tc = q_ref.shape[0]

    @pl.when(tblk == 0)
    def _():
        s_scr[...] = s0_ref[...].T

    la_scr[...] = _log_sigmoid(_dot3(gl_ref[...], w2_ref[...]) + gb_ref[...]) * (1.0 / GLA_GATE_NORM)
    rows = lax.broadcasted_iota(I32, (CHUNK, GLA_DK), 0)
    lane = lax.broadcasted_iota(I32, (GLA_SUB, CHUNK), 1)
    jrow = lax.broadcasted_iota(I32, (GLA_SUB, GLA_DK), 0)
    ng = ng_ref[...]
    nsub = CHUNK // GLA_SUB

    def chunk_body(c, carry):
        r0 = pl.multiple_of(c * CHUNK, CHUNK)
        rs = pl.ds(r0, CHUNK)
        b = la_scr[rs, :]
        sh = 1
        while sh < CHUNK:
            b = b + jnp.where(rows >= sh, pltpu.roll(b, sh, 0), 0.0)
            sh *= 2
        q = q_ref[rs, :] * (GLA_DK ** -0.5)
        k = k_ref[rs, :]
        v = v_ref[rs, :]
        s_old = s_scr[...]
        inter = _nt_dot(q * jnp.exp(b), s_old)
        b_last = b[CHUNK - 1:CHUNK, :]

        for jb in range(nsub):
            js = slice(jb * GLA_SUB, (jb + 1) * GLA_SUB)
            bj = b[js]
            kj = k[js]
            blk = jnp.zeros((GLA_SUB, CHUNK), F32)
            for ii in range(GLA_SUB):
                i_row = jb * GLA_SUB + ii
                e = jnp.exp(jnp.minimum(b[i_row:i_row + 1, :] - bj, 0.0))
                val = jnp.where(jrow <= ii, q[i_row:i_row + 1, :] * kj * e, 0.0)
                col = jnp.sum(val, axis=1, keepdims=True)
                blk = jnp.where(lane == i_row, col, blk)
            if jb + 1 < nsub:
                i0 = (jb + 1) * GLA_SUB
                be = b[i0 - 1:i0, :]
                kt = kj * jnp.exp(be - bj)
                later = rows >= i0
                qt = jnp.where(later, q * jnp.exp(jnp.where(later, b - be, 0.0)), 0.0)
                blk = blk + _nt_dot(kt, qt)
            att_scr[js, :] = blk
        o = inter + _tn_dot(att_scr[...], v)
        khat = k * jnp.exp(b_last - b)
        s_scr[...] = s_old * jnp.exp(b_last) + _tn_dot(v, khat)
        o = o * lax.rsqrt(jnp.mean(o * o, -1, keepdims=True) + GLA_NORM_EPS) * ng
        og = og_ref[rs, :]
        o_ref[rs, :] = (o * (og * _sigmoid(og))).astype(BF16)
        return carry
    lax.fori_loop(0, tc // CHUNK, chunk_body, 0)

    @pl.when(tblk == pl.num_programs(2) - 1)
    def _():
        st_ref[...] = s_scr[...].T


def _gla(proj, row0, nb, seq, w2p, gb, ng, s0, tc):
    ntb = seq // tc
    rb0 = row0 // tc
    cq, ck = _OFF["qc"] // GLA_DK, _OFF["kc"] // GLA_DK
    cv, cg = _OFF["vc"] // GLA_DV, _OFF["og"] // GLA_DV
    cgl = _OFF["gl"] // LANES

    def rows(c0, per_head):
        return lambda b, h, t: (rb0 + b * ntb + t, c0 + (h if per_head else 0))

    st = pl.BlockSpec((None, None, GLA_DK, GLA_DV), lambda b, h, t: (b, h, 0, 0))
    return pl.pallas_call(
        _gla_kernel,
        grid=(nb, GLA_HEADS, ntb),
        in_specs=[pl.BlockSpec((tc, GLA_DK), rows(cq, True)), pl.BlockSpec((tc, GLA_DK), rows(ck, True)),
                  pl.BlockSpec((tc, GLA_DV), rows(cv, True)), pl.BlockSpec((tc, LANES), rows(cgl, False)),
                  pl.BlockSpec((tc, GLA_DV), rows(cg, True)),
                  pl.BlockSpec((LANES, GLA_DK), lambda b, h, t: (0, h)),
                  pl.BlockSpec((1, GLA_DK), lambda b, h, t: (0, h)),
                  pl.BlockSpec((1, GLA_DV), lambda b, h, t: (0, h)), st],
        out_specs=[pl.BlockSpec((tc, GLA_DV), lambda b, h, t: (b * ntb + t, h)), st],
        out_shape=[jax.ShapeDtypeStruct((nb * seq, GLA_VW), BF16),
                   jax.ShapeDtypeStruct((nb, GLA_HEADS, GLA_DK, GLA_DV), F32)],
        scratch_shapes=[pltpu.VMEM((GLA_DV, GLA_DK), F32), pltpu.VMEM((tc, GLA_DK), F32),
                        pltpu.VMEM((CHUNK, CHUNK), F32)],
        compiler_params=_cparams(("parallel", "parallel", "arbitrary")),
        name="gla",
    )(proj, proj, proj, proj, proj, w2p, gb, ng, s0)


def _pack_cols(w):
    parts = []
    for _, s, wd, p in _SEGS:
        parts.append(w[..., s:s + wd])
        if p > wd:
            parts.append(jnp.zeros(w.shape[:-1] + (p - wd,), w.dtype))
    return jnp.concatenate(parts, axis=-1)


def _pack_rw(vec):
    r, k, v = vec[..., 0:1024], vec[..., 1024:2048], vec[..., 2048:3072]
    z = jnp.zeros(vec.shape[:-1] + (32,), vec.dtype)
    lr = jnp.concatenate([vec[..., 3072:3168], z, vec[..., 3168:3264], z, vec[..., 3264:3520]], -1)
    return r, k, v, lr


def _pad_rows(w, n):
    return jnp.concatenate([w, jnp.zeros((n - w.shape[0],) + w.shape[1:], w.dtype)], 0)


def _to_scan(x, nb, seq):
    return jnp.transpose(x.reshape(nb, seq, RW_HEADS, RW_HD), (1, 3, 0, 2)).reshape(seq, RW_HD, nb * RW_HEADS)


def _from_scan(o, nb, seq):
    return jnp.transpose(o.reshape(seq, RW_HD, nb, RW_HEADS), (2, 0, 3, 1)).reshape(nb * seq, RW_W)


def _group_mixers(proj, row0, nb, seq, caches, lw):
    m = nb * seq
    rows = slice(row0, row0 + m)
    cache_k, cache_v, cache_ki, s_rw, s_shift, s_gla = caches
    past = 0 if cache_k is None else cache_k.shape[1]
    l_real = past + seq

    ka = proj[rows, _OFF["ka"]:_OFF["ka"] + 256].reshape(nb, seq, 256)
    va = proj[rows, _OFF["va"]:_OFF["va"] + 256].reshape(nb, seq, 256)
    ki = proj[rows, _OFF["ki"]:_OFF["ki"] + LANES].reshape(nb, seq, LANES)
    if past:
        k_all = jnp.concatenate([cache_k.reshape(nb, past, 256), ka], 1)
        v_all = jnp.concatenate([cache_v.reshape(nb, past, 256), va], 1)
        cki = jnp.concatenate([cache_ki, jnp.zeros((nb, past, LANES - IDX_DIM), F32)], -1)
        ki_all = jnp.concatenate([cki, ki], 1)
    else:
        k_all, v_all, ki_all = ka, va, ki
    kblk = 512 if l_real % 512 == 0 else 384
    l_pad = -(-l_real // kblk) * kblk
    if l_pad > l_real:
        padk = lambda t: jnp.concatenate([t, jnp.zeros((nb, l_pad - l_real, t.shape[-1]), t.dtype)], 1)
        k_all, v_all, ki_all = padk(k_all), padk(v_all), padk(ki_all)
    k_bf = k_all.astype(BF16)
    vt_bf = jnp.swapaxes(v_all, 1, 2).astype(BF16)
    qb = Q_BLOCK
    if seq % qb == 0:
        nqb = seq // qb
        rb0 = row0 // qb
        mk = lambda c: (lambda b, i: (rb0 + b * nqb + i, c))
        q_ops = ((proj, mk(_OFF["qa"] // ATT_W)), (proj, mk(_OFF["qi"] // 1024)),
                 (proj, mk(_OFF["wi"] // LANES)))
        o_a = _dsa(q_ops, k_bf, vt_bf, ki_all, nb=nb, nqb=nqb, past=past, seq=seq,
                   l_real=l_real, kblk=kblk)
    else:
        nqb = 1

        def padq(c0, w):
            t = proj[rows, c0:c0 + w].reshape(nb, seq, w)
            return jnp.concatenate([t, jnp.zeros((nb, qb - seq, w), F32)], 1).reshape(nb * qb, w)
        mk = lambda b, i: (b, 0)
        q_ops = ((padq(_OFF["qa"], ATT_W), mk), (padq(_OFF["qi"], 1024), mk),
                 (padq(_OFF["wi"], LANES), mk))
        o_a = _dsa(q_ops, k_bf, vt_bf, ki_all, nb=nb, nqb=1, past=past, seq=seq,
                   l_real=l_real, kblk=kblk)
        o_a = o_a.reshape(nb, qb, ATT_W)[:, :seq].reshape(m, ATT_W)

    if s_shift is None:
        s_shift = jnp.zeros((nb, RW_PROJ), F32)
    shifts = [t.reshape(nb, 1, -1) for t in _pack_rw(s_shift)]
    tm = 256 if seq % 256 == 0 else seq
    r, k, v, w, a, g = _rw_prep(proj, row0, nb, seq, shifts, lw["rw_mus"], lw["rw_w0"], lw["rw_a0"],
                                lw["rw_w2p"], lw["rw_a2p"], lw["rw_g2"], tm)
    if s_rw is None:
        s0 = jnp.zeros((RW_HD, RW_HD, nb * RW_HEADS), F32)
    else:
        s0 = jnp.transpose(s_rw, (3, 2, 0, 1)).reshape(RW_HD, RW_HD, nb * RW_HEADS)
    tile = lambda p: jnp.tile(p.reshape(RW_HEADS, RW_HD).T, (1, LANES // RW_HEADS))
    params = [tile(lw[n]) for n in ("rw_kk", "rw_ka", "rw_rk", "rw_ln_g", "rw_ln_b")]
    tb = 32 if seq % 32 == 0 else seq
    o_b, s_t = _rw_scan(*[_to_scan(t, nb, seq) for t in (r, k, v, w, a)], params, s0, tb)
    o_b = _from_scan(o_b, nb, seq)
    s_rw_new = jnp.transpose(s_t.reshape(RW_HD, RW_HD, nb, RW_HEADS), (2, 3, 1, 0))
    last = proj[row0 + seq - 1:row0 + m:seq]
    shift_new = jnp.concatenate(
        [last[:, _OFF["r"]:_OFF["r"] + 3072], last[:, _OFF["xw"]:_OFF["xw"] + 96],
         last[:, _OFF["xa"]:_OFF["xa"] + 96], last[:, _OFF["xg"]:_OFF["xg"] + 256]], -1)

    if s_gla is None:
        s_gla = jnp.zeros((nb, GLA_HEADS, GLA_DK, GLA_DV), F32)
    tc = 512 if seq % 512 == 0 else seq
    o_c, s_gla_new = _gla(proj, row0, nb, seq, lw["gla_w2p"], lw["gla_b"], lw["gla_ng"], s_gla, tc)

    new_state = (ka.reshape(nb, seq, ATT_KV, ATT_HD), va.reshape(nb, seq, ATT_KV, ATT_HD),
                 ki[..., :IDX_DIM], s_rw_new, shift_new, s_gla_new)
    return o_a, o_b, g, o_c, new_state


def kernel(x_prompt, x_sample, cache_k, cache_v, cache_kidx, state_rwkv, state_rwkv_shift, state_gla,
           w_in, rw_mu, rw_w0, rw_w2, rw_a0, rw_a2, rw_g2, rw_kk, rw_ka, rw_rk, rw_ln_g, rw_ln_b,
           gla_w2, gla_b, gla_norm_g, w_branch_a, w_branch_b, w_branch_c, w_out,
           ln1_g, ln1_b, ln2_g, ln2_b, ffn_w1, ffn_w3, ffn_w2):
    bp, tp, _ = x_prompt.shape
    bs, ts, _ = x_sample.shape
    mp, ms = bp * tp, bs * ts
    x = jnp.concatenate([x_prompt.reshape(mp, D_MODEL), x_sample.reshape(ms, D_MODEL)], 0)
    xb = x.astype(BF16)
    m = mp + ms
    tm_big = 1024 if m % 1024 == 0 else 256
    tm_mid = 512 if m % 512 == 0 else 256

    w_in_p = _pack_cols(w_in).astype(BF16)
    new_p = [[] for _ in range(6)]
    new_s = [[] for _ in range(6)]
    for l in range(DEPTH):
        lw = {
            "rw_mus": [t.reshape(1, -1) for t in _pack_rw(rw_mu[l])],
            "rw_w0": rw_w0[l].reshape(1, RW_W), "rw_a0": rw_a0[l].reshape(1, RW_W),
            "rw_w2p": _pad_rows(rw_w2[l], 128), "rw_a2p": _pad_rows(rw_a2[l], 128), "rw_g2": rw_g2[l],
            "rw_kk": rw_kk[l], "rw_ka": rw_ka[l], "rw_rk": rw_rk[l],
            "rw_ln_g": rw_ln_g[l], "rw_ln_b": rw_ln_b[l],
            "gla_w2p": _pad_rows(gla_w2[l], LANES), "gla_b": gla_b[l].reshape(1, GLA_KW),
            "gla_ng": gla_norm_g[l].reshape(1, GLA_VW),
        }
        proj = _matmul(xb, w_in_p[l], tm_big, 1024)
        oa_p, ob_p, g_p, oc_p, st_p = _group_mixers(proj, 0, bp, tp, (None,) * 6, lw)
        oa_s, ob_s, g_s, oc_s, st_s = _group_mixers(
            proj, mp, bs, ts,
            (cache_k[l], cache_v[l], cache_kidx[l], state_rwkv[l], state_rwkv_shift[l], state_gla[l]), lw)
        cat = lambda u, v: jnp.concatenate([u, v], 0)
        merged = _merge(cat(oa_p, oa_s), cat(ob_p, ob_s), cat(g_p, g_s), cat(oc_p, oc_s), proj,
                        w_branch_a[l].astype(BF16), w_branch_b[l].astype(BF16),
                        w_branch_c[l].astype(BF16), tm_mid, 512)
        h, hb = _hproj(merged, w_out[l].astype(BF16), x, ln1_g[l].reshape(1, -1),
                       ln1_b[l].reshape(1, -1), tm_mid)
        x, xb = _ffn(hb, h, ffn_w1[l].astype(BF16), ffn_w3[l].astype(BF16), ffn_w2[l].astype(BF16),
                     ln2_g[l].reshape(1, -1), ln2_b[l].reshape(1, -1), tm_mid, 512)
        for i in range(6):
            new_p[i].append(st_p[i])
            new_s[i].append(st_s[i])
    sp = [jnp.stack(a) for a in new_p]
    ss = [jnp.stack(a) for a in new_s]
    yp = x[:mp].reshape(bp, tp, D_MODEL)
    ys = x[mp:].reshape(bs, ts, D_MODEL)
    return (yp, ys, sp[0], sp[1], sp[2], sp[3], sp[4], sp[5], ss[0], ss[1], ss[2], ss[3], ss[4], ss[5])
```

```python
import functools

import jax
import jax.numpy as jnp
from jax import lax
from jax.experimental import pallas as pl
from jax.experimental.pallas import tpu as pltpu

F32 = jnp.float32
BF16 = jnp.bfloat16
I32 = jnp.int32
I16 = jnp.int16

D_MODEL = 2048
DEPTH = 4
CHUNK = 64
Q_BLOCK = 128
ATT_HEADS = 8
ATT_HD = 128
ATT_KV = 2
ATT_REP = ATT_HEADS // ATT_KV
ATT_W = ATT_HEADS * ATT_HD
IDX_HEADS = 16
IDX_DIM = 64
IDX_TOPK = 256
IDX_SCALE = (IDX_DIM * IDX_HEADS) ** -0.5
RW_HEADS = 16
RW_HD = 64
RW_W = RW_HEADS * RW_HD
RW_DECAY_R = 96
RW_A_R = 96
RW_G_R = 256
RW_PROJ = 3 * RW_W + RW_DECAY_R + RW_A_R + RW_G_R
RW_DECAY_SCALE = 0.606531
RW_GN_EPS = 64e-5
GLA_HEADS = 4
GLA_DK = 128
GLA_DV = 256
GLA_KW = GLA_HEADS * GLA_DK
GLA_VW = GLA_HEADS * GLA_DV
GLA_GATE_R = 16
GLA_GATE_NORM = 16.0
GLA_NORM_EPS = 1e-5
FFN_HIDDEN = -(-8 * D_MODEL // (3 * 256)) * 256
DN_ALPHA = (2 * DEPTH) ** 0.25
LN_EPS = 1e-5
LOG2E = 1.4426950408889634

LANES = 128
SUBLANES = 8
PACKED_ROWS = 16
VMEM_LIMIT = 56 * 1024 * 1024

_SEGS = (
    ("qa", 0, 1024, 1024), ("qi", 1536, 1024, 1024), ("vc", 7184, 1024, 1024),
    ("og", 8224, 1024, 1024), ("gates", 9248, 6144, 6144),
    ("r", 2640, 1024, 1024), ("k", 3664, 1024, 1024), ("v", 4688, 1024, 1024),
    ("xw", 5712, 96, 128), ("xa", 5808, 96, 128), ("xg", 5904, 256, 256),
    ("qc", 6160, 512, 512), ("kc", 6672, 512, 512),
    ("ka", 1024, 256, 256), ("va", 1280, 256, 256),
    ("ki", 2560, 64, 128), ("wi", 2624, 16, 128), ("gl", 8208, 16, 128),
)
_OFF = {}
_o = 0
for _n, _s, _w, _p in _SEGS:
    _OFF[_n] = _o
    _o += _p
PROJ_W = _o
KEY_NEG_INF = -0x7F800000
MASK_DIST = 1e32
M_FLOOR = -1e29


def _cparams(sem):
    return pltpu.CompilerParams(dimension_semantics=sem, vmem_limit_bytes=VMEM_LIMIT)


def _split2(x):
    hi = x.astype(BF16)
    lo = (x - hi.astype(F32)).astype(BF16)
    return hi, lo


def _dot3(a, b):
    ah, al = _split2(a)
    bh, bl = _split2(b)
    d = functools.partial(jnp.dot, preferred_element_type=F32)
    return d(ah, bh) + (d(al, bh) + d(ah, bl))


def _nt_dot(a, b):
    return lax.dot_general(a.astype(BF16), b.astype(BF16), (((1,), (1,)), ((), ())),
                           preferred_element_type=F32)


def _tn_dot(a, b):
    return lax.dot_general(a.astype(BF16), b.astype(BF16), (((0,), (0,)), ((), ())),
                           preferred_element_type=F32)


def _sigmoid(x):
    return 1.0 / (1.0 + jnp.exp(-x))


def _log_sigmoid(x):
    return jnp.minimum(x, 0.0) - jnp.log(1.0 + jnp.exp(-jnp.abs(x)))


def _layer_norm(x, g, b):
    mu = jnp.mean(x, -1, keepdims=True)
    xc = x - mu
    var = jnp.mean(xc * xc, -1, keepdims=True)
    return xc * lax.rsqrt(var + LN_EPS) * g + b


def _mm_kernel(a_ref, b_ref, o_ref):
    o_ref[...] = jnp.dot(a_ref[...], b_ref[...], preferred_element_type=F32)


def _matmul(a, b, tm, tn):
    m, k = a.shape
    n = b.shape[1]
    return pl.pallas_call(
        _mm_kernel,
        grid=(m // tm, pl.cdiv(n, tn)),
        in_specs=[pl.BlockSpec((tm, k), lambda i, j: (i, 0)),
                  pl.BlockSpec((k, tn), lambda i, j: (0, j))],
        out_specs=pl.BlockSpec((tm, tn), lambda i, j: (i, j)),
        out_shape=jax.ShapeDtypeStruct((m, n), F32),
        compiler_params=_cparams(("parallel", "arbitrary")),
        name="in_proj",
    )(a, b)


def _merge_kernel(oa_ref, ob_ref, g_ref, oc_ref, ga_ref, gb_ref, gc_ref,
                  wa_ref, wb_ref, wc_ref, o_ref):
    ob = (ob_ref[...] * g_ref[...]).astype(BF16)
    d = functools.partial(jnp.dot, preferred_element_type=F32)
    acc = _sigmoid(ga_ref[...]) * d(oa_ref[...], wa_ref[...])
    acc += _sigmoid(gb_ref[...]) * d(ob, wb_ref[...])
    acc += _sigmoid(gc_ref[...]) * d(oc_ref[...], wc_ref[...])
    o_ref[...] = acc.astype(BF16)


def _merge(oa, ob, g, oc, proj, wa, wb, wc, tm, tn):
    m = oa.shape[0]
    gblk = _OFF["gates"] // tn
    nj = D_MODEL // tn
    row = lambda i, j: (i, 0)
    wspec = pl.BlockSpec((ATT_W, tn), lambda i, j: (0, j))
    return pl.pallas_call(
        _merge_kernel,
        grid=(m // tm, nj),
        in_specs=[pl.BlockSpec((tm, ATT_W), row), pl.BlockSpec((tm, RW_W), row),
                  pl.BlockSpec((tm, RW_W), row), pl.BlockSpec((tm, GLA_VW), row),
                  pl.BlockSpec((tm, tn), lambda i, j: (i, gblk + j)),
                  pl.BlockSpec((tm, tn), lambda i, j: (i, gblk + nj + j)),
                  pl.BlockSpec((tm, tn), lambda i, j: (i, gblk + 2 * nj + j)),
                  wspec, wspec, wspec],
        out_specs=pl.BlockSpec((tm, tn), lambda i, j: (i, j)),
        out_shape=jax.ShapeDtypeStruct((m, D_MODEL), BF16),
        compiler_params=_cparams(("parallel", "arbitrary")),
        name="merge",
    )(oa, ob, g, oc, proj, proj, proj, wa, wb, wc)


def _hproj_kernel(m_ref, w_ref, x_ref, g_ref, b_ref, h_ref, hb_ref):
    z = DN_ALPHA * x_ref[...] + jnp.dot(m_ref[...], w_ref[...], preferred_element_type=F32)
    h = _layer_norm(z, g_ref[...], b_ref[...])
    h_ref[...] = h
    hb_ref[...] = h.astype(BF16)


def _hproj(merged, w_out, x, g, b, tm):
    m = x.shape[0]
    row = lambda i: (i, 0)
    fix = lambda i: (0, 0)
    return pl.pallas_call(
        _hproj_kernel,
        grid=(m // tm,),
        in_specs=[pl.BlockSpec((tm, D_MODEL), row), pl.BlockSpec((D_MODEL, D_MODEL), fix),
                  pl.BlockSpec((tm, D_MODEL), row), pl.BlockSpec((1, D_MODEL), fix),
                  pl.BlockSpec((1, D_MODEL), fix)],
        out_specs=[pl.BlockSpec((tm, D_MODEL), row), pl.BlockSpec((tm, D_MODEL), row)],
        out_shape=[jax.ShapeDtypeStruct((m, D_MODEL), F32),
                   jax.ShapeDtypeStruct((m, D_MODEL), BF16)],
        compiler_params=_cparams(("parallel",)),
        name="out_proj_ln",
    )(merged, w_out, x, g, b)


def _ffn_kernel(hb_ref, h_ref, w1_ref, w3_ref, w2_ref, g_ref, b_ref, y_ref, yb_ref, acc_ref):
    j = pl.program_id(1)

    @pl.when(j == 0)
    def _():
        acc_ref[...] = jnp.zeros_like(acc_ref)

    hb = hb_ref[...]
    u1 = jnp.dot(hb, w1_ref[...], preferred_element_type=F32)
    u3 = jnp.dot(hb, w3_ref[...], preferred_element_type=F32)
    u = (u1 * _sigmoid(u1) * u3).astype(BF16)
    acc_ref[...] += jnp.dot(u, w2_ref[...], preferred_element_type=F32)

    @pl.when(j == pl.num_programs(1) - 1)
    def _():
        y = _layer_norm(DN_ALPHA * h_ref[...] + acc_ref[...], g_ref[...], b_ref[...])
        y_ref[...] = y
        yb_ref[...] = y.astype(BF16)


def _ffn(hb, h, w1, w3, w2, g, b, tm, th):
    m = h.shape[0]
    row = lambda i, j: (i, 0)
    fix = lambda i, j: (0, 0)
    return pl.pallas_call(
        _ffn_kernel,
        grid=(m // tm, FFN_HIDDEN // th),
        in_specs=[pl.BlockSpec((tm, D_MODEL), row), pl.BlockSpec((tm, D_MODEL), row),
                  pl.BlockSpec((D_MODEL, th), lambda i, j: (0, j)),
                  pl.BlockSpec((D_MODEL, th), lambda i, j: (0, j)),
                  pl.BlockSpec((th, D_MODEL), lambda i, j: (j, 0)),
                  pl.BlockSpec((1, D_MODEL), fix), pl.BlockSpec((1, D_MODEL), fix)],
        out_specs=[pl.BlockSpec((tm, D_MODEL), row), pl.BlockSpec((tm, D_MODEL), row)],
        out_shape=[jax.ShapeDtypeStruct((m, D_MODEL), F32),
                   jax.ShapeDtypeStruct((m, D_MODEL), BF16)],
        scratch_shapes=[pltpu.VMEM((tm, D_MODEL), F32)],
        compiler_params=_cparams(("parallel", "arbitrary")),
        name="ffn_ln",
    )(hb, h, w1, w3, w2, g, b)


ATT_SUB = 128


def _dsa_kernel(qa_ref, qi_ref, wi_ref, k_ref, vt_ref, ki_ref, o_ref,
                kcat_ref, keys_ref, hi_ref, lo_ref, wq_ref, qt_ref, m_ref, l_ref, acc_ref,
                lg_ref, pt_ref, dm_ref, al_ref, *, past, l_real, kblk, topk):
    qb = qa_ref.shape[0]
    l_pad = ki_ref.shape[0]
    i = pl.program_id(1)
    kf = float(topk)

    @pl.when(i == 0)
    def _():
        def prep(c, carry):
            r0 = pl.multiple_of(c * kblk, kblk)
            x = ki_ref[pl.ds(r0, kblk), :]
            hi = x.astype(BF16)
            lo = x - hi.astype(F32)
            a = (hi.astype(F32) + pltpu.roll(lo, IDX_DIM, 1)).astype(BF16)
            kcat_ref[pl.ds(r0, kblk), 0:LANES] = a
            kcat_ref[pl.ds(r0, kblk), LANES:2 * LANES] = hi
            return carry
        lax.fori_loop(0, l_pad // kblk, prep, 0)

    qit = qi_ref[...].T
    qhi = qit.astype(BF16)
    qlo = (qit - qhi.astype(F32)).astype(BF16)
    zero = jnp.zeros((IDX_DIM, qb), BF16)
    for h in range(IDX_HEADS):
        hs = slice(h * IDX_DIM, (h + 1) * IDX_DIM)
        cs = slice(h * qb, (h + 1) * qb)
        wq_ref[0:64, cs] = qhi[hs]
        wq_ref[64:128, cs] = qhi[hs]
        wq_ref[128:192, cs] = qlo[hs]
        wq_ref[192:256, cs] = zero
    wit = wi_ref[...].T
    qat = (qa_ref[...] * (ATT_HD ** -0.5 * LOG2E)).T.astype(BF16)
    for h in range(ATT_HEADS):
        g, r = divmod(h, ATT_REP)
        qt_ref[g, :, r * qb:(r + 1) * qb] = qat[h * ATT_HD:(h + 1) * ATT_HD]

    q0 = past + i * qb
    pos = q0 + lax.broadcasted_iota(I32, (1, qb), 1)
    pos_chunk = lax.shift_right_logical(pos, 6)
    adm_keys = jnp.minimum(l_real, (lax.shift_right_logical(q0 + qb - 1, 6) + 1) * CHUNK)
    nkc = lax.div(adm_keys + (kblk - 1), kblk)

    def rows_of(c):
        return pl.ds(pl.multiple_of(c * kblk, kblk), kblk)

    def key_index(c):
        return c * kblk + lax.broadcasted_iota(I32, (kblk, qb), 0)

    def score_body(c, carry):
        kc = kcat_ref[rows_of(c), :]
        acc = jnp.zeros((kblk, qb), F32)
        for hg in range(IDX_HEADS // 4):
            rel = jnp.dot(kc, wq_ref[:, hg * 4 * qb:(hg + 1) * 4 * qb],
                          preferred_element_type=F32)
            for hh in range(4):
                h = hg * 4 + hh
                acc = acc + jnp.maximum(rel[:, hh * qb:(hh + 1) * qb], 0.0) * wit[h:h + 1, :]
        score = acc * IDX_SCALE
        s_idx = key_index(c)
        adm = (lax.shift_right_logical(s_idx, 6) <= pos_chunk) & (s_idx < l_real)
        score = jnp.where(adm, score, -jnp.inf)
        bits = pltpu.bitcast(score, I32)
        key = jnp.where(bits < 0, -(bits & 0x7FFFFFFF), bits)
        keys_ref[rows_of(c), :] = key
        hi_ref[rows_of(c), :] = lax.shift_right_arithmetic(key, 16).astype(I16)
        return carry
    lax.fori_loop(0, nkc, score_body, 0)

    n_acc = 4
    one16 = jnp.ones((), I16)
    zero16 = jnp.zeros((), I16)

    def count16(ref, cand):
        c16 = cand.astype(I16)

        def body(c, accs):
            ind = jnp.where(ref[rows_of(c), :] >= c16, one16, zero16)
            accs = list(accs)
            for s in range(kblk // PACKED_ROWS):
                accs[s % n_acc] = accs[s % n_acc] + ind[s * PACKED_ROWS:(s + 1) * PACKED_ROWS]
            return tuple(accs)
        accs = lax.fori_loop(0, nkc, body, (jnp.zeros((PACKED_ROWS, qb), I16),) * n_acc)
        tot = (accs[0].astype(I32) + accs[1].astype(I32)) + (accs[2].astype(I32) + accs[3].astype(I32))
        return jnp.sum(tot, axis=0, keepdims=True).astype(F32)

    def search16(ref):
        def body(it, carry):
            ans, cnt = carry
            cand = ans + lax.shift_left(jnp.int32(1), 15 - it)
            c = count16(ref, cand)
            ok = c >= kf
            return jnp.where(ok, cand, ans), jnp.where(ok, c, cnt)
        start = jnp.full((1, qb), -32768, I32)
        return lax.fori_loop(0, 16, body, (start, jnp.full((1, qb), 1e9, F32)))

    thr_hi, _ = search16(hi_ref)

    def low_body(c, carry):
        key = keys_ref[rows_of(c), :]
        hi = lax.shift_right_arithmetic(key, 16)
        lo = (key & 0xFFFF) - 32768
        k2 = jnp.where(hi > thr_hi, 32767, jnp.where(hi < thr_hi, -32768, lo))
        lo_ref[rows_of(c), :] = k2.astype(I16)
        return carry
    lax.fori_loop(0, nkc, low_body, 0)
    thr_lo, n_ge = search16(lo_ref)
    thr = thr_hi * 65536 + (thr_lo + 32768)
    thr_eff = jnp.maximum(thr, KEY_NEG_INF + 1)

    tied = (n_ge > kf) & (thr > KEY_NEG_INF)

    @pl.when(jnp.max(jnp.where(tied, 1.0, 0.0)) > 0.0)
    def _():
        def count32(pred_fn):
            def body(c, acc):
                ind = jnp.where(pred_fn(keys_ref[rows_of(c), :], key_index(c)), 1.0, 0.0)
                return acc + jnp.sum(ind.reshape(kblk // SUBLANES, SUBLANES, qb), axis=0)
            acc = lax.fori_loop(0, nkc, body, jnp.zeros((SUBLANES, qb), F32))
            return jnp.sum(acc, axis=0, keepdims=True)
        need = kf - count32(lambda key, s: key > thr)
        nbits = int(l_pad - 1).bit_length()

        def idx_body(it, p):
            cand = p + lax.shift_left(jnp.int32(1), nbits - 1 - it)
            few = count32(lambda key, s: (key == thr) & (s < cand)) < need
            return jnp.where(few, cand, p)
        p_last = lax.fori_loop(0, nbits, idx_body, jnp.zeros((1, qb), I32))

        def demote(c, carry):
            key = keys_ref[rows_of(c), :]
            drop = tied & (key == thr) & (key_index(c) > p_last)
            keys_ref[rows_of(c), :] = jnp.where(drop, thr - 1, key)
            return carry
        lax.fori_loop(0, nkc, demote, 0)

    m_ref[...] = jnp.full(m_ref.shape, M_FLOOR, F32)
    l_ref[...] = jnp.zeros(l_ref.shape, F32)
    acc_ref[...] = jnp.zeros(acc_ref.shape, F32)
    slopes = [float(2.0 ** (-8.0 * (h + 1) / ATT_HEADS)) * LOG2E for h in range(ATT_HEADS)]
    nsub = kblk // ATT_SUB
    fold = lambda t, op: op(t.reshape(ATT_SUB // SUBLANES, SUBLANES, qb), axis=0)

    def logits(g, c):
        lg_ref[g] = jnp.dot(k_ref[rows_of(c), g * ATT_HD:(g + 1) * ATT_HD], qt_ref[g],
                            preferred_element_type=F32)

    def softmax_block(g):
        for r in range(ATT_REP):
            h = g * ATT_REP + r
            cs = slice(r * qb, (r + 1) * qb)
            mx = jnp.full((SUBLANES, qb), M_FLOOR, F32)
            for sb in range(nsub):
                rs = slice(sb * ATT_SUB, (sb + 1) * ATT_SUB)
                lgs = lg_ref[g, rs, cs] - slopes[h] * dm_ref[rs, :]
                lg_ref[g, rs, cs] = lgs
                mx = jnp.maximum(mx, fold(lgs, jnp.max))
            m_old = m_ref[g, :, cs]
            m_new = jnp.maximum(m_old, jnp.max(mx, axis=0, keepdims=True))
            sm = jnp.zeros((SUBLANES, qb), F32)
            for sb in range(nsub):
                rs = slice(sb * ATT_SUB, (sb + 1) * ATT_SUB)
                pr = jnp.exp2(lg_ref[g, rs, cs] - m_new)
                sm = sm + fold(pr, jnp.sum)
                pt_ref[g, rs, cs] = pr.astype(BF16)
            alpha = jnp.exp2(m_old - m_new)
            l_ref[g, :, cs] = alpha * l_ref[g, :, cs] + jnp.sum(sm, axis=0, keepdims=True)
            m_ref[g, :, cs] = m_new
            al_ref[g, :, cs] = alpha

    def weighted_values(g, c):
        pv = jnp.dot(vt_ref[g * ATT_HD:(g + 1) * ATT_HD, rows_of(c)], pt_ref[g],
                     preferred_element_type=F32)
        acc_ref[g] = acc_ref[g] * al_ref[g] + pv

    logits(0, 0)

    def att_body(c, carry):
        key = keys_ref[rows_of(c), :]
        dist = jnp.abs(pos - key_index(c)).astype(F32)
        dm_ref[...] = jnp.where(key >= thr_eff, dist, MASK_DIST)
        logits(1, c)
        softmax_block(0)
        weighted_values(0, c)
        logits(0, jnp.minimum(c + 1, nkc - 1))
        softmax_block(1)
        weighted_values(1, c)
        return carry
    lax.fori_loop(0, nkc, att_body, 0)

    for h in range(ATT_HEADS):
        g, r = divmod(h, ATT_REP)
        cs = slice(r * qb, (r + 1) * qb)
        ot = acc_ref[g, :, cs] / l_ref[g, :, cs]
        o_ref[:, h * ATT_HD:(h + 1) * ATT_HD] = ot.T.astype(BF16)


def _dsa(q_ops, k_all, vt_all, ki_all, *, nb, nqb, past, l_real, kblk):
    qb = Q_BLOCK
    l_pad = k_all.shape[1]
    topk = min(IDX_TOPK, l_real // 4)
    assert kblk >= topk and kblk % ATT_SUB == 0 and l_pad % kblk == 0
    (qa, qa_map), (qi, qi_map), (wi, wi_map) = q_ops
    kern = functools.partial(_dsa_kernel, past=past, l_real=l_real, kblk=kblk, topk=topk)
    return pl.pallas_call(
        kern,
        grid=(nb, nqb),
        in_specs=[pl.BlockSpec((qb, ATT_W), qa_map), pl.BlockSpec((qb, IDX_HEADS * IDX_DIM), qi_map),
                  pl.BlockSpec((qb, LANES), wi_map),
                  pl.BlockSpec((None, l_pad, ATT_KV * ATT_HD), lambda b, i: (b, 0, 0)),
                  pl.BlockSpec((None, ATT_KV * ATT_HD, l_pad), lambda b, i: (b, 0, 0)),
                  pl.BlockSpec((None, l_pad, LANES), lambda b, i: (b, 0, 0))],
        out_specs=pl.BlockSpec((qb, ATT_W), lambda b, i: (b * nqb + i, 0)),
        out_shape=jax.ShapeDtypeStruct((nb * nqb * qb, ATT_W), BF16),
        scratch_shapes=[pltpu.VMEM((l_pad, 2 * LANES), BF16),
                        pltpu.VMEM((l_pad, qb), I32),
                        pltpu.VMEM((l_pad, qb), I16),
                        pltpu.VMEM((l_pad, qb), I16),
                        pltpu.VMEM((2 * LANES, IDX_HEADS * qb), BF16),
                        pltpu.VMEM((ATT_KV, ATT_HD, ATT_REP * qb), BF16),
                        pltpu.VMEM((ATT_KV, 1, ATT_REP * qb), F32),
                        pltpu.VMEM((ATT_KV, 1, ATT_REP * qb), F32),
                        pltpu.VMEM((ATT_KV, ATT_HD, ATT_REP * qb), F32),
                        pltpu.VMEM((ATT_KV, kblk, ATT_REP * qb), F32),
                        pltpu.VMEM((ATT_KV, kblk, ATT_REP * qb), BF16),
                        pltpu.VMEM((kblk, qb), F32),
                        pltpu.VMEM((ATT_KV, 1, ATT_REP * qb), F32)],
        compiler_params=_cparams(("parallel", "arbitrary")),
        name="dsa",
    )(qa, qi, wi, k_all, vt_all, ki_all)


def _rw_prep_kernel(r_ref, k_ref, v_ref, lr_ref, hr_ref, hk_ref, hv_ref, hlr_ref,
                    sr_ref, sk_ref, sv_ref, slr_ref, mr_ref, mk_ref, mv_ref, mlr_ref,
                    w0_ref, a0_ref, w2_ref, a2_ref, g2_ref,
                    ro_ref, ko_ref, vo_ref, wo_ref, ao_ref, go_ref, *, tiles_per_seq):
    first_tile = (pl.program_id(0) % tiles_per_seq) == 0

    def mix(x_ref, halo_ref, shift_ref, mu_ref):
        x = x_ref[...]
        first = jnp.where(first_tile, shift_ref[...], halo_ref[SUBLANES - 1:SUBLANES, :])
        rows = lax.broadcasted_iota(I32, x.shape, 0)
        prev = jnp.where(rows == 0, first, pltpu.roll(x, 1, 0))
        return x + (prev - x) * mu_ref[...]

    ro_ref[...] = mix(r_ref, hr_ref, sr_ref, mr_ref)
    ko_ref[...] = mix(k_ref, hk_ref, sk_ref, mk_ref)
    vo_ref[...] = mix(v_ref, hv_ref, sv_ref, mv_ref)
    lr = mix(lr_ref, hlr_ref, slr_ref, mlr_ref)
    xw = lr[:, 0:128]
    xa = lr[:, 128:256]
    xg = lr[:, 256:512]
    wpre = w0_ref[...] + _dot3(jnp.tanh(xw), w2_ref[...])
    wo_ref[...] = jnp.exp(-RW_DECAY_SCALE * _sigmoid(wpre))
    ao_ref[...] = _sigmoid(a0_ref[...] + _dot3(xa, a2_ref[...]))
    go_ref[...] = _dot3(_sigmoid(xg), g2_ref[...])


def _rw_prep(proj, row0, nb, seq, shifts, mus, w0, a0, w2p, a2p, g2, tm):
    m = nb * seq
    tps = seq // tm
    rb0 = row0 // tm
    hb = tm // SUBLANES
    cr, ck, cv = _OFF["r"] // RW_W, _OFF["k"] // RW_W, _OFF["v"] // RW_W
    clr = _OFF["xw"] // 512

    def main(c):
        return lambda i: (rb0 + i, c)

    def halo(c):
        return lambda i: (jnp.maximum((rb0 + i) * hb - 1, 0), c)

    bidx = lambda i: (i // tps, 0, 0)
    fix = lambda i: (0, 0)
    wide = [pl.BlockSpec((tm, RW_W), main(cr)), pl.BlockSpec((tm, RW_W), main(ck)),
            pl.BlockSpec((tm, RW_W), main(cv)), pl.BlockSpec((tm, 512), main(clr)),
            pl.BlockSpec((SUBLANES, RW_W), halo(cr)), pl.BlockSpec((SUBLANES, RW_W), halo(ck)),
            pl.BlockSpec((SUBLANES, RW_W), halo(cv)), pl.BlockSpec((SUBLANES, 512), halo(clr))]
    sh = [pl.BlockSpec((None, 1, RW_W), bidx)] * 3 + [pl.BlockSpec((None, 1, 512), bidx)]
    mu = [pl.BlockSpec((1, RW_W), fix)] * 3 + [pl.BlockSpec((1, 512), fix)]
    par = [pl.BlockSpec((1, RW_W), fix), pl.BlockSpec((1, RW_W), fix),
           pl.BlockSpec((128, RW_W), fix), pl.BlockSpec((128, RW_W), fix),
           pl.BlockSpec((RW_G_R, RW_W), fix)]
    out = jax.ShapeDtypeStruct((m, RW_W), F32)
    return pl.pallas_call(
        functools.partial(_rw_prep_kernel, tiles_per_seq=tps),
        grid=(m // tm,),
        in_specs=wide + sh + mu + par,
        out_specs=[pl.BlockSpec((tm, RW_W), lambda i: (i, 0))] * 6,
        out_shape=[out] * 6,
        compiler_params=_cparams(("parallel",)),
        name="rwkv_prep",
    )(proj, proj, proj, proj, proj, proj, proj, proj, *shifts, *mus, w0, a0, w2p, a2p, g2)


def _rw_scan_kernel(r_ref, k_ref, v_ref, w_ref, a_ref, kkp_ref, kap_ref, rkp_ref,
                    lng_ref, lnb_ref, s0_ref, o_ref, st_ref, s_scr, tmp_scr):
    tb = pl.program_id(1)
    nsteps = r_ref.shape[0]

    @pl.when(tb == 0)
    def _():
        s_scr[...] = s0_ref[...]

    def step(t, carry):
        k = k_ref[t]
        a = a_ref[t]
        kk = k * kkp_ref[...]
        nrm = jnp.sqrt(jnp.sum(kk * kk, axis=0, keepdims=True))
        kk = kk / jnp.maximum(nrm, 1e-12)
        km = k * (1.0 + (a - 1.0) * kap_ref[...])
        tmp_scr[0] = kk
        tmp_scr[1] = kk * a
        tmp_scr[2] = km
        parts = [jnp.zeros((RW_HD, LANES), F32) for _ in range(2)]
        for kx in range(RW_HD):
            parts[kx % 2] = parts[kx % 2] + s_scr[kx] * tmp_scr[0, kx:kx + 1, :]
        sa = -(parts[0] + parts[1])
        v = v_ref[t]
        outs = [jnp.zeros((RW_HD, LANES), F32) for _ in range(2)]
        for kx in range(RW_HD):
            s_new = (s_scr[kx] * w_ref[t, kx:kx + 1, :] + sa * tmp_scr[1, kx:kx + 1, :]
                     + v * tmp_scr[2, kx:kx + 1, :])
            s_scr[kx] = s_new
            outs[kx % 2] = outs[kx % 2] + s_new * r_ref[t, kx:kx + 1, :]
        o = outs[0] + outs[1]
        mu = jnp.mean(o, axis=0, keepdims=True)
        oc = o - mu
        var = jnp.mean(oc * oc, axis=0, keepdims=True)
        on = oc * lax.rsqrt(var + RW_GN_EPS) * lng_ref[...] + lnb_ref[...]
        bonus = jnp.sum(r_ref[t] * tmp_scr[2] * rkp_ref[...], axis=0, keepdims=True) * v
        o_ref[t] = on + bonus
        return carry
    lax.fori_loop(0, nsteps, step, 0)

    @pl.when(tb == pl.num_programs(1) - 1)
    def _():
        st_ref[...] = s_scr[...]


def _rw_scan(r, k, v, w, a, params, s0, tb):
    seq, _, pairs = r.shape
    assert pairs % LANES == 0, "batch * RW_HEADS must fill whole lane groups"
    seqb = pl.BlockSpec((tb, RW_HD, LANES), lambda g, t: (t, 0, g))
    par = pl.BlockSpec((RW_HD, LANES), lambda g, t: (0, 0))
    st = pl.BlockSpec((RW_HD, RW_HD, LANES), lambda g, t: (0, 0, g))
    return pl.pallas_call(
        _rw_scan_kernel,
        grid=(pairs // LANES, seq // tb),
        in_specs=[seqb] * 5 + [par] * 5 + [st],
        out_specs=[seqb, st],
        out_shape=[jax.ShapeDtypeStruct((seq, RW_HD, pairs), F32),
                   jax.ShapeDtypeStruct((RW_HD, RW_HD, pairs), F32)],
        scratch_shapes=[pltpu.VMEM((RW_HD, RW_HD, LANES), F32),
                        pltpu.VMEM((3, RW_HD, LANES), F32)],
        compiler_params=_cparams(("parallel", "arbitrary")),
        name="rwkv_scan",
    )(r, k, v, w, a, *params, s0)


GLA_SUB = 16


def _gla_kernel(q_ref, k_ref, v_ref, gl_ref, og_ref, w2_ref, gb_ref, ng_ref, s0_ref,
                o_ref, st_ref, s_scr, la_scr, att_scr):
    tblk = pl.program_id(2)
    tc = q_ref.shape[0]

    @pl.when(tblk == 0)
    def _():
        s_scr[...] = s0_ref[...].T

    la_scr[...] = _log_sigmoid(_dot3(gl_ref[...], w2_ref[...]) + gb_ref[...]) * (1.0 / GLA_GATE_NORM)
    rows = lax.broadcasted_iota(I32, (CHUNK, GLA_DK), 0)
    lane = lax.broadcasted_iota(I32, (GLA_SUB, CHUNK), 1)
    jrow = lax.broadcasted_iota(I32, (GLA_SUB, GLA_DK), 0)
    ng = ng_ref[...]
    nsub = CHUNK // GLA_SUB

    def chunk_body(c, carry):
        r0 = pl.multiple_of(c * CHUNK, CHUNK)
        rs = pl.ds(r0, CHUNK)
        b = la_scr[rs, :]
        sh = 1
        while sh < CHUNK:
            b = b + jnp.where(rows >= sh, pltpu.roll(b, sh, 0), 0.0)
            sh *= 2
        q = q_ref[rs, :] * (GLA_DK ** -0.5)
        k = k_ref[rs, :]
        v = v_ref[rs, :]
        s_old = s_scr[...]
        inter = _nt_dot(q * jnp.exp(b), s_old)
        b_last = b[CHUNK - 1:CHUNK, :]

        for jb in range(nsub):
            js = slice(jb * GLA_SUB, (jb + 1) * GLA_SUB)
            bj = b[js]
            kj = k[js]
            blk = jnp.zeros((GLA_SUB, CHUNK), F32)
            for ii in range(GLA_SUB):
                i_row = jb * GLA_SUB + ii
                e = jnp.exp(jnp.minimum(b[i_row:i_row + 1, :] - bj, 0.0))
                val = jnp.where(jrow <= ii, q[i_row:i_row + 1, :] * kj * e, 0.0)
                col = jnp.sum(val, axis=1, keepdims=True)
                blk = jnp.where(lane == i_row, col, blk)
            if jb + 1 < nsub:
                i0 = (jb + 1) * GLA_SUB
                be = b[i0 - 1:i0, :]
                kt = kj * jnp.exp(be - bj)
                later = rows >= i0
                qt = jnp.where(later, q * jnp.exp(jnp.where(later, b - be, 0.0)), 0.0)
                blk = blk + _nt_dot(kt, qt)
            att_scr[js, :] = blk
        o = inter + _tn_dot(att_scr[...], v)
        khat = k * jnp.exp(b_last - b)
        s_scr[...] = s_old * jnp.exp(b_last) + _tn_dot(v, khat)
        o = o * lax.rsqrt(jnp.mean(o * o, -1, keepdims=True) + GLA_NORM_EPS) * ng
        og = og_ref[rs, :]
        o_ref[rs, :] = (o * (og * _sigmoid(og))).astype(BF16)
        return carry
    lax.fori_loop(0, tc // CHUNK, chunk_body, 0)

    @pl.when(tblk == pl.num_programs(2) - 1)
    def _():
        st_ref[...] = s_scr[...].T


def _gla(proj, tail, row0, nb, seq, w2p, gb, ng, s0, tc):
    ntb = seq // tc
    rb0 = row0 // tc
    cq, ck = _OFF["qc"] // GLA_DK, _OFF["kc"] // GLA_DK
    cv, cg = _OFF["vc"] // GLA_DV, _OFF["og"] // GLA_DV
    cgl = (_OFF["gl"] - _OFF["ka"]) // LANES

    def rows(c0, per_head):
        return lambda b, h, t: (rb0 + b * ntb + t, c0 + (h if per_head else 0))

    st = pl.BlockSpec((None, None, GLA_DK, GLA_DV), lambda b, h, t: (b, h, 0, 0))
    return pl.pallas_call(
        _gla_kernel,
        grid=(nb, GLA_HEADS, ntb),
        in_specs=[pl.BlockSpec((tc, GLA_DK), rows(cq, True)), pl.BlockSpec((tc, GLA_DK), rows(ck, True)),
                  pl.BlockSpec((tc, GLA_DV), rows(cv, True)), pl.BlockSpec((tc, LANES), rows(cgl, False)),
                  pl.BlockSpec((tc, GLA_DV), rows(cg, True)),
                  pl.BlockSpec((LANES, GLA_DK), lambda b, h, t: (0, h)),
                  pl.BlockSpec((1, GLA_DK), lambda b, h, t: (0, h)),
                  pl.BlockSpec((1, GLA_DV), lambda b, h, t: (0, h)), st],
        out_specs=[pl.BlockSpec((tc, GLA_DV), lambda b, h, t: (b * ntb + t, h)), st],
        out_shape=[jax.ShapeDtypeStruct((nb * seq, GLA_VW), BF16),
                   jax.ShapeDtypeStruct((nb, GLA_HEADS, GLA_DK, GLA_DV), F32)],
        scratch_shapes=[pltpu.VMEM((GLA_DV, GLA_DK), F32), pltpu.VMEM((tc, GLA_DK), F32),
                        pltpu.VMEM((CHUNK, CHUNK), F32)],
        compiler_params=_cparams(("parallel", "parallel", "arbitrary")),
        name="gla",
    )(proj, proj, proj, tail, proj, w2p, gb, ng, s0)


def _pack_cols(w):
    parts = []
    for _, s, wd, p in _SEGS:
        parts.append(w[..., s:s + wd])
        if p > wd:
            parts.append(jnp.zeros(w.shape[:-1] + (p - wd,), w.dtype))
    return jnp.concatenate(parts, axis=-1)


def _pack_rw(vec):
    r, k, v = vec[..., 0:1024], vec[..., 1024:2048], vec[..., 2048:3072]
    z = jnp.zeros(vec.shape[:-1] + (32,), vec.dtype)
    lr = jnp.concatenate([vec[..., 3072:3168], z, vec[..., 3168:3264], z, vec[..., 3264:3520]], -1)
    return r, k, v, lr


def _pad_rows(w, n):
    return jnp.concatenate([w, jnp.zeros((n - w.shape[0],) + w.shape[1:], w.dtype)], 0)


def _to_scan(x, nb, seq):
    return jnp.transpose(x.reshape(nb, seq, RW_HEADS, RW_HD), (1, 3, 0, 2)).reshape(seq, RW_HD, nb * RW_HEADS)


def _from_scan(o, nb, seq):
    return jnp.transpose(o.reshape(seq, RW_HD, nb, RW_HEADS), (2, 0, 3, 1)).reshape(nb * seq, RW_W)


def _group_mixers(proj, tail, row0, nb, seq, caches, lw):
    m = nb * seq
    cache_k, cache_v, cache_ki, s_rw, s_shift, s_gla = caches
    past = 0 if cache_k is None else cache_k.shape[1]
    l_real = past + seq
    t0 = _OFF["ka"]
    grp = tail[row0:row0 + m].reshape(nb, seq, -1)

    ka = grp[..., _OFF["ka"] - t0:_OFF["ka"] - t0 + 256]
    va = grp[..., _OFF["va"] - t0:_OFF["va"] - t0 + 256]
    ki = grp[..., _OFF["ki"] - t0:_OFF["ki"] - t0 + LANES]
    if past:
        k_all = jnp.concatenate([cache_k.reshape(nb, past, 256), ka], 1)
        v_all = jnp.concatenate([cache_v.reshape(nb, past, 256), va], 1)
        cki = jnp.concatenate([cache_ki, jnp.zeros((nb, past, LANES - IDX_DIM), F32)], -1)
        ki_all = jnp.concatenate([cki, ki], 1)
    else:
        k_all, v_all, ki_all = ka, va, ki
    kblk = 512 if l_real % 512 == 0 else 384
    l_pad = -(-l_real // kblk) * kblk
    if l_pad > l_real:
        padk = lambda t: jnp.concatenate([t, jnp.zeros((nb, l_pad - l_real, t.shape[-1]), t.dtype)], 1)
        k_all, v_all, ki_all = padk(k_all), padk(v_all), padk(ki_all)
    k_bf = k_all.astype(BF16)
    vt_bf = jnp.swapaxes(v_all, 1, 2).astype(BF16)
    qb = Q_BLOCK
    if seq % qb == 0:
        nqb = seq // qb
        rb0 = row0 // qb
        mk = lambda c: (lambda b, i: (rb0 + b * nqb + i, c))
        q_ops = ((proj, mk(_OFF["qa"] // ATT_W)), (proj, mk(_OFF["qi"] // 1024)),
                 (tail, mk((_OFF["wi"] - t0) // LANES)))
        o_a = _dsa(q_ops, k_bf, vt_bf, ki_all, nb=nb, nqb=nqb, past=past, l_real=l_real, kblk=kblk)
    else:
        def padq(t):
            return jnp.concatenate([t, jnp.zeros((nb, qb - seq, t.shape[-1]), F32)], 1).reshape(nb * qb, -1)
        qrows = proj[row0:row0 + m, 0:_OFF["vc"]].reshape(nb, seq, -1)
        mk = lambda b, i: (b, 0)
        q_ops = ((padq(qrows[..., 0:ATT_W]), mk), (padq(qrows[..., ATT_W:]), mk),
                 (padq(grp[..., _OFF["wi"] - t0:_OFF["wi"] - t0 + LANES]), mk))
        o_a = _dsa(q_ops, k_bf, vt_bf, ki_all, nb=nb, nqb=1, past=past, l_real=l_real, kblk=kblk)
        o_a = o_a.reshape(nb, qb, ATT_W)[:, :seq].reshape(m, ATT_W)

    if s_shift is None:
        s_shift = jnp.zeros((nb, RW_PROJ), F32)
    shifts = [t.reshape(nb, 1, -1) for t in _pack_rw(s_shift)]
    tm = 256 if seq % 256 == 0 else seq
    r, k, v, w, a, g = _rw_prep(proj, row0, nb, seq, shifts, lw["rw_mus"], lw["rw_w0"], lw["rw_a0"],
                                lw["rw_w2p"], lw["rw_a2p"], lw["rw_g2"], tm)
    if s_rw is None:
        s0 = jnp.zeros((RW_HD, RW_HD, nb * RW_HEADS), F32)
    else:
        s0 = jnp.transpose(s_rw, (3, 2, 0, 1)).reshape(RW_HD, RW_HD, nb * RW_HEADS)
    tile = lambda p: jnp.tile(p.reshape(RW_HEADS, RW_HD).T, (1, LANES // RW_HEADS))
    params = [tile(lw[n]) for n in ("rw_kk", "rw_ka", "rw_rk", "rw_ln_g", "rw_ln_b")]
    tb = 32 if seq % 32 == 0 else seq
    o_b, s_t = _rw_scan(*[_to_scan(t, nb, seq) for t in (r, k, v, w, a)], params, s0, tb)
    o_b = _from_scan(o_b, nb, seq)
    s_rw_new = jnp.transpose(s_t.reshape(RW_HD, RW_HD, nb, RW_HEADS), (2, 3, 1, 0))
    last = jnp.take(proj, row0 + seq - 1 + seq * jnp.arange(nb), axis=0)[:, _OFF["r"]:_OFF["qc"]]
    base = _OFF["r"]
    shift_new = jnp.concatenate(
        [last[:, 0:3072], last[:, _OFF["xw"] - base:_OFF["xw"] - base + 96],
         last[:, _OFF["xa"] - base:_OFF["xa"] - base + 96], last[:, _OFF["xg"] - base:]], -1)

    if s_gla is None:
        s_gla = jnp.zeros((nb, GLA_HEADS, GLA_DK, GLA_DV), F32)
    tc = 512 if seq % 512 == 0 else seq
    o_c, s_gla_new = _gla(proj, tail, row0, nb, seq, lw["gla_w2p"], lw["gla_b"], lw["gla_ng"], s_gla, tc)

    new_state = (ka.reshape(nb, seq, ATT_KV, ATT_HD), va.reshape(nb, seq, ATT_KV, ATT_HD),
                 ki[..., :IDX_DIM], s_rw_new, shift_new, s_gla_new)
    return o_a, o_b, g, o_c, new_state


def kernel(x_prompt, x_sample, cache_k, cache_v, cache_kidx, state_rwkv, state_rwkv_shift, state_gla,
           w_in, rw_mu, rw_w0, rw_w2, rw_a0, rw_a2, rw_g2, rw_kk, rw_ka, rw_rk, rw_ln_g, rw_ln_b,
           gla_w2, gla_b, gla_norm_g, w_branch_a, w_branch_b, w_branch_c, w_out,
           ln1_g, ln1_b, ln2_g, ln2_b, ffn_w1, ffn_w3, ffn_w2):
    bp, tp, _ = x_prompt.shape
    bs, ts, _ = x_sample.shape
    mp, ms = bp * tp, bs * ts
    x = jnp.concatenate([x_prompt.reshape(mp, D_MODEL), x_sample.reshape(ms, D_MODEL)], 0)
    xb = x.astype(BF16)
    m = mp + ms
    tm_big = 1024 if m % 1024 == 0 else 256
    tm_mid = 512 if m % 512 == 0 else 256

    w_in_p = _pack_cols(w_in).astype(BF16)
    new_p = [[] for _ in range(6)]
    new_s = [[] for _ in range(6)]
    for l in range(DEPTH):
        lw = {
            "rw_mus": [t.reshape(1, -1) for t in _pack_rw(rw_mu[l])],
            "rw_w0": rw_w0[l].reshape(1, RW_W), "rw_a0": rw_a0[l].reshape(1, RW_W),
            "rw_w2p": _pad_rows(rw_w2[l], 128), "rw_a2p": _pad_rows(rw_a2[l], 128), "rw_g2": rw_g2[l],
            "rw_kk": rw_kk[l], "rw_ka": rw_ka[l], "rw_rk": rw_rk[l],
            "rw_ln_g": rw_ln_g[l], "rw_ln_b": rw_ln_b[l],
            "gla_w2p": _pad_rows(gla_w2[l], LANES), "gla_b": gla_b[l].reshape(1, GLA_KW),
            "gla_ng": gla_norm_g[l].reshape(1, GLA_VW),
        }
        proj = _matmul(xb, w_in_p[l, :, :_OFF["ka"]], tm_big, 1024)
        tail = _matmul(xb, w_in_p[l, :, _OFF["ka"]:], tm_big, PROJ_W - _OFF["ka"])
        oa_p, ob_p, g_p, oc_p, st_p = _group_mixers(proj, tail, 0, bp, tp, (None,) * 6, lw)
        oa_s, ob_s, g_s, oc_s, st_s = _group_mixers(
            proj, tail, mp, bs, ts,
            (cache_k[l], cache_v[l], cache_kidx[l], state_rwkv[l], state_rwkv_shift[l], state_gla[l]), lw)
        cat = lambda u, v: jnp.concatenate([u, v], 0)
        merged = _merge(cat(oa_p, oa_s), cat(ob_p, ob_s), cat(g_p, g_s), cat(oc_p, oc_s), proj,
                        w_branch_a[l].astype(BF16), w_branch_b[l].astype(BF16),
                        w_branch_c[l].astype(BF16), tm_mid, 512)
        h, hb = _hproj(merged, w_out[l].astype(BF16), x, ln1_g[l].reshape(1, -1),
                       ln1_b[l].reshape(1, -1), tm_mid)
        x, xb = _ffn(hb, h, ffn_w1[l].astype(BF16), ffn_w3[l].astype(BF16), ffn_w2[l].astype(BF16),
                     ln2_g[l].reshape(1, -1), ln2_b[l].reshape(1, -1), tm_mid, 512)
        for i in range(6):
            new_p[i].append(st_p[i])
            new_s[i].append(st_s[i])
    sp = [jnp.stack(a) for a in new_p]
    ss = [jnp.stack(a) for a in new_s]
    yp = x[:mp].reshape(bp, tp, D_MODEL)
    ys = x[mp:].reshape(bs, ts, D_MODEL)
    return (yp, ys, sp[0], sp[1], sp[2], sp[3], sp[4], sp[5], ss[0], ss[1], ss[2], ss[3], ss[4], ss[5])
```

```python
import functools

import jax
import jax.numpy as jnp
from jax import lax
from jax.experimental import pallas as pl
from jax.experimental.pallas import tpu as pltpu

F32 = jnp.float32
BF16 = jnp.bfloat16
I32 = jnp.int32
I16 = jnp.int16

D_MODEL = 2048
DEPTH = 4
CHUNK = 64
Q_BLOCK = 128
ATT_HEADS = 8
ATT_HD = 128
ATT_KV = 2
ATT_REP = ATT_HEADS // ATT_KV
ATT_W = ATT_HEADS * ATT_HD
IDX_HEADS = 16
IDX_DIM = 64
IDX_TOPK = 256
IDX_SCALE = (IDX_DIM * IDX_HEADS) ** -0.5
RW_HEADS = 16
RW_HD = 64
RW_W = RW_HEADS * RW_HD
RW_DECAY_R = 96
RW_A_R = 96
RW_G_R = 256
RW_PROJ = 3 * RW_W + RW_DECAY_R + RW_A_R + RW_G_R
RW_DECAY_SCALE = 0.606531
RW_GN_EPS = 64e-5
GLA_HEADS = 4
GLA_DK = 128
GLA_DV = 256
GLA_KW = GLA_HEADS * GLA_DK
GLA_VW = GLA_HEADS * GLA_DV
GLA_GATE_R = 16
GLA_GATE_NORM = 16.0
GLA_NORM_EPS = 1e-5
FFN_HIDDEN = -(-8 * D_MODEL // (3 * 256)) * 256
DN_ALPHA = (2 * DEPTH) ** 0.25
LN_EPS = 1e-5
LOG2E = 1.4426950408889634

LANES = 128
SUBLANES = 8
PACKED_ROWS = 16
VMEM_LIMIT = 56 * 1024 * 1024

_SEGS = (
    ("qa", 0, 1024, 1024), ("qi", 1536, 1024, 1024), ("vc", 7184, 1024, 1024),
    ("og", 8224, 1024, 1024), ("gates", 9248, 6144, 6144),
    ("r", 2640, 1024, 1024), ("k", 3664, 1024, 1024), ("v", 4688, 1024, 1024),
    ("xw", 5712, 96, 128), ("xa", 5808, 96, 128), ("xg", 5904, 256, 256),
    ("qc", 6160, 512, 512), ("kc", 6672, 512, 512),
    ("ka", 1024, 256, 256), ("va", 1280, 256, 256),
    ("ki", 2560, 64, 128), ("wi", 2624, 16, 128), ("gl", 8208, 16, 128),
)
_OFF = {}
_o = 0
for _n, _s, _w, _p in _SEGS:
    _OFF[_n] = _o
    _o += _p
PROJ_W = _o
KEY_NEG_INF = -0x7F800000
MASK_DIST = 1e32
M_FLOOR = -1e29


def _cparams(sem):
    return pltpu.CompilerParams(dimension_semantics=sem, vmem_limit_bytes=VMEM_LIMIT)


def _split2(x):
    hi = x.astype(BF16)
    lo = (x - hi.astype(F32)).astype(BF16)
    return hi, lo


def _dot3(a, b):
    ah, al = _split2(a)
    bh, bl = _split2(b)
    d = functools.partial(jnp.dot, preferred_element_type=F32)
    return d(ah, bh) + (d(al, bh) + d(ah, bl))


def _nt_dot(a, b):
    return lax.dot_general(a.astype(BF16), b.astype(BF16), (((1,), (1,)), ((), ())),
                           preferred_element_type=F32)


def _tn_dot(a, b):
    return lax.dot_general(a.astype(BF16), b.astype(BF16), (((0,), (0,)), ((), ())),
                           preferred_element_type=F32)


def _sigmoid(x):
    return 1.0 / (1.0 + jnp.exp(-x))


def _log_sigmoid(x):
    return jnp.minimum(x, 0.0) - jnp.log(1.0 + jnp.exp(-jnp.abs(x)))


def _layer_norm(x, g, b):
    mu = jnp.mean(x, -1, keepdims=True)
    xc = x - mu
    var = jnp.mean(xc * xc, -1, keepdims=True)
    return xc * lax.rsqrt(var + LN_EPS) * g + b


def _mm_kernel(a_ref, b_ref, o_ref):
    o_ref[...] = jnp.dot(a_ref[...], b_ref[...], preferred_element_type=F32)


def _matmul(a, b, tm, tn):
    m, k = a.shape
    n = b.shape[1]
    return pl.pallas_call(
        _mm_kernel,
        grid=(m // tm, pl.cdiv(n, tn)),
        in_specs=[pl.BlockSpec((tm, k), lambda i, j: (i, 0)),
                  pl.BlockSpec((k, tn), lambda i, j: (0, j))],
        out_specs=pl.BlockSpec((tm, tn), lambda i, j: (i, j)),
        out_shape=jax.ShapeDtypeStruct((m, n), F32),
        compiler_params=_cparams(("parallel", "arbitrary")),
        name="in_proj",
    )(a, b)


def _merge_kernel(oa_ref, ob_ref, g_ref, oc_ref, ga_ref, gb_ref, gc_ref,
                  wa_ref, wb_ref, wc_ref, o_ref):
    ob = (ob_ref[...] * g_ref[...]).astype(BF16)
    d = functools.partial(jnp.dot, preferred_element_type=F32)
    acc = _sigmoid(ga_ref[...]) * d(oa_ref[...], wa_ref[...])
    acc += _sigmoid(gb_ref[...]) * d(ob, wb_ref[...])
    acc += _sigmoid(gc_ref[...]) * d(oc_ref[...], wc_ref[...])
    o_ref[...] = acc.astype(BF16)


def _merge(oa, ob, g, oc, proj, wa, wb, wc, tm, tn):
    m = oa.shape[0]
    gblk = _OFF["gates"] // tn
    nj = D_MODEL // tn
    row = lambda i, j: (i, 0)
    wspec = pl.BlockSpec((ATT_W, tn), lambda i, j: (0, j))
    return pl.pallas_call(
        _merge_kernel,
        grid=(m // tm, nj),
        in_specs=[pl.BlockSpec((tm, ATT_W), row), pl.BlockSpec((tm, RW_W), row),
                  pl.BlockSpec((tm, RW_W), row), pl.BlockSpec((tm, GLA_VW), row),
                  pl.BlockSpec((tm, tn), lambda i, j: (i, gblk + j)),
                  pl.BlockSpec((tm, tn), lambda i, j: (i, gblk + nj + j)),
                  pl.BlockSpec((tm, tn), lambda i, j: (i, gblk + 2 * nj + j)),
                  wspec, wspec, wspec],
        out_specs=pl.BlockSpec((tm, tn), lambda i, j: (i, j)),
        out_shape=jax.ShapeDtypeStruct((m, D_MODEL), BF16),
        compiler_params=_cparams(("parallel", "arbitrary")),
        name="merge",
    )(oa, ob, g, oc, proj, proj, proj, wa, wb, wc)


def _hproj_kernel(m_ref, w_ref, x_ref, g_ref, b_ref, h_ref, hb_ref):
    z = DN_ALPHA * x_ref[...] + jnp.dot(m_ref[...], w_ref[...], preferred_element_type=F32)
    h = _layer_norm(z, g_ref[...], b_ref[...])
    h_ref[...] = h
    hb_ref[...] = h.astype(BF16)


def _hproj(merged, w_out, x, g, b, tm):
    m = x.shape[0]
    row = lambda i: (i, 0)
    fix = lambda i: (0, 0)
    return pl.pallas_call(
        _hproj_kernel,
        grid=(m // tm,),
        in_specs=[pl.BlockSpec((tm, D_MODEL), row), pl.BlockSpec((D_MODEL, D_MODEL), fix),
                  pl.BlockSpec((tm, D_MODEL), row), pl.BlockSpec((1, D_MODEL), fix),
                  pl.BlockSpec((1, D_MODEL), fix)],
        out_specs=[pl.BlockSpec((tm, D_MODEL), row), pl.BlockSpec((tm, D_MODEL), row)],
        out_shape=[jax.ShapeDtypeStruct((m, D_MODEL), F32),
                   jax.ShapeDtypeStruct((m, D_MODEL), BF16)],
        compiler_params=_cparams(("parallel",)),
        name="out_proj_ln",
    )(merged, w_out, x, g, b)


def _ffn_kernel(hb_ref, h_ref, w1_ref, w3_ref, w2_ref, g_ref, b_ref, y_ref, yb_ref, acc_ref):
    j = pl.program_id(1)

    @pl.when(j == 0)
    def _():
        acc_ref[...] = jnp.zeros_like(acc_ref)

    hb = hb_ref[...]
    u1 = jnp.dot(hb, w1_ref[...], preferred_element_type=F32)
    u3 = jnp.dot(hb, w3_ref[...], preferred_element_type=F32)
    u = (u1 * _sigmoid(u1) * u3).astype(BF16)
    acc_ref[...] += jnp.dot(u, w2_ref[...], preferred_element_type=F32)

    @pl.when(j == pl.num_programs(1) - 1)
    def _():
        y = _layer_norm(DN_ALPHA * h_ref[...] + acc_ref[...], g_ref[...], b_ref[...])
        y_ref[...] = y
        yb_ref[...] = y.astype(BF16)


def _ffn(hb, h, w1, w3, w2, g, b, tm, th):
    m = h.shape[0]
    row = lambda i, j: (i, 0)
    fix = lambda i, j: (0, 0)
    return pl.pallas_call(
        _ffn_kernel,
        grid=(m // tm, FFN_HIDDEN // th),
        in_specs=[pl.BlockSpec((tm, D_MODEL), row), pl.BlockSpec((tm, D_MODEL), row),
                  pl.BlockSpec((D_MODEL, th), lambda i, j: (0, j)),
                  pl.BlockSpec((D_MODEL, th), lambda i, j: (0, j)),
                  pl.BlockSpec((th, D_MODEL), lambda i, j: (j, 0)),
                  pl.BlockSpec((1, D_MODEL), fix), pl.BlockSpec((1, D_MODEL), fix)],
        out_specs=[pl.BlockSpec((tm, D_MODEL), row), pl.BlockSpec((tm, D_MODEL), row)],
        out_shape=[jax.ShapeDtypeStruct((m, D_MODEL), F32),
                   jax.ShapeDtypeStruct((m, D_MODEL), BF16)],
        scratch_shapes=[pltpu.VMEM((tm, D_MODEL), F32)],
        compiler_params=_cparams(("parallel", "arbitrary")),
        name="ffn_ln",
    )(hb, h, w1, w3, w2, g, b)


ATT_SUB = 128
DSA_QB = 256


def _dsa_kernel(qa_ref, qi_ref, wi_ref, k_ref, vt_ref, ki_ref, o_ref,
                kcat_ref, keys_ref, hi_ref, lo_ref, wq_ref, qt_ref, m_ref, l_ref, acc_ref,
                lg_ref, pt_ref, dm_ref, al_ref, *, past, l_real, kblk, topk):
    qb = qa_ref.shape[0]
    l_pad = ki_ref.shape[0]
    i = pl.program_id(1)
    kf = float(topk)

    @pl.when(i == 0)
    def _():
        def prep(c, carry):
            r0 = pl.multiple_of(c * kblk, kblk)
            x = ki_ref[pl.ds(r0, kblk), :]
            hi = x.astype(BF16)
            lo = x - hi.astype(F32)
            a = (hi.astype(F32) + pltpu.roll(lo, IDX_DIM, 1)).astype(BF16)
            kcat_ref[pl.ds(r0, kblk), 0:LANES] = a
            kcat_ref[pl.ds(r0, kblk), LANES:2 * LANES] = hi
            return carry
        lax.fori_loop(0, l_pad // kblk, prep, 0)

    qit = qi_ref[...].T
    qhi = qit.astype(BF16)
    qlo = (qit - qhi.astype(F32)).astype(BF16)
    zero = jnp.zeros((IDX_DIM, qb), BF16)
    for h in range(IDX_HEADS):
        hs = slice(h * IDX_DIM, (h + 1) * IDX_DIM)
        cs = slice(h * qb, (h + 1) * qb)
        wq_ref[0:64, cs] = qhi[hs]
        wq_ref[64:128, cs] = qhi[hs]
        wq_ref[128:192, cs] = qlo[hs]
        wq_ref[192:256, cs] = zero
    wit = wi_ref[...].T
    qat = (qa_ref[...] * (ATT_HD ** -0.5 * LOG2E)).T.astype(BF16)
    for h in range(ATT_HEADS):
        g, r = divmod(h, ATT_REP)
        qt_ref[g, :, r * qb:(r + 1) * qb] = qat[h * ATT_HD:(h + 1) * ATT_HD]

    q0 = past + i * qb
    pos = q0 + lax.broadcasted_iota(I32, (1, qb), 1)
    pos_chunk = lax.shift_right_logical(pos, 6)
    adm_keys = jnp.minimum(l_real, (lax.shift_right_logical(q0 + qb - 1, 6) + 1) * CHUNK)
    nkc = lax.div(adm_keys + (kblk - 1), kblk)

    def rows_of(c):
        return pl.ds(pl.multiple_of(c * kblk, kblk), kblk)

    def key_index(c):
        return c * kblk + lax.broadcasted_iota(I32, (kblk, qb), 0)

    def score_body(c, carry):
        kc = kcat_ref[rows_of(c), :]
        acc = jnp.zeros((kblk, qb), F32)
        for hg in range(IDX_HEADS // 4):
            rel = jnp.dot(kc, wq_ref[:, hg * 4 * qb:(hg + 1) * 4 * qb],
                          preferred_element_type=F32)
            for hh in range(4):
                h = hg * 4 + hh
                acc = acc + jnp.maximum(rel[:, hh * qb:(hh + 1) * qb], 0.0) * wit[h:h + 1, :]
        score = acc * IDX_SCALE
        s_idx = key_index(c)
        adm = (lax.shift_right_logical(s_idx, 6) <= pos_chunk) & (s_idx < l_real)
        score = jnp.where(adm, score, -jnp.inf)
        bits = pltpu.bitcast(score, I32)
        key = jnp.where(bits < 0, -(bits & 0x7FFFFFFF), bits)
        keys_ref[rows_of(c), :] = key
        hi_ref[rows_of(c), :] = lax.shift_right_arithmetic(key, 16).astype(I16)
        return carry
    lax.fori_loop(0, nkc, score_body, 0)

    n_acc = 4
    one16 = jnp.ones((), I16)
    zero16 = jnp.zeros((), I16)

    def count16(ref, cand):
        c16 = cand.astype(I16)

        def body(c, accs):
            ind = jnp.where(ref[rows_of(c), :] >= c16, one16, zero16)
            accs = list(accs)
            for s in range(kblk // PACKED_ROWS):
                accs[s % n_acc] = accs[s % n_acc] + ind[s * PACKED_ROWS:(s + 1) * PACKED_ROWS]
            return tuple(accs)
        accs = lax.fori_loop(0, nkc, body, (jnp.zeros((PACKED_ROWS, qb), I16),) * n_acc)
        tot = (accs[0].astype(I32) + accs[1].astype(I32)) + (accs[2].astype(I32) + accs[3].astype(I32))
        return jnp.sum(tot, axis=0, keepdims=True).astype(F32)

    def search16(ref):
        ans = jnp.full((1, qb), -32768, I32)
        cnt = jnp.full((1, qb), 1e9, F32)
        for bit in range(15, -1, -1):
            cand = ans + (1 << bit)
            c = count16(ref, cand)
            ok = c >= kf
            ans = jnp.where(ok, cand, ans)
            cnt = jnp.where(ok, c, cnt)
        return ans, cnt

    thr_hi, _ = search16(hi_ref)

    def low_body(c, carry):
        key = keys_ref[rows_of(c), :]
        hi = lax.shift_right_arithmetic(key, 16)
        lo = (key & 0xFFFF) - 32768
        k2 = jnp.where(hi > thr_hi, 32767, jnp.where(hi < thr_hi, -32768, lo))
        lo_ref[rows_of(c), :] = k2.astype(I16)
        return carry
    lax.fori_loop(0, nkc, low_body, 0)
    thr_lo, n_ge = search16(lo_ref)
    thr = thr_hi * 65536 + (thr_lo + 32768)
    thr_eff = jnp.maximum(thr, KEY_NEG_INF + 1)

    tied = (n_ge > kf) & (thr > KEY_NEG_INF)

    @pl.when(jnp.max(jnp.where(tied, 1.0, 0.0)) > 0.0)
    def _():
        def count32(pred_fn):
            def body(c, acc):
                ind = jnp.where(pred_fn(keys_ref[rows_of(c), :], key_index(c)), 1.0, 0.0)
                return acc + jnp.sum(ind.reshape(kblk // SUBLANES, SUBLANES, qb), axis=0)
            acc = lax.fori_loop(0, nkc, body, jnp.zeros((SUBLANES, qb), F32))
            return jnp.sum(acc, axis=0, keepdims=True)
        need = kf - count32(lambda key, s: key > thr)
        nbits = int(l_pad - 1).bit_length()

        def idx_body(it, p):
            cand = p + lax.shift_left(jnp.int32(1), nbits - 1 - it)
            few = count32(lambda key, s: (key == thr) & (s < cand)) < need
            return jnp.where(few, cand, p)
        p_last = lax.fori_loop(0, nbits, idx_body, jnp.zeros((1, qb), I32))

        def demote(c, carry):
            key = keys_ref[rows_of(c), :]
            drop = tied & (key == thr) & (key_index(c) > p_last)
            keys_ref[rows_of(c), :] = jnp.where(drop, thr - 1, key)
            return carry
        lax.fori_loop(0, nkc, demote, 0)

    m_ref[...] = jnp.full(m_ref.shape, M_FLOOR, F32)
    l_ref[...] = jnp.zeros(l_ref.shape, F32)
    acc_ref[...] = jnp.zeros(acc_ref.shape, F32)
    slopes = [float(2.0 ** (-8.0 * (h + 1) / ATT_HEADS)) * LOG2E for h in range(ATT_HEADS)]
    nsub = kblk // ATT_SUB
    fold = lambda t, op: op(t.reshape(ATT_SUB // SUBLANES, SUBLANES, qb), axis=0)

    def logits(g, c):
        lg_ref[g] = jnp.dot(k_ref[rows_of(c), g * ATT_HD:(g + 1) * ATT_HD], qt_ref[g],
                            preferred_element_type=F32)

    def softmax_block(g):
        for r in range(ATT_REP):
            h = g * ATT_REP + r
            cs = slice(r * qb, (r + 1) * qb)
            mx = jnp.full((SUBLANES, qb), M_FLOOR, F32)
            for sb in range(nsub):
                rs = slice(sb * ATT_SUB, (sb + 1) * ATT_SUB)
                lgs = lg_ref[g, rs, cs] - slopes[h] * dm_ref[rs, :]
                lg_ref[g, rs, cs] = lgs
                mx = jnp.maximum(mx, fold(lgs, jnp.max))
            m_old = m_ref[g, :, cs]
            m_new = jnp.maximum(m_old, jnp.max(mx, axis=0, keepdims=True))
            sm = jnp.zeros((SUBLANES, qb), F32)
            for sb in range(nsub):
                rs = slice(sb * ATT_SUB, (sb + 1) * ATT_SUB)
                pr = jnp.exp2(lg_ref[g, rs, cs] - m_new)
                sm = sm + fold(pr, jnp.sum)
                pt_ref[g, rs, cs] = pr.astype(BF16)
            alpha = jnp.exp2(m_old - m_new)
            l_ref[g, :, cs] = alpha * l_ref[g, :, cs] + jnp.sum(sm, axis=0, keepdims=True)
            m_ref[g, :, cs] = m_new
            al_ref[g, :, cs] = alpha

    def weighted_values(g, c):
        pv = jnp.dot(vt_ref[g * ATT_HD:(g + 1) * ATT_HD, rows_of(c)], pt_ref[g],
                     preferred_element_type=F32)
        acc_ref[g] = acc_ref[g] * al_ref[g] + pv

    logits(0, 0)

    def att_body(c, carry):
        key = keys_ref[rows_of(c), :]
        dist = jnp.abs(pos - key_index(c)).astype(F32)
        dm_ref[...] = jnp.where(key >= thr_eff, dist, MASK_DIST)
        logits(1, c)
        softmax_block(0)
        weighted_values(0, c)
        logits(0, jnp.minimum(c + 1, nkc - 1))
        softmax_block(1)
        weighted_values(1, c)
        return carry
    lax.fori_loop(0, nkc, att_body, 0)

    for h in range(ATT_HEADS):
        g, r = divmod(h, ATT_REP)
        cs = slice(r * qb, (r + 1) * qb)
        ot = acc_ref[g, :, cs] / l_ref[g, :, cs]
        o_ref[:, h * ATT_HD:(h + 1) * ATT_HD] = ot.T.astype(BF16)


def _dsa(q_ops, k_all, vt_all, ki_all, *, nb, nqb, qb, past, l_real, kblk):
    l_pad = k_all.shape[1]
    topk = min(IDX_TOPK, l_real // 4)
    assert kblk >= topk and kblk % ATT_SUB == 0 and l_pad % kblk == 0
    (qa, qa_map), (qi, qi_map), (wi, wi_map) = q_ops
    kern = functools.partial(_dsa_kernel, past=past, l_real=l_real, kblk=kblk, topk=topk)
    return pl.pallas_call(
        kern,
        grid=(nb, nqb),
        in_specs=[pl.BlockSpec((qb, ATT_W), qa_map), pl.BlockSpec((qb, IDX_HEADS * IDX_DIM), qi_map),
                  pl.BlockSpec((qb, LANES), wi_map),
                  pl.BlockSpec((None, l_pad, ATT_KV * ATT_HD), lambda b, i: (b, 0, 0)),
                  pl.BlockSpec((None, ATT_KV * ATT_HD, l_pad), lambda b, i: (b, 0, 0)),
                  pl.BlockSpec((None, l_pad, LANES), lambda b, i: (b, 0, 0))],
        out_specs=pl.BlockSpec((qb, ATT_W), lambda b, i: (b * nqb + i, 0)),
        out_shape=jax.ShapeDtypeStruct((nb * nqb * qb, ATT_W), BF16),
        scratch_shapes=[pltpu.VMEM((l_pad, 2 * LANES), BF16),
                        pltpu.VMEM((l_pad, qb), I32),
                        pltpu.VMEM((l_pad, qb), I16),
                        pltpu.VMEM((l_pad, qb), I16),
                        pltpu.VMEM((2 * LANES, IDX_HEADS * qb), BF16),
                        pltpu.VMEM((ATT_KV, ATT_HD, ATT_REP * qb), BF16),
                        pltpu.VMEM((ATT_KV, 1, ATT_REP * qb), F32),
                        pltpu.VMEM((ATT_KV, 1, ATT_REP * qb), F32),
                        pltpu.VMEM((ATT_KV, ATT_HD, ATT_REP * qb), F32),
                        pltpu.VMEM((ATT_KV, kblk, ATT_REP * qb), F32),
                        pltpu.VMEM((ATT_KV, kblk, ATT_REP * qb), BF16),
                        pltpu.VMEM((kblk, qb), F32),
                        pltpu.VMEM((ATT_KV, 1, ATT_REP * qb), F32)],
        compiler_params=_cparams(("parallel", "arbitrary")),
        name="dsa",
    )(qa, qi, wi, k_all, vt_all, ki_all)


def _rw_prep_kernel(r_ref, k_ref, v_ref, lr_ref, hr_ref, hk_ref, hv_ref, hlr_ref,
                    sr_ref, sk_ref, sv_ref, slr_ref, mr_ref, mk_ref, mv_ref, mlr_ref,
                    w0_ref, a0_ref, w2_ref, a2_ref, g2_ref,
                    ro_ref, ko_ref, vo_ref, wo_ref, ao_ref, go_ref, *, tiles_per_seq):
    first_tile = (pl.program_id(0) % tiles_per_seq) == 0

    def mix(x_ref, halo_ref, shift_ref, mu_ref):
        x = x_ref[...]
        first = jnp.where(first_tile, shift_ref[...], halo_ref[SUBLANES - 1:SUBLANES, :])
        rows = lax.broadcasted_iota(I32, x.shape, 0)
        prev = jnp.where(rows == 0, first, pltpu.roll(x, 1, 0))
        return x + (prev - x) * mu_ref[...]

    ro_ref[...] = mix(r_ref, hr_ref, sr_ref, mr_ref)
    ko_ref[...] = mix(k_ref, hk_ref, sk_ref, mk_ref)
    vo_ref[...] = mix(v_ref, hv_ref, sv_ref, mv_ref)
    lr = mix(lr_ref, hlr_ref, slr_ref, mlr_ref)
    xw = lr[:, 0:128]
    xa = lr[:, 128:256]
    xg = lr[:, 256:512]
    wpre = w0_ref[...] + _dot3(jnp.tanh(xw), w2_ref[...])
    wo_ref[...] = jnp.exp(-RW_DECAY_SCALE * _sigmoid(wpre))
    ao_ref[...] = _sigmoid(a0_ref[...] + _dot3(xa, a2_ref[...]))
    go_ref[...] = _dot3(_sigmoid(xg), g2_ref[...])


def _rw_prep(proj, row0, nb, seq, shifts, mus, w0, a0, w2p, a2p, g2, tm):
    m = nb * seq
    tps = seq // tm
    rb0 = row0 // tm
    hb = tm // SUBLANES
    cr, ck, cv = _OFF["r"] // RW_W, _OFF["k"] // RW_W, _OFF["v"] // RW_W
    clr = _OFF["xw"] // 512

    def main(c):
        return lambda i: (rb0 + i, c)

    def halo(c):
        return lambda i: (jnp.maximum((rb0 + i) * hb - 1, 0), c)

    bidx = lambda i: (i // tps, 0, 0)
    fix = lambda i: (0, 0)
    wide = [pl.BlockSpec((tm, RW_W), main(cr)), pl.BlockSpec((tm, RW_W), main(ck)),
            pl.BlockSpec((tm, RW_W), main(cv)), pl.BlockSpec((tm, 512), main(clr)),
            pl.BlockSpec((SUBLANES, RW_W), halo(cr)), pl.BlockSpec((SUBLANES, RW_W), halo(ck)),
            pl.BlockSpec((SUBLANES, RW_W), halo(cv)), pl.BlockSpec((SUBLANES, 512), halo(clr))]
    sh = [pl.BlockSpec((None, 1, RW_W), bidx)] * 3 + [pl.BlockSpec((None, 1, 512), bidx)]
    mu = [pl.BlockSpec((1, RW_W), fix)] * 3 + [pl.BlockSpec((1, 512), fix)]
    par = [pl.BlockSpec((1, RW_W), fix), pl.BlockSpec((1, RW_W), fix),
           pl.BlockSpec((128, RW_W), fix), pl.BlockSpec((128, RW_W), fix),
           pl.BlockSpec((RW_G_R, RW_W), fix)]
    out = jax.ShapeDtypeStruct((m, RW_W), F32)
    return pl.pallas_call(
        functools.partial(_rw_prep_kernel, tiles_per_seq=tps),
        grid=(m // tm,),
        in_specs=wide + sh + mu + par,
        out_specs=[pl.BlockSpec((tm, RW_W), lambda i: (i, 0))] * 6,
        out_shape=[out] * 6,
        compiler_params=_cparams(("parallel",)),
        name="rwkv_prep",
    )(proj, proj, proj, proj, proj, proj, proj, proj, *shifts, *mus, w0, a0, w2p, a2p, g2)


def _rw_scan_kernel(r_ref, k_ref, v_ref, w_ref, a_ref, kkp_ref, kap_ref, rkp_ref,
                    lng_ref, lnb_ref, s0_ref, o_ref, st_ref, s_scr, tmp_scr):
    tb = pl.program_id(1)
    nsteps = r_ref.shape[0]

    @pl.when(tb == 0)
    def _():
        s_scr[...] = s0_ref[...]

    def step(t, carry):
        k = k_ref[t]
        a = a_ref[t]
        kk = k * kkp_ref[...]
        nrm = jnp.sqrt(jnp.sum(kk * kk, axis=0, keepdims=True))
        kk = kk / jnp.maximum(nrm, 1e-12)
        km = k * (1.0 + (a - 1.0) * kap_ref[...])
        tmp_scr[0] = kk
        tmp_scr[1] = kk * a
        tmp_scr[2] = km
        parts = [jnp.zeros((RW_HD, LANES), F32) for _ in range(2)]
        for kx in range(RW_HD):
            parts[kx % 2] = parts[kx % 2] + s_scr[kx] * tmp_scr[0, kx:kx + 1, :]
        sa = -(parts[0] + parts[1])
        v = v_ref[t]
        outs = [jnp.zeros((RW_HD, LANES), F32) for _ in range(2)]
        for kx in range(RW_HD):
            s_new = (s_scr[kx] * w_ref[t, kx:kx + 1, :] + sa * tmp_scr[1, kx:kx + 1, :]
                     + v * tmp_scr[2, kx:kx + 1, :])
            s_scr[kx] = s_new
            outs[kx % 2] = outs[kx % 2] + s_new * r_ref[t, kx:kx + 1, :]
        o = outs[0] + outs[1]
        mu = jnp.mean(o, axis=0, keepdims=True)
        oc = o - mu
        var = jnp.mean(oc * oc, axis=0, keepdims=True)
        on = oc * lax.rsqrt(var + RW_GN_EPS) * lng_ref[...] + lnb_ref[...]
        bonus = jnp.sum(r_ref[t] * tmp_scr[2] * rkp_ref[...], axis=0, keepdims=True) * v
        o_ref[t] = on + bonus
        return carry
    lax.fori_loop(0, nsteps, step, 0)

    @pl.when(tb == pl.num_programs(1) - 1)
    def _():
        st_ref[...] = s_scr[...]


def _rw_scan(r, k, v, w, a, params, s0, tb):
    seq, _, pairs = r.shape
    assert pairs % LANES == 0, "batch * RW_HEADS must fill whole lane groups"
    seqb = pl.BlockSpec((tb, RW_HD, LANES), lambda g, t: (t, 0, g))
    par = pl.BlockSpec((RW_HD, LANES), lambda g, t: (0, 0))
    st = pl.BlockSpec((RW_HD, RW_HD, LANES), lambda g, t: (0, 0, g))
    return pl.pallas_call(
        _rw_scan_kernel,
        grid=(pairs // LANES, seq // tb),
        in_specs=[seqb] * 5 + [par] * 5 + [st],
        out_specs=[seqb, st],
        out_shape=[jax.ShapeDtypeStruct((seq, RW_HD, pairs), F32),
                   jax.ShapeDtypeStruct((RW_HD, RW_HD, pairs), F32)],
        scratch_shapes=[pltpu.VMEM((RW_HD, RW_HD, LANES), F32),
                        pltpu.VMEM((3, RW_HD, LANES), F32)],
        compiler_params=_cparams(("parallel", "arbitrary")),
        name="rwkv_scan",
    )(r, k, v, w, a, *params, s0)


GLA_SUB = 16


def _gla_kernel(q_ref, k_ref, v_ref, gl_ref, og_ref, w2_ref, gb_ref, ng_ref, s0_ref,
                o_ref, st_ref, s_scr, la_scr, att_scr):
    tblk = pl.program_id(2)
    tc = q_ref.shape[0]

    @pl.when(tblk == 0)
    def _():
        s_scr[...] = s0_ref[...].T

    la_scr[...] = _log_sigmoid(_dot3(gl_ref[...], w2_ref[...]) + gb_ref[...]) * (1.0 / GLA_GATE_NORM)
    rows = lax.broadcasted_iota(I32, (CHUNK, GLA_DK), 0)
    lane = lax.broadcasted_iota(I32, (GLA_SUB, CHUNK), 1)
    jrow = lax.broadcasted_iota(I32, (GLA_SUB, GLA_DK), 0)
    ng = ng_ref[...]
    nsub = CHUNK // GLA_SUB

    def chunk_body(c, carry):
        r0 = pl.multiple_of(c * CHUNK, CHUNK)
        rs = pl.ds(r0, CHUNK)
        b = la_scr[rs, :]
        sh = 1
        while sh < CHUNK:
            b = b + jnp.where(rows >= sh, pltpu.roll(b, sh, 0), 0.0)
            sh *= 2
        q = q_ref[rs, :] * (GLA_DK ** -0.5)
        k = k_ref[rs, :]
        v = v_ref[rs, :]
        s_old = s_scr[...]
        inter = _nt_dot(q * jnp.exp(b), s_old)
        b_last = b[CHUNK - 1:CHUNK, :]

        for jb in range(nsub):
            js = slice(jb * GLA_SUB, (jb + 1) * GLA_SUB)
            bj = b[js]
            kj = k[js]
            blk = jnp.zeros((GLA_SUB, CHUNK), F32)
            for ii in range(GLA_SUB):
                i_row = jb * GLA_SUB + ii
                e = jnp.exp(jnp.minimum(b[i_row:i_row + 1, :] - bj, 0.0))
                val = jnp.where(jrow <= ii, q[i_row:i_row + 1, :] * kj * e, 0.0)
                col = jnp.sum(val, axis=1, keepdims=True)
                blk = jnp.where(lane == i_row, col, blk)
            if jb + 1 < nsub:
                i0 = (jb + 1) * GLA_SUB
                be = b[i0 - 1:i0, :]
                kt = kj * jnp.exp(be - bj)
                later = rows >= i0
                qt = jnp.where(later, q * jnp.exp(jnp.where(later, b - be, 0.0)), 0.0)
                blk = blk + _nt_dot(kt, qt)
            att_scr[js, :] = blk
        o = inter + _tn_dot(att_scr[...], v)
        khat = k * jnp.exp(b_last - b)
        s_scr[...] = s_old * jnp.exp(b_last) + _tn_dot(v, khat)
        o = o * lax.rsqrt(jnp.mean(o * o, -1, keepdims=True) + GLA_NORM_EPS) * ng
        og = og_ref[rs, :]
        o_ref[rs, :] = (o * (og * _sigmoid(og))).astype(BF16)
        return carry
    lax.fori_loop(0, tc // CHUNK, chunk_body, 0)

    @pl.when(tblk == pl.num_programs(2) - 1)
    def _():
        st_ref[...] = s_scr[...].T


def _gla(proj, tail, row0, nb, seq, w2p, gb, ng, s0, tc):
    ntb = seq // tc
    rb0 = row0 // tc
    cq, ck = _OFF["qc"] // GLA_DK, _OFF["kc"] // GLA_DK
    cv, cg = _OFF["vc"] // GLA_DV, _OFF["og"] // GLA_DV
    cgl = (_OFF["gl"] - _OFF["ka"]) // LANES

    def rows(c0, per_head):
        return lambda b, h, t: (rb0 + b * ntb + t, c0 + (h if per_head else 0))

    st = pl.BlockSpec((None, None, GLA_DK, GLA_DV), lambda b, h, t: (b, h, 0, 0))
    return pl.pallas_call(
        _gla_kernel,
        grid=(nb, GLA_HEADS, ntb),
        in_specs=[pl.BlockSpec((tc, GLA_DK), rows(cq, True)), pl.BlockSpec((tc, GLA_DK), rows(ck, True)),
                  pl.BlockSpec((tc, GLA_DV), rows(cv, True)), pl.BlockSpec((tc, LANES), rows(cgl, False)),
                  pl.BlockSpec((tc, GLA_DV), rows(cg, True)),
                  pl.BlockSpec((LANES, GLA_DK), lambda b, h, t: (0, h)),
                  pl.BlockSpec((1, GLA_DK), lambda b, h, t: (0, h)),
                  pl.BlockSpec((1, GLA_DV), lambda b, h, t: (0, h)), st],
        out_specs=[pl.BlockSpec((tc, GLA_DV), lambda b, h, t: (b * ntb + t, h)), st],
        out_shape=[jax.ShapeDtypeStruct((nb * seq, GLA_VW), BF16),
                   jax.ShapeDtypeStruct((nb, GLA_HEADS, GLA_DK, GLA_DV), F32)],
        scratch_shapes=[pltpu.VMEM((GLA_DV, GLA_DK), F32), pltpu.VMEM((tc, GLA_DK), F32),
                        pltpu.VMEM((CHUNK, CHUNK), F32)],
        compiler_params=_cparams(("parallel", "parallel", "arbitrary")),
        name="gla",
    )(proj, proj, proj, tail, proj, w2p, gb, ng, s0)


def _pack_cols(w):
    parts = []
    for _, s, wd, p in _SEGS:
        parts.append(w[..., s:s + wd])
        if p > wd:
            parts.append(jnp.zeros(w.shape[:-1] + (p - wd,), w.dtype))
    return jnp.concatenate(parts, axis=-1)


def _pack_rw(vec):
    r, k, v = vec[..., 0:1024], vec[..., 1024:2048], vec[..., 2048:3072]
    z = jnp.zeros(vec.shape[:-1] + (32,), vec.dtype)
    lr = jnp.concatenate([vec[..., 3072:3168], z, vec[..., 3168:3264], z, vec[..., 3264:3520]], -1)
    return r, k, v, lr


def _pad_rows(w, n):
    return jnp.concatenate([w, jnp.zeros((n - w.shape[0],) + w.shape[1:], w.dtype)], 0)


def _to_scan(x, nb, seq):
    return jnp.transpose(x.reshape(nb, seq, RW_HEADS, RW_HD), (1, 3, 0, 2)).reshape(seq, RW_HD, nb * RW_HEADS)


def _from_scan(o, nb, seq):
    return jnp.transpose(o.reshape(seq, RW_HD, nb, RW_HEADS), (2, 0, 3, 1)).reshape(nb * seq, RW_W)


def _group_mixers(proj, tail, row0, nb, seq, caches, lw):
    m = nb * seq
    cache_k, cache_v, cache_ki, s_rw, s_shift, s_gla = caches
    past = 0 if cache_k is None else cache_k.shape[1]
    l_real = past + seq
    t0 = _OFF["ka"]
    grp = tail[row0:row0 + m].reshape(nb, seq, -1)

    ka = grp[..., _OFF["ka"] - t0:_OFF["ka"] - t0 + 256]
    va = grp[..., _OFF["va"] - t0:_OFF["va"] - t0 + 256]
    ki = grp[..., _OFF["ki"] - t0:_OFF["ki"] - t0 + LANES]
    if past:
        k_all = jnp.concatenate([cache_k.reshape(nb, past, 256), ka], 1)
        v_all = jnp.concatenate([cache_v.reshape(nb, past, 256), va], 1)
        cki = jnp.concatenate([cache_ki, jnp.zeros((nb, past, LANES - IDX_DIM), F32)], -1)
        ki_all = jnp.concatenate([cki, ki], 1)
    else:
        k_all, v_all, ki_all = ka, va, ki
    kblk = 512 if l_real % 512 == 0 else 384
    l_pad = -(-l_real // kblk) * kblk
    if l_pad > l_real:
        padk = lambda t: jnp.concatenate([t, jnp.zeros((nb, l_pad - l_real, t.shape[-1]), t.dtype)], 1)
        k_all, v_all, ki_all = padk(k_all), padk(v_all), padk(ki_all)
    k_bf = k_all.astype(BF16)
    vt_bf = jnp.swapaxes(v_all, 1, 2).astype(BF16)
    qb = DSA_QB if seq % DSA_QB == 0 else LANES
    if seq % qb == 0:
        nqb = seq // qb
        rb0 = row0 // qb
        mk = lambda c: (lambda b, i: (rb0 + b * nqb + i, c))
        q_ops = ((proj, mk(_OFF["qa"] // ATT_W)), (proj, mk(_OFF["qi"] // 1024)),
                 (tail, mk((_OFF["wi"] - t0) // LANES)))
        o_a = _dsa(q_ops, k_bf, vt_bf, ki_all, nb=nb, nqb=nqb, qb=qb, past=past, l_real=l_real, kblk=kblk)
    else:
        def padq(t):
            return jnp.concatenate([t, jnp.zeros((nb, qb - seq, t.shape[-1]), F32)], 1).reshape(nb * qb, -1)
        qrows = proj[row0:row0 + m, 0:_OFF["vc"]].reshape(nb, seq, -1)
        mk = lambda b, i: (b, 0)
        q_ops = ((padq(qrows[..., 0:ATT_W]), mk), (padq(qrows[..., ATT_W:]), mk),
                 (padq(grp[..., _OFF["wi"] - t0:_OFF["wi"] - t0 + LANES]), mk))
        o_a = _dsa(q_ops, k_bf, vt_bf, ki_all, nb=nb, nqb=1, qb=qb, past=past, l_real=l_real, kblk=kblk)
        o_a = o_a.reshape(nb, qb, ATT_W)[:, :seq].reshape(m, ATT_W)

    if s_shift is None:
        s_shift = jnp.zeros((nb, RW_PROJ), F32)
    shifts = [t.reshape(nb, 1, -1) for t in _pack_rw(s_shift)]
    tm = 256 if seq % 256 == 0 else seq
    r, k, v, w, a, g = _rw_prep(proj, row0, nb, seq, shifts, lw["rw_mus"], lw["rw_w0"], lw["rw_a0"],
                                lw["rw_w2p"], lw["rw_a2p"], lw["rw_g2"], tm)
    if s_rw is None:
        s0 = jnp.zeros((RW_HD, RW_HD, nb * RW_HEADS), F32)
    else:
        s0 = jnp.transpose(s_rw, (3, 2, 0, 1)).reshape(RW_HD, RW_HD, nb * RW_HEADS)
    tile = lambda p: jnp.tile(p.reshape(RW_HEADS, RW_HD).T, (1, LANES // RW_HEADS))
    params = [tile(lw[n]) for n in ("rw_kk", "rw_ka", "rw_rk", "rw_ln_g", "rw_ln_b")]
    tb = 32 if seq % 32 == 0 else seq
    o_b, s_t = _rw_scan(*[_to_scan(t, nb, seq) for t in (r, k, v, w, a)], params, s0, tb)
    o_b = _from_scan(o_b, nb, seq)
    s_rw_new = jnp.transpose(s_t.reshape(RW_HD, RW_HD, nb, RW_HEADS), (2, 3, 1, 0))
    last = jnp.take(proj, row0 + seq - 1 + seq * jnp.arange(nb), axis=0)[:, _OFF["r"]:_OFF["qc"]]
    base = _OFF["r"]
    shift_new = jnp.concatenate(
        [last[:, 0:3072], last[:, _OFF["xw"] - base:_OFF["xw"] - base + 96],
         last[:, _OFF["xa"] - base:_OFF["xa"] - base + 96], last[:, _OFF["xg"] - base:]], -1)

    if s_gla is None:
        s_gla = jnp.zeros((nb, GLA_HEADS, GLA_DK, GLA_DV), F32)
    tc = 512 if seq % 512 == 0 else seq
    o_c, s_gla_new = _gla(proj, tail, row0, nb, seq, lw["gla_w2p"], lw["gla_b"], lw["gla_ng"], s_gla, tc)

    new_state = (ka.reshape(nb, seq, ATT_KV, ATT_HD), va.reshape(nb, seq, ATT_KV, ATT_HD),
                 ki[..., :IDX_DIM], s_rw_new, shift_new, s_gla_new)
    return o_a, o_b, g, o_c, new_state


def kernel(x_prompt, x_sample, cache_k, cache_v, cache_kidx, state_rwkv, state_rwkv_shift, state_gla,
           w_in, rw_mu, rw_w0, rw_w2, rw_a0, rw_a2, rw_g2, rw_kk, rw_ka, rw_rk, rw_ln_g, rw_ln_b,
           gla_w2, gla_b, gla_norm_g, w_branch_a, w_branch_b, w_branch_c, w_out,
           ln1_g, ln1_b, ln2_g, ln2_b, ffn_w1, ffn_w3, ffn_w2):
    bp, tp, _ = x_prompt.shape
    bs, ts, _ = x_sample.shape
    mp, ms = bp * tp, bs * ts
    x = jnp.concatenate([x_prompt.reshape(mp, D_MODEL), x_sample.reshape(ms, D_MODEL)], 0)
    xb = x.astype(BF16)
    m = mp + ms
    tm_big = 1024 if m % 1024 == 0 else 256
    tm_mid = 512 if m % 512 == 0 else 256

    w_in_p = _pack_cols(w_in).astype(BF16)
    new_p = [[] for _ in range(6)]
    new_s = [[] for _ in range(6)]
    for l in range(DEPTH):
        lw = {
            "rw_mus": [t.reshape(1, -1) for t in _pack_rw(rw_mu[l])],
            "rw_w0": rw_w0[l].reshape(1, RW_W), "rw_a0": rw_a0[l].reshape(1, RW_W),
            "rw_w2p": _pad_rows(rw_w2[l], 128), "rw_a2p": _pad_rows(rw_a2[l], 128), "rw_g2": rw_g2[l],
            "rw_kk": rw_kk[l], "rw_ka": rw_ka[l], "rw_rk": rw_rk[l],
            "rw_ln_g": rw_ln_g[l], "rw_ln_b": rw_ln_b[l],
            "gla_w2p": _pad_rows(gla_w2[l], LANES), "gla_b": gla_b[l].reshape(1, GLA_KW),
            "gla_ng": gla_norm_g[l].reshape(1, GLA_VW),
        }
        proj = _matmul(xb, w_in_p[l, :, :_OFF["ka"]], tm_big, 1024)
        tail = _matmul(xb, w_in_p[l, :, _OFF["ka"]:], tm_big, PROJ_W - _OFF["ka"])
        oa_p, ob_p, g_p, oc_p, st_p = _group_mixers(proj, tail, 0, bp, tp, (None,) * 6, lw)
        oa_s, ob_s, g_s, oc_s, st_s = _group_mixers(
            proj, tail, mp, bs, ts,
            (cache_k[l], cache_v[l], cache_kidx[l], state_rwkv[l], state_rwkv_shift[l], state_gla[l]), lw)
        cat = lambda u, v: jnp.concatenate([u, v], 0)
        merged = _merge(cat(oa_p, oa_s), cat(ob_p, ob_s), cat(g_p, g_s), cat(oc_p, oc_s), proj,
                        w_branch_a[l].astype(BF16), w_branch_b[l].astype(BF16),
                        w_branch_c[l].astype(BF16), 256, D_MODEL)
        h, hb = _hproj(merged, w_out[l].astype(BF16), x, ln1_g[l].reshape(1, -1),
                       ln1_b[l].reshape(1, -1), tm_mid)
        x, xb = _ffn(hb, h, ffn_w1[l].astype(BF16), ffn_w3[l].astype(BF16), ffn_w2[l].astype(BF16),
                     ln2_g[l].reshape(1, -1), ln2_b[l].reshape(1, -1), tm_mid, 512)
        for i in range(6):
            new_p[i].append(st_p[i])
            new_s[i].append(st_s[i])
    sp = [jnp.stack(a) for a in new_p]
    ss = [jnp.stack(a) for a in new_s]
    yp = x[:mp].reshape(bp, tp, D_MODEL)
    ys = x[mp:].reshape(bs, ts, D_MODEL)
    return (yp, ys, sp[0], sp[1], sp[2], sp[3], sp[4], sp[5], ss[0], ss[1], ss[2], ss[3], ss[4], ss[5])
```

```python
import functools

import jax
import jax.numpy as jnp
from jax import lax
from jax.experimental import pallas as pl
from jax.experimental.pallas import tpu as pltpu

F32 = jnp.float32
BF16 = jnp.bfloat16
I32 = jnp.int32
I16 = jnp.int16

D_MODEL = 2048
DEPTH = 4
CHUNK = 64
Q_BLOCK = 128
ATT_HEADS = 8
ATT_HD = 128
ATT_KV = 2
ATT_REP = ATT_HEADS // ATT_KV
ATT_W = ATT_HEADS * ATT_HD
IDX_HEADS = 16
IDX_DIM = 64
IDX_TOPK = 256
IDX_SCALE = (IDX_DIM * IDX_HEADS) ** -0.5
RW_HEADS = 16
RW_HD = 64
RW_W = RW_HEADS * RW_HD
RW_DECAY_R = 96
RW_A_R = 96
RW_G_R = 256
RW_PROJ = 3 * RW_W + RW_DECAY_R + RW_A_R + RW_G_R
RW_DECAY_SCALE = 0.606531
RW_GN_EPS = 64e-5
GLA_HEADS = 4
GLA_DK = 128
GLA_DV = 256
GLA_KW = GLA_HEADS * GLA_DK
GLA_VW = GLA_HEADS * GLA_DV
GLA_GATE_R = 16
GLA_GATE_NORM = 16.0
GLA_NORM_EPS = 1e-5
FFN_HIDDEN = -(-8 * D_MODEL // (3 * 256)) * 256
DN_ALPHA = (2 * DEPTH) ** 0.25
LN_EPS = 1e-5
LOG2E = 1.4426950408889634

LANES = 128
SUBLANES = 8
PACKED_ROWS = 16
VMEM_LIMIT = 56 * 1024 * 1024

_SEGS = (
    ("qa", 0, 1024, 1024), ("qi", 1536, 1024, 1024), ("vc", 7184, 1024, 1024),
    ("og", 8224, 1024, 1024), ("gates", 9248, 6144, 6144),
    ("r", 2640, 1024, 1024), ("k", 3664, 1024, 1024), ("v", 4688, 1024, 1024),
    ("xw", 5712, 96, 128), ("xa", 5808, 96, 128), ("xg", 5904, 256, 256),
    ("qc", 6160, 512, 512), ("kc", 6672, 512, 512),
    ("ka", 1024, 256, 256), ("va", 1280, 256, 256),
    ("ki", 2560, 64, 128), ("wi", 2624, 16, 128), ("gl", 8208, 16, 128),
)
_OFF = {}
_o = 0
for _n, _s, _w, _p in _SEGS:
    _OFF[_n] = _o
    _o += _p
PROJ_W = _o
KEY_NEG_INF = -0x7F800000
MASK_DIST = 1e32
M_FLOOR = -1e29


def _cparams(sem):
    return pltpu.CompilerParams(dimension_semantics=sem, vmem_limit_bytes=VMEM_LIMIT)


def _split2(x):
    hi = x.astype(BF16)
    lo = (x - hi.astype(F32)).astype(BF16)
    return hi, lo


def _dot3(a, b):
    ah, al = _split2(a)
    bh, bl = _split2(b)
    d = functools.partial(jnp.dot, preferred_element_type=F32)
    return d(ah, bh) + (d(al, bh) + d(ah, bl))


def _nt_dot(a, b):
    return lax.dot_general(a.astype(BF16), b.astype(BF16), (((1,), (1,)), ((), ())),
                           preferred_element_type=F32)


def _tn_dot(a, b):
    return lax.dot_general(a.astype(BF16), b.astype(BF16), (((0,), (0,)), ((), ())),
                           preferred_element_type=F32)


def _sigmoid(x):
    return 1.0 / (1.0 + jnp.exp(-x))


def _log_sigmoid(x):
    return jnp.minimum(x, 0.0) - jnp.log(1.0 + jnp.exp(-jnp.abs(x)))


def _layer_norm(x, g, b):
    mu = jnp.mean(x, -1, keepdims=True)
    xc = x - mu
    var = jnp.mean(xc * xc, -1, keepdims=True)
    return xc * lax.rsqrt(var + LN_EPS) * g + b


def _mm_kernel(a_ref, b_ref, o_ref):
    o_ref[...] = jnp.dot(a_ref[...], b_ref[...], preferred_element_type=F32)


def _matmul(a, b, tm, tn):
    m, k = a.shape
    n = b.shape[1]
    return pl.pallas_call(
        _mm_kernel,
        grid=(m // tm, pl.cdiv(n, tn)),
        in_specs=[pl.BlockSpec((tm, k), lambda i, j: (i, 0)),
                  pl.BlockSpec((k, tn), lambda i, j: (0, j))],
        out_specs=pl.BlockSpec((tm, tn), lambda i, j: (i, j)),
        out_shape=jax.ShapeDtypeStruct((m, n), F32),
        compiler_params=_cparams(("parallel", "arbitrary")),
        name="in_proj",
    )(a, b)


def _merge_kernel(oa_ref, ob_ref, g_ref, oc_ref, ga_ref, gb_ref, gc_ref,
                  wa_ref, wb_ref, wc_ref, o_ref):
    ob = (ob_ref[...] * g_ref[...]).astype(BF16)
    d = functools.partial(jnp.dot, preferred_element_type=F32)
    acc = _sigmoid(ga_ref[...]) * d(oa_ref[...], wa_ref[...])
    acc += _sigmoid(gb_ref[...]) * d(ob, wb_ref[...])
    acc += _sigmoid(gc_ref[...]) * d(oc_ref[...], wc_ref[...])
    o_ref[...] = acc.astype(BF16)


def _merge(oa, ob, g, oc, proj, wa, wb, wc, tm, tn):
    m = oa.shape[0]
    gblk = _OFF["gates"] // tn
    nj = D_MODEL // tn
    row = lambda i, j: (i, 0)
    wspec = pl.BlockSpec((ATT_W, tn), lambda i, j: (0, j))
    return pl.pallas_call(
        _merge_kernel,
        grid=(m // tm, nj),
        in_specs=[pl.BlockSpec((tm, ATT_W), row), pl.BlockSpec((tm, RW_W), row),
                  pl.BlockSpec((tm, RW_W), row), pl.BlockSpec((tm, GLA_VW), row),
                  pl.BlockSpec((tm, tn), lambda i, j: (i, gblk + j)),
                  pl.BlockSpec((tm, tn), lambda i, j: (i, gblk + nj + j)),
                  pl.BlockSpec((tm, tn), lambda i, j: (i, gblk + 2 * nj + j)),
                  wspec, wspec, wspec],
        out_specs=pl.BlockSpec((tm, tn), lambda i, j: (i, j)),
        out_shape=jax.ShapeDtypeStruct((m, D_MODEL), BF16),
        compiler_params=_cparams(("parallel", "arbitrary")),
        name="merge",
    )(oa, ob, g, oc, proj, proj, proj, wa, wb, wc)


def _hproj_kernel(m_ref, w_ref, x_ref, g_ref, b_ref, h_ref, hb_ref):
    z = DN_ALPHA * x_ref[...] + jnp.dot(m_ref[...], w_ref[...], preferred_element_type=F32)
    h = _layer_norm(z, g_ref[...], b_ref[...])
    h_ref[...] = h
    hb_ref[...] = h.astype(BF16)


def _hproj(merged, w_out, x, g, b, tm):
    m = x.shape[0]
    row = lambda i: (i, 0)
    fix = lambda i: (0, 0)
    return pl.pallas_call(
        _hproj_kernel,
        grid=(m // tm,),
        in_specs=[pl.BlockSpec((tm, D_MODEL), row), pl.BlockSpec((D_MODEL, D_MODEL), fix),
                  pl.BlockSpec((tm, D_MODEL), row), pl.BlockSpec((1, D_MODEL), fix),
                  pl.BlockSpec((1, D_MODEL), fix)],
        out_specs=[pl.BlockSpec((tm, D_MODEL), row), pl.BlockSpec((tm, D_MODEL), row)],
        out_shape=[jax.ShapeDtypeStruct((m, D_MODEL), F32),
                   jax.ShapeDtypeStruct((m, D_MODEL), BF16)],
        compiler_params=_cparams(("parallel",)),
        name="out_proj_ln",
    )(merged, w_out, x, g, b)


def _ffn_kernel(hb_ref, h_ref, w1_ref, w3_ref, w2_ref, g_ref, b_ref, y_ref, yb_ref, acc_ref):
    j = pl.program_id(1)

    @pl.when(j == 0)
    def _():
        acc_ref[...] = jnp.zeros_like(acc_ref)

    hb = hb_ref[...]
    u1 = jnp.dot(hb, w1_ref[...], preferred_element_type=F32)
    u3 = jnp.dot(hb, w3_ref[...], preferred_element_type=F32)
    u = (u1 * _sigmoid(u1) * u3).astype(BF16)
    acc_ref[...] += jnp.dot(u, w2_ref[...], preferred_element_type=F32)

    @pl.when(j == pl.num_programs(1) - 1)
    def _():
        y = _layer_norm(DN_ALPHA * h_ref[...] + acc_ref[...], g_ref[...], b_ref[...])
        y_ref[...] = y
        yb_ref[...] = y.astype(BF16)


def _ffn(hb, h, w1, w3, w2, g, b, tm, th):
    m = h.shape[0]
    row = lambda i, j: (i, 0)
    fix = lambda i, j: (0, 0)
    return pl.pallas_call(
        _ffn_kernel,
        grid=(m // tm, FFN_HIDDEN // th),
        in_specs=[pl.BlockSpec((tm, D_MODEL), row), pl.BlockSpec((tm, D_MODEL), row),
                  pl.BlockSpec((D_MODEL, th), lambda i, j: (0, j)),
                  pl.BlockSpec((D_MODEL, th), lambda i, j: (0, j)),
                  pl.BlockSpec((th, D_MODEL), lambda i, j: (j, 0)),
                  pl.BlockSpec((1, D_MODEL), fix), pl.BlockSpec((1, D_MODEL), fix)],
        out_specs=[pl.BlockSpec((tm, D_MODEL), row), pl.BlockSpec((tm, D_MODEL), row)],
        out_shape=[jax.ShapeDtypeStruct((m, D_MODEL), F32),
                   jax.ShapeDtypeStruct((m, D_MODEL), BF16)],
        scratch_shapes=[pltpu.VMEM((tm, D_MODEL), F32)],
        compiler_params=_cparams(("parallel", "arbitrary")),
        name="ffn_ln",
    )(hb, h, w1, w3, w2, g, b)


ATT_SUB = 128
DSA_QB = 256


def _dsa_kernel(qa_ref, qi_ref, wi_ref, k_ref, vt_ref, ki_ref, o_ref,
                kcat_ref, keys_ref, hi_ref, lo_ref, wq_ref, qt_ref, m_ref, l_ref, acc_ref,
                lg_ref, pt_ref, dm_ref, al_ref, *, past, l_real, kblk, topk):
    qb = qa_ref.shape[0]
    l_pad = ki_ref.shape[0]
    i = pl.program_id(1)
    kf = float(topk)

    @pl.when(i == 0)
    def _():
        def prep(c, carry):
            r0 = pl.multiple_of(c * kblk, kblk)
            x = ki_ref[pl.ds(r0, kblk), :]
            hi = x.astype(BF16)
            lo = x - hi.astype(F32)
            a = (hi.astype(F32) + pltpu.roll(lo, IDX_DIM, 1)).astype(BF16)
            kcat_ref[pl.ds(r0, kblk), 0:LANES] = a
            kcat_ref[pl.ds(r0, kblk), LANES:2 * LANES] = hi
            return carry
        lax.fori_loop(0, l_pad // kblk, prep, 0)

    qit = qi_ref[...].T
    qhi = qit.astype(BF16)
    qlo = (qit - qhi.astype(F32)).astype(BF16)
    zero = jnp.zeros((IDX_DIM, qb), BF16)
    for h in range(IDX_HEADS):
        hs = slice(h * IDX_DIM, (h + 1) * IDX_DIM)
        cs = slice(h * qb, (h + 1) * qb)
        wq_ref[0:64, cs] = qhi[hs]
        wq_ref[64:128, cs] = qhi[hs]
        wq_ref[128:192, cs] = qlo[hs]
        wq_ref[192:256, cs] = zero
    wit = wi_ref[...].T
    qat = (qa_ref[...] * (ATT_HD ** -0.5 * LOG2E)).T.astype(BF16)
    for h in range(ATT_HEADS):
        g, r = divmod(h, ATT_REP)
        qt_ref[g, :, r * qb:(r + 1) * qb] = qat[h * ATT_HD:(h + 1) * ATT_HD]

    q0 = past + i * qb
    pos = q0 + lax.broadcasted_iota(I32, (1, qb), 1)
    pos_chunk = lax.shift_right_logical(pos, 6)
    adm_keys = jnp.minimum(l_real, (lax.shift_right_logical(q0 + qb - 1, 6) + 1) * CHUNK)
    nkc = lax.div(adm_keys + (kblk - 1), kblk)

    def rows_of(c):
        return pl.ds(pl.multiple_of(c * kblk, kblk), kblk)

    def key_index(c):
        return c * kblk + lax.broadcasted_iota(I32, (kblk, qb), 0)

    def score_body(c, carry):
        kc = kcat_ref[rows_of(c), :]
        acc = jnp.zeros((kblk, qb), F32)
        for hg in range(IDX_HEADS // 4):
            rel = jnp.dot(kc, wq_ref[:, hg * 4 * qb:(hg + 1) * 4 * qb],
                          preferred_element_type=F32)
            for hh in range(4):
                h = hg * 4 + hh
                acc = acc + jnp.maximum(rel[:, hh * qb:(hh + 1) * qb], 0.0) * wit[h:h + 1, :]
        score = acc * IDX_SCALE
        s_idx = key_index(c)
        adm = (lax.shift_right_logical(s_idx, 6) <= pos_chunk) & (s_idx < l_real)
        score = jnp.where(adm, score, -jnp.inf)
        bits = pltpu.bitcast(score, I32)
        key = jnp.where(bits < 0, -(bits & 0x7FFFFFFF), bits)
        keys_ref[rows_of(c), :] = key
        hi_ref[rows_of(c), :] = lax.shift_right_arithmetic(key, 16).astype(I16)
        return carry
    lax.fori_loop(0, nkc, score_body, 0)

    n_acc = 4
    one16 = jnp.ones((), I16)
    zero16 = jnp.zeros((), I16)

    def count16(ref, cand):
        c16 = cand.astype(I16)

        def body(c, accs):
            ind = jnp.where(ref[rows_of(c), :] >= c16, one16, zero16)
            accs = list(accs)
            for s in range(kblk // PACKED_ROWS):
                accs[s % n_acc] = accs[s % n_acc] + ind[s * PACKED_ROWS:(s + 1) * PACKED_ROWS]
            return tuple(accs)
        accs = lax.fori_loop(0, nkc, body, (jnp.zeros((PACKED_ROWS, qb), I16),) * n_acc)
        tot = (accs[0].astype(I32) + accs[1].astype(I32)) + (accs[2].astype(I32) + accs[3].astype(I32))
        return jnp.sum(tot, axis=0, keepdims=True).astype(F32)

    def search16(ref):
        ans = jnp.full((1, qb), -32768, I32)
        cnt = jnp.full((1, qb), 1e9, F32)
        for bit in range(15, -1, -1):
            cand = ans + (1 << bit)
            c = count16(ref, cand)
            ok = c >= kf
            ans = jnp.where(ok, cand, ans)
            cnt = jnp.where(ok, c, cnt)
        return ans, cnt

    thr_hi, _ = search16(hi_ref)

    def low_body(c, carry):
        key = keys_ref[rows_of(c), :]
        hi = lax.shift_right_arithmetic(key, 16)
        lo = (key & 0xFFFF) - 32768
        k2 = jnp.where(hi > thr_hi, 32767, jnp.where(hi < thr_hi, -32768, lo))
        lo_ref[rows_of(c), :] = k2.astype(I16)
        return carry
    lax.fori_loop(0, nkc, low_body, 0)
    thr_lo, n_ge = search16(lo_ref)
    thr = thr_hi * 65536 + (thr_lo + 32768)
    thr_eff = jnp.maximum(thr, KEY_NEG_INF + 1)

    tied = (n_ge > kf) & (thr > KEY_NEG_INF)

    @pl.when(jnp.max(jnp.where(tied, 1.0, 0.0)) > 0.0)
    def _():
        def count32(pred_fn):
            def body(c, acc):
                ind = jnp.where(pred_fn(keys_ref[rows_of(c), :], key_index(c)), 1.0, 0.0)
                return acc + jnp.sum(ind.reshape(kblk // SUBLANES, SUBLANES, qb), axis=0)
            acc = lax.fori_loop(0, nkc, body, jnp.zeros((SUBLANES, qb), F32))
            return jnp.sum(acc, axis=0, keepdims=True)
        need = kf - count32(lambda key, s: key > thr)
        nbits = int(l_pad - 1).bit_length()

        def idx_body(it, p):
            cand = p + lax.shift_left(jnp.int32(1), nbits - 1 - it)
            few = count32(lambda key, s: (key == thr) & (s < cand)) < need
            return jnp.where(few, cand, p)
        p_last = lax.fori_loop(0, nbits, idx_body, jnp.zeros((1, qb), I32))

        def demote(c, carry):
            key = keys_ref[rows_of(c), :]
            drop = tied & (key == thr) & (key_index(c) > p_last)
            keys_ref[rows_of(c), :] = jnp.where(drop, thr - 1, key)
            return carry
        lax.fori_loop(0, nkc, demote, 0)

    m_ref[...] = jnp.full(m_ref.shape, M_FLOOR, F32)
    l_ref[...] = jnp.zeros(l_ref.shape, F32)
    acc_ref[...] = jnp.zeros(acc_ref.shape, F32)
    slopes = [float(2.0 ** (-8.0 * (h + 1) / ATT_HEADS)) * LOG2E for h in range(ATT_HEADS)]
    nsub = kblk // ATT_SUB
    fold = lambda t, op: op(t.reshape(ATT_SUB // SUBLANES, SUBLANES, qb), axis=0)

    def logits(g, c):
        lg_ref[g] = jnp.dot(k_ref[rows_of(c), g * ATT_HD:(g + 1) * ATT_HD], qt_ref[g],
                            preferred_element_type=F32)

    def softmax_block(g):
        for r in range(ATT_REP):
            h = g * ATT_REP + r
            cs = slice(r * qb, (r + 1) * qb)
            mx = jnp.full((SUBLANES, qb), M_FLOOR, F32)
            for sb in range(nsub):
                rs = slice(sb * ATT_SUB, (sb + 1) * ATT_SUB)
                lgs = lg_ref[g, rs, cs] - slopes[h] * dm_ref[rs, :]
                lg_ref[g, rs, cs] = lgs
                mx = jnp.maximum(mx, fold(lgs, jnp.max))
            m_old = m_ref[g, :, cs]
            m_new = jnp.maximum(m_old, jnp.max(mx, axis=0, keepdims=True))
            sm = jnp.zeros((SUBLANES, qb), F32)
            for sb in range(nsub):
                rs = slice(sb * ATT_SUB, (sb + 1) * ATT_SUB)
                pr = jnp.exp2(lg_ref[g, rs, cs] - m_new)
                sm = sm + fold(pr, jnp.sum)
                pt_ref[g, rs, cs] = pr.astype(BF16)
            alpha = jnp.exp2(m_old - m_new)
            l_ref[g, :, cs] = alpha * l_ref[g, :, cs] + jnp.sum(sm, axis=0, keepdims=True)
            m_ref[g, :, cs] = m_new
            al_ref[g, :, cs] = alpha

    def weighted_values(g, c):
        pv = jnp.dot(vt_ref[g * ATT_HD:(g + 1) * ATT_HD, rows_of(c)], pt_ref[g],
                     preferred_element_type=F32)
        acc_ref[g] = acc_ref[g] * al_ref[g] + pv

    logits(0, 0)

    def att_body(c, carry):
        key = keys_ref[rows_of(c), :]
        dist = jnp.abs(pos - key_index(c)).astype(F32)
        dm_ref[...] = jnp.where(key >= thr_eff, dist, MASK_DIST)
        logits(1, c)
        softmax_block(0)
        weighted_values(0, c)
        logits(0, jnp.minimum(c + 1, nkc - 1))
        softmax_block(1)
        weighted_values(1, c)
        return carry
    lax.fori_loop(0, nkc, att_body, 0)

    for h in range(ATT_HEADS):
        g, r = divmod(h, ATT_REP)
        cs = slice(r * qb, (r + 1) * qb)
        ot = acc_ref[g, :, cs] / l_ref[g, :, cs]
        o_ref[:, h * ATT_HD:(h + 1) * ATT_HD] = ot.T.astype(BF16)


def _dsa(q_ops, k_all, vt_all, ki_all, *, nb, nqb, qb, past, l_real, kblk, out_rows=None):
    l_pad = k_all.shape[1]
    topk = min(IDX_TOPK, l_real // 4)
    assert kblk >= topk and kblk % ATT_SUB == 0 and l_pad % kblk == 0
    (qa, qa_map), (qi, qi_map), (wi, wi_map) = q_ops
    kern = functools.partial(_dsa_kernel, past=past, l_real=l_real, kblk=kblk, topk=topk)
    return pl.pallas_call(
        kern,
        grid=(nb, nqb),
        in_specs=[pl.BlockSpec((qb, ATT_W), qa_map), pl.BlockSpec((qb, IDX_HEADS * IDX_DIM), qi_map),
                  pl.BlockSpec((qb, LANES), wi_map),
                  pl.BlockSpec((None, l_pad, ATT_KV * ATT_HD), lambda b, i: (b, 0, 0)),
                  pl.BlockSpec((None, ATT_KV * ATT_HD, l_pad), lambda b, i: (b, 0, 0)),
                  pl.BlockSpec((None, l_pad, LANES), lambda b, i: (b, 0, 0))],
        out_specs=pl.BlockSpec((qb, ATT_W), lambda b, i: (b * nqb + i, 0)),
        out_shape=jax.ShapeDtypeStruct((out_rows or nb * nqb * qb, ATT_W), BF16),
        scratch_shapes=[pltpu.VMEM((l_pad, 2 * LANES), BF16),
                        pltpu.VMEM((l_pad, qb), I32),
                        pltpu.VMEM((l_pad, qb), I16),
                        pltpu.VMEM((l_pad, qb), I16),
                        pltpu.VMEM((2 * LANES, IDX_HEADS * qb), BF16),
                        pltpu.VMEM((ATT_KV, ATT_HD, ATT_REP * qb), BF16),
                        pltpu.VMEM((ATT_KV, 1, ATT_REP * qb), F32),
                        pltpu.VMEM((ATT_KV, 1, ATT_REP * qb), F32),
                        pltpu.VMEM((ATT_KV, ATT_HD, ATT_REP * qb), F32),
                        pltpu.VMEM((ATT_KV, kblk, ATT_REP * qb), F32),
                        pltpu.VMEM((ATT_KV, kblk, ATT_REP * qb), BF16),
                        pltpu.VMEM((kblk, qb), F32),
                        pltpu.VMEM((ATT_KV, 1, ATT_REP * qb), F32)],
        compiler_params=_cparams(("parallel", "arbitrary")),
        name="dsa",
    )(qa, qi, wi, k_all, vt_all, ki_all)


def _rw_prep_kernel(r_ref, k_ref, v_ref, lr_ref, hr_ref, hk_ref, hv_ref, hlr_ref,
                    sr_ref, sk_ref, sv_ref, slr_ref, mr_ref, mk_ref, mv_ref, mlr_ref,
                    w0_ref, a0_ref, w2_ref, a2_ref, g2_ref,
                    ro_ref, ko_ref, vo_ref, wo_ref, ao_ref, go_ref, *, tiles_per_seq):
    first_tile = (pl.program_id(0) % tiles_per_seq) == 0

    def mix(x_ref, halo_ref, shift_ref, mu_ref):
        x = x_ref[...]
        first = jnp.where(first_tile, shift_ref[...], halo_ref[SUBLANES - 1:SUBLANES, :])
        rows = lax.broadcasted_iota(I32, x.shape, 0)
        prev = jnp.where(rows == 0, first, pltpu.roll(x, 1, 0))
        return x + (prev - x) * mu_ref[...]

    ro_ref[...] = mix(r_ref, hr_ref, sr_ref, mr_ref)
    ko_ref[...] = mix(k_ref, hk_ref, sk_ref, mk_ref)
    vo_ref[...] = mix(v_ref, hv_ref, sv_ref, mv_ref)
    lr = mix(lr_ref, hlr_ref, slr_ref, mlr_ref)
    xw = lr[:, 0:128]
    xa = lr[:, 128:256]
    xg = lr[:, 256:512]
    wpre = w0_ref[...] + _dot3(jnp.tanh(xw), w2_ref[...])
    wo_ref[...] = jnp.exp(-RW_DECAY_SCALE * _sigmoid(wpre))
    ao_ref[...] = _sigmoid(a0_ref[...] + _dot3(xa, a2_ref[...]))
    go_ref[...] = _dot3(_sigmoid(xg), g2_ref[...])


def _rw_prep(proj, row0, nb, seq, shifts, mus, w0, a0, w2p, a2p, g2, tm, out_rows=None):
    m = nb * seq
    tps = seq // tm
    rb0 = row0 // tm
    hb = tm // SUBLANES
    cr, ck, cv = _OFF["r"] // RW_W, _OFF["k"] // RW_W, _OFF["v"] // RW_W
    clr = _OFF["xw"] // 512

    def main(c):
        return lambda i: (rb0 + i, c)

    def halo(c):
        return lambda i: (jnp.maximum((rb0 + i) * hb - 1, 0), c)

    bidx = lambda i: (i // tps, 0, 0)
    fix = lambda i: (0, 0)
    wide = [pl.BlockSpec((tm, RW_W), main(cr)), pl.BlockSpec((tm, RW_W), main(ck)),
            pl.BlockSpec((tm, RW_W), main(cv)), pl.BlockSpec((tm, 512), main(clr)),
            pl.BlockSpec((SUBLANES, RW_W), halo(cr)), pl.BlockSpec((SUBLANES, RW_W), halo(ck)),
            pl.BlockSpec((SUBLANES, RW_W), halo(cv)), pl.BlockSpec((SUBLANES, 512), halo(clr))]
    sh = [pl.BlockSpec((None, 1, RW_W), bidx)] * 3 + [pl.BlockSpec((None, 1, 512), bidx)]
    mu = [pl.BlockSpec((1, RW_W), fix)] * 3 + [pl.BlockSpec((1, 512), fix)]
    par = [pl.BlockSpec((1, RW_W), fix), pl.BlockSpec((1, RW_W), fix),
           pl.BlockSpec((128, RW_W), fix), pl.BlockSpec((128, RW_W), fix),
           pl.BlockSpec((RW_G_R, RW_W), fix)]
    out = jax.ShapeDtypeStruct((m, RW_W), F32)
    return pl.pallas_call(
        functools.partial(_rw_prep_kernel, tiles_per_seq=tps),
        grid=(m // tm,),
        in_specs=wide + sh + mu + par,
        out_specs=[pl.BlockSpec((tm, RW_W), lambda i: (i, 0))] * 6,
        out_shape=[out] * 5 + [jax.ShapeDtypeStruct((out_rows or m, RW_W), F32)],
        compiler_params=_cparams(("parallel",)),
        name="rwkv_prep",
    )(proj, proj, proj, proj, proj, proj, proj, proj, *shifts, *mus, w0, a0, w2p, a2p, g2)


def _rw_scan_kernel(r_ref, k_ref, v_ref, w_ref, a_ref, kkp_ref, kap_ref, rkp_ref,
                    lng_ref, lnb_ref, s0_ref, o_ref, st_ref, s_scr, tmp_scr):
    tb = pl.program_id(1)
    nsteps = r_ref.shape[0]

    @pl.when(tb == 0)
    def _():
        s_scr[...] = s0_ref[...]

    def step(t, carry):
        k = k_ref[t]
        a = a_ref[t]
        kk = k * kkp_ref[...]
        nrm = jnp.sqrt(jnp.sum(kk * kk, axis=0, keepdims=True))
        kk = kk / jnp.maximum(nrm, 1e-12)
        km = k * (1.0 + (a - 1.0) * kap_ref[...])
        tmp_scr[0] = kk
        tmp_scr[1] = kk * a
        tmp_scr[2] = km
        parts = [jnp.zeros((RW_HD, LANES), F32) for _ in range(2)]
        for kx in range(RW_HD):
            parts[kx % 2] = parts[kx % 2] + s_scr[kx] * tmp_scr[0, kx:kx + 1, :]
        sa = -(parts[0] + parts[1])
        v = v_ref[t]
        outs = [jnp.zeros((RW_HD, LANES), F32) for _ in range(2)]
        for kx in range(RW_HD):
            s_new = (s_scr[kx] * w_ref[t, kx:kx + 1, :] + sa * tmp_scr[1, kx:kx + 1, :]
                     + v * tmp_scr[2, kx:kx + 1, :])
            s_scr[kx] = s_new
            outs[kx % 2] = outs[kx % 2] + s_new * r_ref[t, kx:kx + 1, :]
        o = outs[0] + outs[1]
        mu = jnp.mean(o, axis=0, keepdims=True)
        oc = o - mu
        var = jnp.mean(oc * oc, axis=0, keepdims=True)
        on = oc * lax.rsqrt(var + RW_GN_EPS) * lng_ref[...] + lnb_ref[...]
        bonus = jnp.sum(r_ref[t] * tmp_scr[2] * rkp_ref[...], axis=0, keepdims=True) * v
        o_ref[t] = on + bonus
        return carry
    lax.fori_loop(0, nsteps, step, 0)

    @pl.when(tb == pl.num_programs(1) - 1)
    def _():
        st_ref[...] = s_scr[...]


def _rw_scan(r, k, v, w, a, params, s0, tb):
    seq, _, pairs = r.shape
    assert pairs % LANES == 0, "batch * RW_HEADS must fill whole lane groups"
    seqb = pl.BlockSpec((tb, RW_HD, LANES), lambda g, t: (t, 0, g))
    par = pl.BlockSpec((RW_HD, LANES), lambda g, t: (0, 0))
    st = pl.BlockSpec((RW_HD, RW_HD, LANES), lambda g, t: (0, 0, g))
    return pl.pallas_call(
        _rw_scan_kernel,
        grid=(pairs // LANES, seq // tb),
        in_specs=[seqb] * 5 + [par] * 5 + [st],
        out_specs=[seqb, st],
        out_shape=[jax.ShapeDtypeStruct((seq, RW_HD, pairs), F32),
                   jax.ShapeDtypeStruct((RW_HD, RW_HD, pairs), F32)],
        scratch_shapes=[pltpu.VMEM((RW_HD, RW_HD, LANES), F32),
                        pltpu.VMEM((3, RW_HD, LANES), F32)],
        compiler_params=_cparams(("parallel", "arbitrary")),
        name="rwkv_scan",
    )(r, k, v, w, a, *params, s0)


GLA_SUB = 16


def _gla_kernel(q_ref, k_ref, v_ref, gl_ref, og_ref, w2_ref, gb_ref, ng_ref, s0_ref,
                o_ref, st_ref, s_scr, la_scr, att_scr):
    tblk = pl.program_id(1)
    tc = q_ref.shape[0]

    @pl.when(tblk == 0)
    def _():
        for h in range(GLA_HEADS):
            s_scr[h] = s0_ref[h].T

    la_scr[...] = _log_sigmoid(_dot3(gl_ref[...], w2_ref[...]) + gb_ref[...]) * (1.0 / GLA_GATE_NORM)
    rows = lax.broadcasted_iota(I32, (CHUNK, GLA_DK), 0)
    lane = lax.broadcasted_iota(I32, (GLA_SUB, CHUNK), 1)
    jrow = lax.broadcasted_iota(I32, (GLA_SUB, GLA_DK), 0)
    nsub = CHUNK // GLA_SUB

    def head_chunk(h, rs):
        ks = slice(h * GLA_DK, (h + 1) * GLA_DK)
        vs = slice(h * GLA_DV, (h + 1) * GLA_DV)
        b = la_scr[rs, ks]
        sh = 1
        while sh < CHUNK:
            b = b + jnp.where(rows >= sh, pltpu.roll(b, sh, 0), 0.0)
            sh *= 2
        q = q_ref[rs, ks] * (GLA_DK ** -0.5)
        k = k_ref[rs, ks]
        v = v_ref[rs, vs]
        s_old = s_scr[h]
        inter = _nt_dot(q * jnp.exp(b), s_old)
        b_last = b[CHUNK - 1:CHUNK, :]

        for jb in range(nsub):
            js = slice(jb * GLA_SUB, (jb + 1) * GLA_SUB)
            bj = b[js]
            kj = k[js]
            blk = jnp.zeros((GLA_SUB, CHUNK), F32)
            for ii in range(GLA_SUB):
                i_row = jb * GLA_SUB + ii
                e = jnp.exp(jnp.minimum(b[i_row:i_row + 1, :] - bj, 0.0))
                val = jnp.where(jrow <= ii, q[i_row:i_row + 1, :] * kj * e, 0.0)
                col = jnp.sum(val, axis=1, keepdims=True)
                blk = jnp.where(lane == i_row, col, blk)
            if jb + 1 < nsub:
                i0 = (jb + 1) * GLA_SUB
                be = b[i0 - 1:i0, :]
                kt = kj * jnp.exp(be - bj)
                later = rows >= i0
                qt = jnp.where(later, q * jnp.exp(jnp.where(later, b - be, 0.0)), 0.0)
                blk = blk + _nt_dot(kt, qt)
            att_scr[h, js, :] = blk
        o = inter + _tn_dot(att_scr[h], v)
        khat = k * jnp.exp(b_last - b)
        s_scr[h] = s_old * jnp.exp(b_last) + _tn_dot(v, khat)
        o = o * lax.rsqrt(jnp.mean(o * o, -1, keepdims=True) + GLA_NORM_EPS) * ng_ref[:, vs]
        og = og_ref[rs, vs]
        o_ref[rs, vs] = (o * (og * _sigmoid(og))).astype(BF16)

    def chunk_body(c, carry):
        rs = pl.ds(pl.multiple_of(c * CHUNK, CHUNK), CHUNK)
        for h in range(GLA_HEADS):
            head_chunk(h, rs)
        return carry
    lax.fori_loop(0, tc // CHUNK, chunk_body, 0)

    @pl.when(tblk == pl.num_programs(1) - 1)
    def _():
        for h in range(GLA_HEADS):
            st_ref[h] = s_scr[h].T


def _gla(proj, tail, row0, nb, seq, w2p, gb, ng, s0, tc, out_rows=None):
    ntb = seq // tc
    rb0 = row0 // tc
    cq, ck = _OFF["qc"] // GLA_KW, _OFF["kc"] // GLA_KW
    cv, cg = _OFF["vc"] // GLA_VW, _OFF["og"] // GLA_VW
    cgl = (_OFF["gl"] - _OFF["ka"]) // LANES

    def rows(c0):
        return lambda b, t: (rb0 + b * ntb + t, c0)

    fix = lambda b, t: (0, 0)
    st = pl.BlockSpec((None, GLA_HEADS, GLA_DK, GLA_DV), lambda b, t: (b, 0, 0, 0))
    return pl.pallas_call(
        _gla_kernel,
        grid=(nb, ntb),
        in_specs=[pl.BlockSpec((tc, GLA_KW), rows(cq)), pl.BlockSpec((tc, GLA_KW), rows(ck)),
                  pl.BlockSpec((tc, GLA_VW), rows(cv)), pl.BlockSpec((tc, LANES), rows(cgl)),
                  pl.BlockSpec((tc, GLA_VW), rows(cg)),
                  pl.BlockSpec((LANES, GLA_KW), fix), pl.BlockSpec((1, GLA_KW), fix),
                  pl.BlockSpec((1, GLA_VW), fix), st],
        out_specs=[pl.BlockSpec((tc, GLA_VW), lambda b, t: (b * ntb + t, 0)), st],
        out_shape=[jax.ShapeDtypeStruct((out_rows or nb * seq, GLA_VW), BF16),
                   jax.ShapeDtypeStruct((nb, GLA_HEADS, GLA_DK, GLA_DV), F32)],
        scratch_shapes=[pltpu.VMEM((GLA_HEADS, GLA_DV, GLA_DK), F32), pltpu.VMEM((tc, GLA_KW), F32),
                        pltpu.VMEM((GLA_HEADS, CHUNK, CHUNK), F32)],
        compiler_params=_cparams(("parallel", "arbitrary")),
        name="gla",
    )(proj, proj, proj, tail, proj, w2p, gb, ng, s0)


def _pack_cols(w):
    parts = []
    for _, s, wd, p in _SEGS:
        parts.append(w[..., s:s + wd])
        if p > wd:
            parts.append(jnp.zeros(w.shape[:-1] + (p - wd,), w.dtype))
    return jnp.concatenate(parts, axis=-1)


def _pack_rw(vec):
    r, k, v = vec[..., 0:1024], vec[..., 1024:2048], vec[..., 2048:3072]
    z = jnp.zeros(vec.shape[:-1] + (32,), vec.dtype)
    lr = jnp.concatenate([vec[..., 3072:3168], z, vec[..., 3168:3264], z, vec[..., 3264:3520]], -1)
    return r, k, v, lr


def _pad_rows(w, n):
    return jnp.concatenate([w, jnp.zeros((n - w.shape[0],) + w.shape[1:], w.dtype)], 0)


RELAYOUT_T = 128


def _to_scan_kernel(x_ref, o_ref, y_scr):
    nbat = x_ref.shape[0]
    for b in range(nbat):
        for c in range(RW_W // LANES):
            y_scr[b, c * LANES:(c + 1) * LANES, :] = x_ref[b, :, c * LANES:(c + 1) * LANES].T
    for n in range(RW_HD):
        z = jnp.concatenate([y_scr[b, pl.ds(n, RW_HEADS, stride=RW_HD), :] for b in range(nbat)], axis=0)
        o_ref[:, n, :] = z.T


def _from_scan_kernel(o_ref, x_ref, y_scr):
    nbat = x_ref.shape[0]
    for n in range(RW_HD):
        zt = o_ref[:, n, :].T
        for b in range(nbat):
            y_scr[b, pl.ds(n, RW_HEADS, stride=RW_HD), :] = zt[b * RW_HEADS:(b + 1) * RW_HEADS]
    for b in range(nbat):
        for c in range(RW_W // LANES):
            x_ref[b, :, c * LANES:(c + 1) * LANES] = y_scr[b, c * LANES:(c + 1) * LANES, :].T


def _relayout_ok(nb, seq):
    return nb * RW_HEADS == LANES and seq % RELAYOUT_T == 0


def _to_scan(x, nb, seq):
    if not _relayout_ok(nb, seq):
        return jnp.transpose(x.reshape(nb, seq, RW_HEADS, RW_HD), (1, 3, 0, 2)).reshape(seq, RW_HD, nb * RW_HEADS)
    return pl.pallas_call(
        _to_scan_kernel,
        grid=(seq // RELAYOUT_T,),
        in_specs=[pl.BlockSpec((nb, RELAYOUT_T, RW_W), lambda t: (0, t, 0))],
        out_specs=pl.BlockSpec((RELAYOUT_T, RW_HD, LANES), lambda t: (t, 0, 0)),
        out_shape=jax.ShapeDtypeStruct((seq, RW_HD, LANES), F32),
        scratch_shapes=[pltpu.VMEM((nb, RW_W, RELAYOUT_T), F32)],
        compiler_params=_cparams(("parallel",)),
        name="to_scan_layout",
    )(x.reshape(nb, seq, RW_W))


def _from_scan(o, nb, seq):
    if not _relayout_ok(nb, seq):
        return jnp.transpose(o.reshape(seq, RW_HD, nb, RW_HEADS), (2, 0, 3, 1)).reshape(nb * seq, RW_W)
    return pl.pallas_call(
        _from_scan_kernel,
        grid=(seq // RELAYOUT_T,),
        in_specs=[pl.BlockSpec((RELAYOUT_T, RW_HD, LANES), lambda t: (t, 0, 0))],
        out_specs=pl.BlockSpec((nb, RELAYOUT_T, RW_W), lambda t: (0, t, 0)),
        out_shape=jax.ShapeDtypeStruct((nb, seq, RW_W), F32),
        scratch_shapes=[pltpu.VMEM((nb, RW_W, RELAYOUT_T), F32)],
        compiler_params=_cparams(("parallel",)),
        name="from_scan_layout",
    )(o).reshape(nb * seq, RW_W)


def _group_mixers(proj, tail, row0, nb, seq, caches, lw, out_rows=None):
    m = nb * seq
    cache_k, cache_v, cache_ki, s_rw, s_shift, s_gla = caches
    past = 0 if cache_k is None else cache_k.shape[1]
    l_real = past + seq
    t0 = _OFF["ka"]
    grp = tail[row0:row0 + m].reshape(nb, seq, -1)

    ka = grp[..., _OFF["ka"] - t0:_OFF["ka"] - t0 + 256]
    va = grp[..., _OFF["va"] - t0:_OFF["va"] - t0 + 256]
    ki = grp[..., _OFF["ki"] - t0:_OFF["ki"] - t0 + LANES]
    if past:
        k_all = jnp.concatenate([cache_k.reshape(nb, past, 256), ka], 1)
        v_all = jnp.concatenate([cache_v.reshape(nb, past, 256), va], 1)
        cki = jnp.concatenate([cache_ki, jnp.zeros((nb, past, LANES - IDX_DIM), F32)], -1)
        ki_all = jnp.concatenate([cki, ki], 1)
    else:
        k_all, v_all, ki_all = ka, va, ki
    kblk = 512 if l_real % 512 == 0 else 384
    l_pad = -(-l_real // kblk) * kblk
    if l_pad > l_real:
        padk = lambda t: jnp.concatenate([t, jnp.zeros((nb, l_pad - l_real, t.shape[-1]), t.dtype)], 1)
        k_all, v_all, ki_all = padk(k_all), padk(v_all), padk(ki_all)
    k_bf = k_all.astype(BF16)
    vt_bf = jnp.swapaxes(v_all, 1, 2).astype(BF16)
    qb = DSA_QB if seq % DSA_QB == 0 else LANES
    if seq % qb == 0:
        nqb = seq // qb
        rb0 = row0 // qb
        mk = lambda c: (lambda b, i: (rb0 + b * nqb + i, c))
        q_ops = ((proj, mk(_OFF["qa"] // ATT_W)), (proj, mk(_OFF["qi"] // 1024)),
                 (tail, mk((_OFF["wi"] - t0) // LANES)))
        o_a = _dsa(q_ops, k_bf, vt_bf, ki_all, nb=nb, nqb=nqb, qb=qb, past=past, l_real=l_real, kblk=kblk,
                   out_rows=out_rows)
    else:
        def padq(t):
            return jnp.concatenate([t, jnp.zeros((nb, qb - seq, t.shape[-1]), F32)], 1).reshape(nb * qb, -1)
        qrows = proj[row0:row0 + m, 0:_OFF["vc"]].reshape(nb, seq, -1)
        mk = lambda b, i: (b, 0)
        q_ops = ((padq(qrows[..., 0:ATT_W]), mk), (padq(qrows[..., ATT_W:]), mk),
                 (padq(grp[..., _OFF["wi"] - t0:_OFF["wi"] - t0 + LANES]), mk))
        o_a = _dsa(q_ops, k_bf, vt_bf, ki_all, nb=nb, nqb=1, qb=qb, past=past, l_real=l_real, kblk=kblk)
        o_a = o_a.reshape(nb, qb, ATT_W)[:, :seq].reshape(m, ATT_W)

    if s_shift is None:
        s_shift = jnp.zeros((nb, RW_PROJ), F32)
    shifts = [t.reshape(nb, 1, -1) for t in _pack_rw(s_shift)]
    tm = 256 if seq % 256 == 0 else seq
    r, k, v, w, a, g = _rw_prep(proj, row0, nb, seq, shifts, lw["rw_mus"], lw["rw_w0"], lw["rw_a0"],
                                lw["rw_w2p"], lw["rw_a2p"], lw["rw_g2"], tm, out_rows)
    if s_rw is None:
        s0 = jnp.zeros((RW_HD, RW_HD, nb * RW_HEADS), F32)
    else:
        s0 = jnp.transpose(s_rw, (3, 2, 0, 1)).reshape(RW_HD, RW_HD, nb * RW_HEADS)
    tile = lambda p: jnp.tile(p.reshape(RW_HEADS, RW_HD).T, (1, LANES // RW_HEADS))
    params = [tile(lw[n]) for n in ("rw_kk", "rw_ka", "rw_rk", "rw_ln_g", "rw_ln_b")]
    tb = 32 if seq % 32 == 0 else seq
    o_b, s_t = _rw_scan(*[_to_scan(t, nb, seq) for t in (r, k, v, w, a)], params, s0, tb)
    o_b = _from_scan(o_b, nb, seq)
    s_rw_new = jnp.transpose(s_t.reshape(RW_HD, RW_HD, nb, RW_HEADS), (2, 3, 1, 0))
    last = jnp.take(proj, row0 + seq - 1 + seq * jnp.arange(nb), axis=0)[:, _OFF["r"]:_OFF["qc"]]
    base = _OFF["r"]
    shift_new = jnp.concatenate(
        [last[:, 0:3072], last[:, _OFF["xw"] - base:_OFF["xw"] - base + 96],
         last[:, _OFF["xa"] - base:_OFF["xa"] - base + 96], last[:, _OFF["xg"] - base:]], -1)

    if s_gla is None:
        s_gla = jnp.zeros((nb, GLA_HEADS, GLA_DK, GLA_DV), F32)
    tc = 512 if seq % 512 == 0 else seq
    o_c, s_gla_new = _gla(proj, tail, row0, nb, seq, lw["gla_w2p"], lw["gla_b"], lw["gla_ng"], s_gla, tc,
                          out_rows)

    new_state = (ka.reshape(nb, seq, ATT_KV, ATT_HD), va.reshape(nb, seq, ATT_KV, ATT_HD),
                 ki[..., :IDX_DIM], s_rw_new, shift_new, s_gla_new)
    return o_a, o_b, g, o_c, new_state


def kernel(x_prompt, x_sample, cache_k, cache_v, cache_kidx, state_rwkv, state_rwkv_shift, state_gla,
           w_in, rw_mu, rw_w0, rw_w2, rw_a0, rw_a2, rw_g2, rw_kk, rw_ka, rw_rk, rw_ln_g, rw_ln_b,
           gla_w2, gla_b, gla_norm_g, w_branch_a, w_branch_b, w_branch_c, w_out,
           ln1_g, ln1_b, ln2_g, ln2_b, ffn_w1, ffn_w3, ffn_w2):
    bp, tp, _ = x_prompt.shape
    bs, ts, _ = x_sample.shape
    mp, ms = bp * tp, bs * ts
    x = jnp.concatenate([x_prompt.reshape(mp, D_MODEL), x_sample.reshape(ms, D_MODEL)], 0)
    xb = x.astype(BF16)
    m = mp + ms
    tm_big = 1024 if m % 1024 == 0 else 256
    tm_mid = 512 if m % 512 == 0 else 256

    w_in_p = _pack_cols(w_in).astype(BF16)
    new_p = [[] for _ in range(6)]
    new_s = [[] for _ in range(6)]
    for l in range(DEPTH):
        lw = {
            "rw_mus": [t.reshape(1, -1) for t in _pack_rw(rw_mu[l])],
            "rw_w0": rw_w0[l].reshape(1, RW_W), "rw_a0": rw_a0[l].reshape(1, RW_W),
            "rw_w2p": _pad_rows(rw_w2[l], 128), "rw_a2p": _pad_rows(rw_a2[l], 128), "rw_g2": rw_g2[l],
            "rw_kk": rw_kk[l], "rw_ka": rw_ka[l], "rw_rk": rw_rk[l],
            "rw_ln_g": rw_ln_g[l], "rw_ln_b": rw_ln_b[l],
            "gla_w2p": _pad_rows(gla_w2[l], LANES), "gla_b": gla_b[l].reshape(1, GLA_KW),
            "gla_ng": gla_norm_g[l].reshape(1, GLA_VW),
        }
        proj = _matmul(xb, w_in_p[l, :, :_OFF["ka"]], tm_big, 1024)
        tail = _matmul(xb, w_in_p[l, :, _OFF["ka"]:], tm_big, PROJ_W - _OFF["ka"])
        oa_p, ob_p, g_p, oc_p, st_p = _group_mixers(proj, tail, 0, bp, tp, (None,) * 6, lw, out_rows=m)
        oa_s, ob_s, g_s, oc_s, st_s = _group_mixers(
            proj, tail, mp, bs, ts,
            (cache_k[l], cache_v[l], cache_kidx[l], state_rwkv[l], state_rwkv_shift[l], state_gla[l]), lw)
        put = lambda full, part: lax.dynamic_update_slice(full, part, (mp, 0))
        merged = _merge(put(oa_p, oa_s), jnp.concatenate([ob_p, ob_s], 0), put(g_p, g_s), put(oc_p, oc_s), proj,
                        w_branch_a[l].astype(BF16), w_branch_b[l].astype(BF16),
                        w_branch_c[l].astype(BF16), 256, D_MODEL)
        h, hb = _hproj(merged, w_out[l].astype(BF16), x, ln1_g[l].reshape(1, -1),
                       ln1_b[l].reshape(1, -1), tm_mid)
        x, xb = _ffn(hb, h, ffn_w1[l].astype(BF16), ffn_w3[l].astype(BF16), ffn_w2[l].astype(BF16),
                     ln2_g[l].reshape(1, -1), ln2_b[l].reshape(1, -1), tm_mid, 512)
        for i in range(6):
            new_p[i].append(st_p[i])
            new_s[i].append(st_s[i])
    sp = [jnp.stack(a) for a in new_p]
    ss = [jnp.stack(a) for a in new_s]
    yp = x[:mp].reshape(bp, tp, D_MODEL)
    ys = x[mp:].reshape(bs, ts, D_MODEL)
    return (yp, ys, sp[0], sp[1], sp[2], sp[3], sp[4], sp[5], ss[0], ss[1], ss[2], ss[3], ss[4], ss[5])
```

```python
import functools

import jax
import jax.numpy as jnp
from jax import lax
from jax.experimental import pallas as pl
from jax.experimental.pallas import tpu as pltpu

F32 = jnp.float32
BF16 = jnp.bfloat16
I32 = jnp.int32
I16 = jnp.int16

D_MODEL = 2048
DEPTH = 4
CHUNK = 64
Q_BLOCK = 128
ATT_HEADS = 8
ATT_HD = 128
ATT_KV = 2
ATT_REP = ATT_HEADS // ATT_KV
ATT_W = ATT_HEADS * ATT_HD
IDX_HEADS = 16
IDX_DIM = 64
IDX_TOPK = 256
IDX_SCALE = (IDX_DIM * IDX_HEADS) ** -0.5
RW_HEADS = 16
RW_HD = 64
RW_W = RW_HEADS * RW_HD
RW_DECAY_R = 96
RW_A_R = 96
RW_G_R = 256
RW_PROJ = 3 * RW_W + RW_DECAY_R + RW_A_R + RW_G_R
RW_DECAY_SCALE = 0.606531
RW_GN_EPS = 64e-5
GLA_HEADS = 4
GLA_DK = 128
GLA_DV = 256
GLA_KW = GLA_HEADS * GLA_DK
GLA_VW = GLA_HEADS * GLA_DV
GLA_GATE_R = 16
GLA_GATE_NORM = 16.0
GLA_NORM_EPS = 1e-5
FFN_HIDDEN = -(-8 * D_MODEL // (3 * 256)) * 256
DN_ALPHA = (2 * DEPTH) ** 0.25
LN_EPS = 1e-5
LOG2E = 1.4426950408889634

LANES = 128
SUBLANES = 8
PACKED_ROWS = 16
VMEM_LIMIT = 56 * 1024 * 1024

_SEGS = (
    ("qa", 0, 1024, 1024), ("qi", 1536, 1024, 1024), ("vc", 7184, 1024, 1024),
    ("og", 8224, 1024, 1024), ("gates", 9248, 6144, 6144),
    ("r", 2640, 1024, 1024), ("k", 3664, 1024, 1024), ("v", 4688, 1024, 1024),
    ("xw", 5712, 96, 128), ("xa", 5808, 96, 128), ("xg", 5904, 256, 256),
    ("qc", 6160, 512, 512), ("kc", 6672, 512, 512),
    ("ka", 1024, 256, 256), ("va", 1280, 256, 256),
    ("ki", 2560, 64, 128), ("wi", 2624, 16, 128), ("gl", 8208, 16, 128),
)
_OFF = {}
_o = 0
for _n, _s, _w, _p in _SEGS:
    _OFF[_n] = _o
    _o += _p
PROJ_W = _o
KEY_NEG_INF = -0x7F800000
MASK_DIST = 1e32
M_FLOOR = -1e29


def _cparams(sem):
    return pltpu.CompilerParams(dimension_semantics=sem, vmem_limit_bytes=VMEM_LIMIT)


def _split2(x):
    hi = x.astype(BF16)
    lo = (x - hi.astype(F32)).astype(BF16)
    return hi, lo


def _dot3(a, b):
    ah, al = _split2(a)
    bh, bl = _split2(b)
    d = functools.partial(jnp.dot, preferred_element_type=F32)
    return d(ah, bh) + (d(al, bh) + d(ah, bl))


def _nt_dot(a, b):
    return lax.dot_general(a.astype(BF16), b.astype(BF16), (((1,), (1,)), ((), ())),
                           preferred_element_type=F32)


def _tn_dot(a, b):
    return lax.dot_general(a.astype(BF16), b.astype(BF16), (((0,), (0,)), ((), ())),
                           preferred_element_type=F32)


def _sigmoid(x):
    return 1.0 / (1.0 + jnp.exp(-x))


def _log_sigmoid(x):
    return jnp.minimum(x, 0.0) - jnp.log(1.0 + jnp.exp(-jnp.abs(x)))


def _layer_norm(x, g, b):
    mu = jnp.mean(x, -1, keepdims=True)
    xc = x - mu
    var = jnp.mean(xc * xc, -1, keepdims=True)
    return xc * lax.rsqrt(var + LN_EPS) * g + b


def _mm_kernel(a_ref, b_ref, o_ref):
    o_ref[...] = jnp.dot(a_ref[...], b_ref[...], preferred_element_type=F32)


def _matmul(a, b, tm, tn):
    m, k = a.shape
    n = b.shape[1]
    return pl.pallas_call(
        _mm_kernel,
        grid=(m // tm, pl.cdiv(n, tn)),
        in_specs=[pl.BlockSpec((tm, k), lambda i, j: (i, 0)),
                  pl.BlockSpec((k, tn), lambda i, j: (0, j))],
        out_specs=pl.BlockSpec((tm, tn), lambda i, j: (i, j)),
        out_shape=jax.ShapeDtypeStruct((m, n), F32),
        compiler_params=_cparams(("parallel", "arbitrary")),
        name="in_proj",
    )(a, b)


def _merge_kernel(oa_ref, ob_ref, g_ref, oc_ref, ga_ref, gb_ref, gc_ref,
                  wa_ref, wb_ref, wc_ref, o_ref):
    ob = (ob_ref[...] * g_ref[...]).astype(BF16)
    d = functools.partial(jnp.dot, preferred_element_type=F32)
    acc = _sigmoid(ga_ref[...]) * d(oa_ref[...], wa_ref[...])
    acc += _sigmoid(gb_ref[...]) * d(ob, wb_ref[...])
    acc += _sigmoid(gc_ref[...]) * d(oc_ref[...], wc_ref[...])
    o_ref[...] = acc.astype(BF16)


def _merge(oa, ob, g, oc, proj, wa, wb, wc, tm, tn):
    m = oa.shape[0]
    gblk = _OFF["gates"] // tn
    nj = D_MODEL // tn
    row = lambda i, j: (i, 0)
    wspec = pl.BlockSpec((ATT_W, tn), lambda i, j: (0, j))
    return pl.pallas_call(
        _merge_kernel,
        grid=(m // tm, nj),
        in_specs=[pl.BlockSpec((tm, ATT_W), row), pl.BlockSpec((tm, RW_W), row),
                  pl.BlockSpec((tm, RW_W), row), pl.BlockSpec((tm, GLA_VW), row),
                  pl.BlockSpec((tm, tn), lambda i, j: (i, gblk + j)),
                  pl.BlockSpec((tm, tn), lambda i, j: (i, gblk + nj + j)),
                  pl.BlockSpec((tm, tn), lambda i, j: (i, gblk + 2 * nj + j)),
                  wspec, wspec, wspec],
        out_specs=pl.BlockSpec((tm, tn), lambda i, j: (i, j)),
        out_shape=jax.ShapeDtypeStruct((m, D_MODEL), BF16),
        compiler_params=_cparams(("parallel", "arbitrary")),
        name="merge",
    )(oa, ob, g, oc, proj, proj, proj, wa, wb, wc)


def _hproj_kernel(m_ref, w_ref, x_ref, g_ref, b_ref, h_ref, hb_ref):
    z = DN_ALPHA * x_ref[...] + jnp.dot(m_ref[...], w_ref[...], preferred_element_type=F32)
    h = _layer_norm(z, g_ref[...], b_ref[...])
    h_ref[...] = h
    hb_ref[...] = h.astype(BF16)


def _hproj(merged, w_out, x, g, b, tm):
    m = x.shape[0]
    row = lambda i: (i, 0)
    fix = lambda i: (0, 0)
    return pl.pallas_call(
        _hproj_kernel,
        grid=(m // tm,),
        in_specs=[pl.BlockSpec((tm, D_MODEL), row), pl.BlockSpec((D_MODEL, D_MODEL), fix),
                  pl.BlockSpec((tm, D_MODEL), row), pl.BlockSpec((1, D_MODEL), fix),
                  pl.BlockSpec((1, D_MODEL), fix)],
        out_specs=[pl.BlockSpec((tm, D_MODEL), row), pl.BlockSpec((tm, D_MODEL), row)],
        out_shape=[jax.ShapeDtypeStruct((m, D_MODEL), F32),
                   jax.ShapeDtypeStruct((m, D_MODEL), BF16)],
        compiler_params=_cparams(("parallel",)),
        name="out_proj_ln",
    )(merged, w_out, x, g, b)


def _ffn_kernel(hb_ref, h_ref, w1_ref, w3_ref, w2_ref, g_ref, b_ref, y_ref, yb_ref, acc_ref):
    j = pl.program_id(1)

    @pl.when(j == 0)
    def _():
        acc_ref[...] = jnp.zeros_like(acc_ref)

    hb = hb_ref[...]
    u1 = jnp.dot(hb, w1_ref[...], preferred_element_type=F32)
    u3 = jnp.dot(hb, w3_ref[...], preferred_element_type=F32)
    u = (u1 * _sigmoid(u1) * u3).astype(BF16)
    acc_ref[...] += jnp.dot(u, w2_ref[...], preferred_element_type=F32)

    @pl.when(j == pl.num_programs(1) - 1)
    def _():
        y = _layer_norm(DN_ALPHA * h_ref[...] + acc_ref[...], g_ref[...], b_ref[...])
        y_ref[...] = y
        yb_ref[...] = y.astype(BF16)


def _ffn(hb, h, w1, w3, w2, g, b, tm, th):
    m = h.shape[0]
    row = lambda i, j: (i, 0)
    fix = lambda i, j: (0, 0)
    return pl.pallas_call(
        _ffn_kernel,
        grid=(m // tm, FFN_HIDDEN // th),
        in_specs=[pl.BlockSpec((tm, D_MODEL), row), pl.BlockSpec((tm, D_MODEL), row),
                  pl.BlockSpec((D_MODEL, th), lambda i, j: (0, j)),
                  pl.BlockSpec((D_MODEL, th), lambda i, j: (0, j)),
                  pl.BlockSpec((th, D_MODEL), lambda i, j: (j, 0)),
                  pl.BlockSpec((1, D_MODEL), fix), pl.BlockSpec((1, D_MODEL), fix)],
        out_specs=[pl.BlockSpec((tm, D_MODEL), row), pl.BlockSpec((tm, D_MODEL), row)],
        out_shape=[jax.ShapeDtypeStruct((m, D_MODEL), F32),
                   jax.ShapeDtypeStruct((m, D_MODEL), BF16)],
        scratch_shapes=[pltpu.VMEM((tm, D_MODEL), F32)],
        compiler_params=_cparams(("parallel", "arbitrary")),
        name="ffn_ln",
    )(hb, h, w1, w3, w2, g, b)


ATT_SUB = 128
DSA_QB = 256


def _dsa_kernel(qa_ref, qi_ref, wi_ref, k_ref, vt_ref, ki_ref, o_ref,
                kcat_ref, keys_ref, hi_ref, lo_ref, wq_ref, qt_ref, m_ref, l_ref, acc_ref,
                lg_ref, pt_ref, dm_ref, al_ref, red_ref, *, past, l_real, kblk, topk):
    qb = qa_ref.shape[0]
    l_pad = ki_ref.shape[0]
    i = pl.program_id(1)
    kf = float(topk)

    @pl.when(i == 0)
    def _():
        def prep(c, carry):
            r0 = pl.multiple_of(c * kblk, kblk)
            x = ki_ref[pl.ds(r0, kblk), :]
            hi = x.astype(BF16)
            lo = x - hi.astype(F32)
            a = (hi.astype(F32) + pltpu.roll(lo, IDX_DIM, 1)).astype(BF16)
            kcat_ref[pl.ds(r0, kblk), 0:LANES] = a
            kcat_ref[pl.ds(r0, kblk), LANES:2 * LANES] = hi
            return carry
        lax.fori_loop(0, l_pad // kblk, prep, 0)

    qit = qi_ref[...].T
    qhi = qit.astype(BF16)
    qlo = (qit - qhi.astype(F32)).astype(BF16)
    zero = jnp.zeros((IDX_DIM, qb), BF16)
    for h in range(IDX_HEADS):
        hs = slice(h * IDX_DIM, (h + 1) * IDX_DIM)
        cs = slice(h * qb, (h + 1) * qb)
        wq_ref[0:64, cs] = qhi[hs]
        wq_ref[64:128, cs] = qhi[hs]
        wq_ref[128:192, cs] = qlo[hs]
        wq_ref[192:256, cs] = zero
    wit = wi_ref[...].T
    qat = (qa_ref[...] * (ATT_HD ** -0.5 * LOG2E)).T.astype(BF16)
    for h in range(ATT_HEADS):
        g, r = divmod(h, ATT_REP)
        qt_ref[g, :, r * qb:(r + 1) * qb] = qat[h * ATT_HD:(h + 1) * ATT_HD]

    q0 = past + i * qb
    pos = q0 + lax.broadcasted_iota(I32, (1, qb), 1)
    pos_chunk = lax.shift_right_logical(pos, 6)
    adm_keys = jnp.minimum(l_real, (lax.shift_right_logical(q0 + qb - 1, 6) + 1) * CHUNK)
    nkc = lax.div(adm_keys + (kblk - 1), kblk)

    def rows_of(c):
        return pl.ds(pl.multiple_of(c * kblk, kblk), kblk)

    def key_index(c):
        return c * kblk + lax.broadcasted_iota(I32, (kblk, qb), 0)

    def score_body(c, carry):
        kc = kcat_ref[rows_of(c), :]
        acc = jnp.zeros((kblk, qb), F32)
        for hg in range(IDX_HEADS // 4):
            rel = jnp.dot(kc, wq_ref[:, hg * 4 * qb:(hg + 1) * 4 * qb],
                          preferred_element_type=F32)
            for hh in range(4):
                h = hg * 4 + hh
                acc = acc + jnp.maximum(rel[:, hh * qb:(hh + 1) * qb], 0.0) * wit[h:h + 1, :]
        score = acc * IDX_SCALE
        s_idx = key_index(c)
        adm = (lax.shift_right_logical(s_idx, 6) <= pos_chunk) & (s_idx < l_real)
        score = jnp.where(adm, score, -jnp.inf)
        bits = pltpu.bitcast(score, I32)
        key = jnp.where(bits < 0, -(bits & 0x7FFFFFFF), bits)
        keys_ref[rows_of(c), :] = key
        hi_ref[rows_of(c), :] = lax.shift_right_arithmetic(key, 16).astype(I16)
        return carry
    lax.fori_loop(0, nkc, score_body, 0)

    n_acc = 4
    one16 = jnp.ones((), I16)
    zero16 = jnp.zeros((), I16)

    def all_rows(x, op, init, slot):
        red_ref[slot] = x
        cols = []
        for c0 in range(0, qb, LANES):
            even = jnp.full((SUBLANES, LANES), init, F32)
            odd = jnp.full((SUBLANES, LANES), init, F32)
            for r in range(0, SUBLANES, 2):
                even = op(even, red_ref[slot, r:r + 1, c0:c0 + LANES])
                odd = op(odd, red_ref[slot, r + 1:r + 2, c0:c0 + LANES])
            cols.append(op(even, odd))
        return jnp.concatenate(cols, axis=1)

    def all_rows_sum(x, slot=0):
        return all_rows(x, jnp.add, 0.0, slot)

    def count16(ref, cand):
        c16 = jnp.concatenate([cand, cand], axis=0).astype(I16)[None]

        def body(c, accs):
            blk = ref[rows_of(c), :].reshape(kblk // PACKED_ROWS, PACKED_ROWS, qb)
            ind = jnp.where(blk >= c16, one16, zero16)
            accs = list(accs)
            for s in range(kblk // PACKED_ROWS):
                accs[s % n_acc] = accs[s % n_acc] + ind[s]
            return tuple(accs)
        accs = lax.fori_loop(0, nkc, body, (jnp.zeros((PACKED_ROWS, qb), I16),) * n_acc)
        tot = (accs[0].astype(I32) + accs[1].astype(I32)) + (accs[2].astype(I32) + accs[3].astype(I32))
        return all_rows_sum((tot[0:SUBLANES] + tot[SUBLANES:]).astype(F32))

    def search16(ref):
        ans = jnp.full((SUBLANES, qb), -32768, I32)
        cnt = jnp.full((SUBLANES, qb), 1e9, F32)
        for bit in range(15, -1, -1):
            cand = ans + (1 << bit)
            c = count16(ref, cand)
            ok = c >= kf
            ans = jnp.where(ok, cand, ans)
            cnt = jnp.where(ok, c, cnt)
        return ans[0:1], cnt[0:1]

    thr_hi, _ = search16(hi_ref)

    def low_body(c, carry):
        key = keys_ref[rows_of(c), :]
        hi = lax.shift_right_arithmetic(key, 16)
        lo = (key & 0xFFFF) - 32768
        k2 = jnp.where(hi > thr_hi, 32767, jnp.where(hi < thr_hi, -32768, lo))
        lo_ref[rows_of(c), :] = k2.astype(I16)
        return carry
    lax.fori_loop(0, nkc, low_body, 0)
    thr_lo, n_ge = search16(lo_ref)
    thr = thr_hi * 65536 + (thr_lo + 32768)
    thr_eff = jnp.maximum(thr, KEY_NEG_INF + 1)

    tied = (n_ge > kf) & (thr > KEY_NEG_INF)

    @pl.when(jnp.max(jnp.where(tied, 1.0, 0.0)) > 0.0)
    def _():
        def count32(pred_fn):
            def body(c, acc):
                ind = jnp.where(pred_fn(keys_ref[rows_of(c), :], key_index(c)), 1.0, 0.0)
                return acc + jnp.sum(ind.reshape(kblk // SUBLANES, SUBLANES, qb), axis=0)
            acc = lax.fori_loop(0, nkc, body, jnp.zeros((SUBLANES, qb), F32))
            return jnp.sum(acc, axis=0, keepdims=True)
        need = kf - count32(lambda key, s: key > thr)
        nbits = int(l_pad - 1).bit_length()

        def idx_body(it, p):
            cand = p + lax.shift_left(jnp.int32(1), nbits - 1 - it)
            few = count32(lambda key, s: (key == thr) & (s < cand)) < need
            return jnp.where(few, cand, p)
        p_last = lax.fori_loop(0, nbits, idx_body, jnp.zeros((1, qb), I32))

        def demote(c, carry):
            key = keys_ref[rows_of(c), :]
            drop = tied & (key == thr) & (key_index(c) > p_last)
            keys_ref[rows_of(c), :] = jnp.where(drop, thr - 1, key)
            return carry
        lax.fori_loop(0, nkc, demote, 0)

    m_ref[...] = jnp.full(m_ref.shape, M_FLOOR, F32)
    l_ref[...] = jnp.zeros(l_ref.shape, F32)
    acc_ref[...] = jnp.zeros(acc_ref.shape, F32)
    slopes = [float(2.0 ** (-8.0 * (h + 1) / ATT_HEADS)) * LOG2E for h in range(ATT_HEADS)]
    nsub = kblk // ATT_SUB
    rows8 = lambda t: t.reshape(t.shape[0] // SUBLANES, SUBLANES, t.shape[1])

    def logits(g, c):
        lg_ref[g] = jnp.dot(k_ref[rows_of(c), g * ATT_HD:(g + 1) * ATT_HD], qt_ref[g],
                            preferred_element_type=F32)

    def softmax_block(g):
        for r in range(ATT_REP):
            h = g * ATT_REP + r
            cs = slice(r * qb, (r + 1) * qb)
            mx = jnp.full((SUBLANES, qb), M_FLOOR, F32)
            for sb in range(nsub):
                rs = slice(sb * ATT_SUB, (sb + 1) * ATT_SUB)
                lgs = lg_ref[g, rs, cs] - slopes[h] * dm_ref[rs, :]
                lg_ref[g, rs, cs] = lgs
                mx = jnp.maximum(mx, jnp.max(rows8(lgs), axis=0))
            m_old = m_ref[g, :, cs]
            m_new = jnp.maximum(m_old, all_rows(mx, jnp.maximum, M_FLOOR, 1))
            sm = jnp.zeros((SUBLANES, qb), F32)
            for sb in range(nsub):
                rs = slice(sb * ATT_SUB, (sb + 1) * ATT_SUB)
                pr = jnp.exp2(rows8(lg_ref[g, rs, cs]) - m_new[None])
                sm = sm + jnp.sum(pr, axis=0)
                pt_ref[g, rs, cs] = pr.reshape(ATT_SUB, qb).astype(BF16)
            alpha = jnp.exp2(m_old - m_new)
            l_ref[g, :, cs] = alpha * l_ref[g, :, cs] + all_rows_sum(sm, 2)
            m_ref[g, :, cs] = m_new
            al_ref[g, :, cs] = alpha

    def weighted_values(g, c):
        pv = jnp.dot(vt_ref[g * ATT_HD:(g + 1) * ATT_HD, rows_of(c)], pt_ref[g],
                     preferred_element_type=F32)
        acc_ref[g] = (rows8(acc_ref[g]) * al_ref[g][None]).reshape(ATT_HD, ATT_REP * qb) + pv

    logits(0, 0)

    def att_body(c, carry):
        key = keys_ref[rows_of(c), :]
        dist = jnp.abs(pos - key_index(c)).astype(F32)
        dm_ref[...] = jnp.where(key >= thr_eff, dist, MASK_DIST)
        logits(1, c)
        softmax_block(0)
        weighted_values(0, c)
        logits(0, jnp.minimum(c + 1, nkc - 1))
        softmax_block(1)
        weighted_values(1, c)
        return carry
    lax.fori_loop(0, nkc, att_body, 0)

    for h in range(ATT_HEADS):
        g, r = divmod(h, ATT_REP)
        cs = slice(r * qb, (r + 1) * qb)
        ot = (rows8(acc_ref[g, :, cs]) / l_ref[g, :, cs][None]).reshape(ATT_HD, qb)
        o_ref[:, h * ATT_HD:(h + 1) * ATT_HD] = ot.T.astype(BF16)


def _dsa(q_ops, k_all, vt_all, ki_all, *, nb, nqb, qb, past, l_real, kblk, out_rows=None):
    l_pad = k_all.shape[1]
    topk = min(IDX_TOPK, l_real // 4)
    assert kblk >= topk and kblk % ATT_SUB == 0 and l_pad % kblk == 0
    (qa, qa_map), (qi, qi_map), (wi, wi_map) = q_ops
    kern = functools.partial(_dsa_kernel, past=past, l_real=l_real, kblk=kblk, topk=topk)
    return pl.pallas_call(
        kern,
        grid=(nb, nqb),
        in_specs=[pl.BlockSpec((qb, ATT_W), qa_map), pl.BlockSpec((qb, IDX_HEADS * IDX_DIM), qi_map),
                  pl.BlockSpec((qb, LANES), wi_map),
                  pl.BlockSpec((None, l_pad, ATT_KV * ATT_HD), lambda b, i: (b, 0, 0)),
                  pl.BlockSpec((None, ATT_KV * ATT_HD, l_pad), lambda b, i: (b, 0, 0)),
                  pl.BlockSpec((None, l_pad, LANES), lambda b, i: (b, 0, 0))],
        out_specs=pl.BlockSpec((qb, ATT_W), lambda b, i: (b * nqb + i, 0)),
        out_shape=jax.ShapeDtypeStruct((out_rows or nb * nqb * qb, ATT_W), BF16),
        scratch_shapes=[pltpu.VMEM((l_pad, 2 * LANES), BF16),
                        pltpu.VMEM((l_pad, qb), I32),
                        pltpu.VMEM((l_pad, qb), I16),
                        pltpu.VMEM((l_pad, qb), I16),
                        pltpu.VMEM((2 * LANES, IDX_HEADS * qb), BF16),
                        pltpu.VMEM((ATT_KV, ATT_HD, ATT_REP * qb), BF16),
                        pltpu.VMEM((ATT_KV, SUBLANES, ATT_REP * qb), F32),
                        pltpu.VMEM((ATT_KV, SUBLANES, ATT_REP * qb), F32),
                        pltpu.VMEM((ATT_KV, ATT_HD, ATT_REP * qb), F32),
                        pltpu.VMEM((ATT_KV, kblk, ATT_REP * qb), F32),
                        pltpu.VMEM((ATT_KV, kblk, ATT_REP * qb), BF16),
                        pltpu.VMEM((kblk, qb), F32),
                        pltpu.VMEM((ATT_KV, SUBLANES, ATT_REP * qb), F32),
                        pltpu.VMEM((3, SUBLANES, qb), F32)],
        compiler_params=_cparams(("parallel", "arbitrary")),
        name="dsa",
    )(qa, qi, wi, k_all, vt_all, ki_all)


def _rw_prep_kernel(r_ref, k_ref, v_ref, lr_ref, hr_ref, hk_ref, hv_ref, hlr_ref,
                    sr_ref, sk_ref, sv_ref, slr_ref, mr_ref, mk_ref, mv_ref, mlr_ref,
                    w0_ref, a0_ref, w2_ref, a2_ref, g2_ref,
                    ro_ref, ko_ref, vo_ref, wo_ref, ao_ref, go_ref, *, tiles_per_seq):
    first_tile = (pl.program_id(0) % tiles_per_seq) == 0

    def mix(x_ref, halo_ref, shift_ref, mu_ref):
        x = x_ref[...]
        first = jnp.where(first_tile, shift_ref[...], halo_ref[SUBLANES - 1:SUBLANES, :])
        rows = lax.broadcasted_iota(I32, x.shape, 0)
        prev = jnp.where(rows == 0, first, pltpu.roll(x, 1, 0))
        return x + (prev - x) * mu_ref[...]

    ro_ref[...] = mix(r_ref, hr_ref, sr_ref, mr_ref)
    ko_ref[...] = mix(k_ref, hk_ref, sk_ref, mk_ref)
    vo_ref[...] = mix(v_ref, hv_ref, sv_ref, mv_ref)
    lr = mix(lr_ref, hlr_ref, slr_ref, mlr_ref)
    xw = lr[:, 0:128]
    xa = lr[:, 128:256]
    xg = lr[:, 256:512]
    wpre = w0_ref[...] + _dot3(jnp.tanh(xw), w2_ref[...])
    wo_ref[...] = jnp.exp(-RW_DECAY_SCALE * _sigmoid(wpre))
    ao_ref[...] = _sigmoid(a0_ref[...] + _dot3(xa, a2_ref[...]))
    go_ref[...] = _dot3(_sigmoid(xg), g2_ref[...])


def _rw_prep(proj, row0, nb, seq, shifts, mus, w0, a0, w2p, a2p, g2, tm, out_rows=None):
    m = nb * seq
    tps = seq // tm
    rb0 = row0 // tm
    hb = tm // SUBLANES
    cr, ck, cv = _OFF["r"] // RW_W, _OFF["k"] // RW_W, _OFF["v"] // RW_W
    clr = _OFF["xw"] // 512

    def main(c):
        return lambda i: (rb0 + i, c)

    def halo(c):
        return lambda i: (jnp.maximum((rb0 + i) * hb - 1, 0), c)

    bidx = lambda i: (i // tps, 0, 0)
    fix = lambda i: (0, 0)
    wide = [pl.BlockSpec((tm, RW_W), main(cr)), pl.BlockSpec((tm, RW_W), main(ck)),
            pl.BlockSpec((tm, RW_W), main(cv)), pl.BlockSpec((tm, 512), main(clr)),
            pl.BlockSpec((SUBLANES, RW_W), halo(cr)), pl.BlockSpec((SUBLANES, RW_W), halo(ck)),
            pl.BlockSpec((SUBLANES, RW_W), halo(cv)), pl.BlockSpec((SUBLANES, 512), halo(clr))]
    sh = [pl.BlockSpec((None, 1, RW_W), bidx)] * 3 + [pl.BlockSpec((None, 1, 512), bidx)]
    mu = [pl.BlockSpec((1, RW_W), fix)] * 3 + [pl.BlockSpec((1, 512), fix)]
    par = [pl.BlockSpec((1, RW_W), fix), pl.BlockSpec((1, RW_W), fix),
           pl.BlockSpec((128, RW_W), fix), pl.BlockSpec((128, RW_W), fix),
           pl.BlockSpec((RW_G_R, RW_W), fix)]
    out = jax.ShapeDtypeStruct((m, RW_W), F32)
    return pl.pallas_call(
        functools.partial(_rw_prep_kernel, tiles_per_seq=tps),
        grid=(m // tm,),
        in_specs=wide + sh + mu + par,
        out_specs=[pl.BlockSpec((tm, RW_W), lambda i: (i, 0))] * 6,
        out_shape=[out] * 5 + [jax.ShapeDtypeStruct((out_rows or m, RW_W), F32)],
        compiler_params=_cparams(("parallel",)),
        name="rwkv_prep",
    )(proj, proj, proj, proj, proj, proj, proj, proj, *shifts, *mus, w0, a0, w2p, a2p, g2)


def _rw_scan_kernel(r_ref, k_ref, v_ref, w_ref, a_ref, kkp_ref, kap_ref, rkp_ref,
                    lng_ref, lnb_ref, s0_ref, o_ref, st_ref, s_scr, tmp_scr):
    tb = pl.program_id(1)
    nsteps = r_ref.shape[0]

    @pl.when(tb == 0)
    def _():
        s_scr[...] = s0_ref[...]

    def step(t, carry):
        k = k_ref[t]
        a = a_ref[t]
        kk = k * kkp_ref[...]
        nrm = jnp.sqrt(jnp.sum(kk * kk, axis=0, keepdims=True))
        kk = kk / jnp.maximum(nrm, 1e-12)
        km = k * (1.0 + (a - 1.0) * kap_ref[...])
        tmp_scr[0] = kk
        tmp_scr[1] = kk * a
        tmp_scr[2] = km
        parts = [jnp.zeros((RW_HD, LANES), F32) for _ in range(2)]
        for kx in range(RW_HD):
            parts[kx % 2] = parts[kx % 2] + s_scr[kx] * tmp_scr[0, kx:kx + 1, :]
        sa = -(parts[0] + parts[1])
        v = v_ref[t]
        outs = [jnp.zeros((RW_HD, LANES), F32) for _ in range(2)]
        for kx in range(RW_HD):
            s_new = (s_scr[kx] * w_ref[t, kx:kx + 1, :] + sa * tmp_scr[1, kx:kx + 1, :]
                     + v * tmp_scr[2, kx:kx + 1, :])
            s_scr[kx] = s_new
            outs[kx % 2] = outs[kx % 2] + s_new * r_ref[t, kx:kx + 1, :]
        o = outs[0] + outs[1]
        mu = jnp.mean(o, axis=0, keepdims=True)
        oc = o - mu
        var = jnp.mean(oc * oc, axis=0, keepdims=True)
        on = oc * lax.rsqrt(var + RW_GN_EPS) * lng_ref[...] + lnb_ref[...]
        bonus = jnp.sum(r_ref[t] * tmp_scr[2] * rkp_ref[...], axis=0, keepdims=True) * v
        o_ref[t] = on + bonus
        return carry
    lax.fori_loop(0, nsteps, step, 0)

    @pl.when(tb == pl.num_programs(1) - 1)
    def _():
        st_ref[...] = s_scr[...]


def _rw_scan(r, k, v, w, a, params, s0, tb):
    seq, _, pairs = r.shape
    assert pairs % LANES == 0, "batch * RW_HEADS must fill whole lane groups"
    seqb = pl.BlockSpec((tb, RW_HD, LANES), lambda g, t: (t, 0, g))
    par = pl.BlockSpec((RW_HD, LANES), lambda g, t: (0, 0))
    st = pl.BlockSpec((RW_HD, RW_HD, LANES), lambda g, t: (0, 0, g))
    return pl.pallas_call(
        _rw_scan_kernel,
        grid=(pairs // LANES, seq // tb),
        in_specs=[seqb] * 5 + [par] * 5 + [st],
        out_specs=[seqb, st],
        out_shape=[jax.ShapeDtypeStruct((seq, RW_HD, pairs), F32),
                   jax.ShapeDtypeStruct((RW_HD, RW_HD, pairs), F32)],
        scratch_shapes=[pltpu.VMEM((RW_HD, RW_HD, LANES), F32),
                        pltpu.VMEM((3, RW_HD, LANES), F32)],
        compiler_params=_cparams(("parallel", "arbitrary")),
        name="rwkv_scan",
    )(r, k, v, w, a, *params, s0)


GLA_SUB = 16


def _gla_kernel(q_ref, k_ref, v_ref, gl_ref, og_ref, w2_ref, gb_ref, ng_ref, s0_ref,
                o_ref, st_ref, s_scr, la_scr, att_scr):
    tblk = pl.program_id(1)
    tc = q_ref.shape[0]

    @pl.when(tblk == 0)
    def _():
        for h in range(GLA_HEADS):
            s_scr[h] = s0_ref[h].T

    la_scr[...] = _log_sigmoid(_dot3(gl_ref[...], w2_ref[...]) + gb_ref[...]) * (1.0 / GLA_GATE_NORM)
    rows = lax.broadcasted_iota(I32, (CHUNK, GLA_DK), 0)
    lane = lax.broadcasted_iota(I32, (GLA_SUB, CHUNK), 1)
    jrow = lax.broadcasted_iota(I32, (GLA_SUB, GLA_DK), 0)
    nsub = CHUNK // GLA_SUB

    def head_chunk(h, rs):
        ks = slice(h * GLA_DK, (h + 1) * GLA_DK)
        vs = slice(h * GLA_DV, (h + 1) * GLA_DV)
        b = la_scr[rs, ks]
        sh = 1
        while sh < CHUNK:
            b = b + jnp.where(rows >= sh, pltpu.roll(b, sh, 0), 0.0)
            sh *= 2
        q = q_ref[rs, ks] * (GLA_DK ** -0.5)
        k = k_ref[rs, ks]
        v = v_ref[rs, vs]
        s_old = s_scr[h]
        inter = _nt_dot(q * jnp.exp(b), s_old)
        b_last = b[CHUNK - 1:CHUNK, :]

        for jb in range(nsub):
            js = slice(jb * GLA_SUB, (jb + 1) * GLA_SUB)
            bj = b[js]
            kj = k[js]
            blk = jnp.zeros((GLA_SUB, CHUNK), F32)
            for ii in range(GLA_SUB):
                i_row = jb * GLA_SUB + ii
                e = jnp.exp(jnp.minimum(b[i_row:i_row + 1, :] - bj, 0.0))
                val = jnp.where(jrow <= ii, q[i_row:i_row + 1, :] * kj * e, 0.0)
                col = jnp.sum(val, axis=1, keepdims=True)
                blk = jnp.where(lane == i_row, col, blk)
            if jb + 1 < nsub:
                i0 = (jb + 1) * GLA_SUB
                be = b[i0 - 1:i0, :]
                kt = kj * jnp.exp(be - bj)
                later = rows >= i0
                qt = jnp.where(later, q * jnp.exp(jnp.where(later, b - be, 0.0)), 0.0)
                blk = blk + _nt_dot(kt, qt)
            att_scr[h, js, :] = blk
        o = inter + _tn_dot(att_scr[h], v)
        khat = k * jnp.exp(b_last - b)
        s_scr[h] = s_old * jnp.exp(b_last) + _tn_dot(v, khat)
        o = o * lax.rsqrt(jnp.mean(o * o, -1, keepdims=True) + GLA_NORM_EPS) * ng_ref[:, vs]
        og = og_ref[rs, vs]
        o_ref[rs, vs] = (o * (og * _sigmoid(og))).astype(BF16)

    def chunk_body(c, carry):
        rs = pl.ds(pl.multiple_of(c * CHUNK, CHUNK), CHUNK)
        for h in range(GLA_HEADS):
            head_chunk(h, rs)
        return carry
    lax.fori_loop(0, tc // CHUNK, chunk_body, 0)

    @pl.when(tblk == pl.num_programs(1) - 1)
    def _():
        for h in range(GLA_HEADS):
            st_ref[h] = s_scr[h].T


def _gla(proj, tail, row0, nb, seq, w2p, gb, ng, s0, tc, out_rows=None):
    ntb = seq // tc
    rb0 = row0 // tc
    cq, ck = _OFF["qc"] // GLA_KW, _OFF["kc"] // GLA_KW
    cv, cg = _OFF["vc"] // GLA_VW, _OFF["og"] // GLA_VW
    cgl = (_OFF["gl"] - _OFF["ka"]) // LANES

    def rows(c0):
        return lambda b, t: (rb0 + b * ntb + t, c0)

    fix = lambda b, t: (0, 0)
    st = pl.BlockSpec((None, GLA_HEADS, GLA_DK, GLA_DV), lambda b, t: (b, 0, 0, 0))
    return pl.pallas_call(
        _gla_kernel,
        grid=(nb, ntb),
        in_specs=[pl.BlockSpec((tc, GLA_KW), rows(cq)), pl.BlockSpec((tc, GLA_KW), rows(ck)),
                  pl.BlockSpec((tc, GLA_VW), rows(cv)), pl.BlockSpec((tc, LANES), rows(cgl)),
                  pl.BlockSpec((tc, GLA_VW), rows(cg)),
                  pl.BlockSpec((LANES, GLA_KW), fix), pl.BlockSpec((1, GLA_KW), fix),
                  pl.BlockSpec((1, GLA_VW), fix), st],
        out_specs=[pl.BlockSpec((tc, GLA_VW), lambda b, t: (b * ntb + t, 0)), st],
        out_shape=[jax.ShapeDtypeStruct((out_rows or nb * seq, GLA_VW), BF16),
                   jax.ShapeDtypeStruct((nb, GLA_HEADS, GLA_DK, GLA_DV), F32)],
        scratch_shapes=[pltpu.VMEM((GLA_HEADS, GLA_DV, GLA_DK), F32), pltpu.VMEM((tc, GLA_KW), F32),
                        pltpu.VMEM((GLA_HEADS, CHUNK, CHUNK), F32)],
        compiler_params=_cparams(("parallel", "arbitrary")),
        name="gla",
    )(proj, proj, proj, tail, proj, w2p, gb, ng, s0)


def _pack_cols(w):
    parts = []
    for _, s, wd, p in _SEGS:
        parts.append(w[..., s:s + wd])
        if p > wd:
            parts.append(jnp.zeros(w.shape[:-1] + (p - wd,), w.dtype))
    return jnp.concatenate(parts, axis=-1)


def _pack_rw(vec):
    r, k, v = vec[..., 0:1024], vec[..., 1024:2048], vec[..., 2048:3072]
    z = jnp.zeros(vec.shape[:-1] + (32,), vec.dtype)
    lr = jnp.concatenate([vec[..., 3072:3168], z, vec[..., 3168:3264], z, vec[..., 3264:3520]], -1)
    return r, k, v, lr


def _pad_rows(w, n):
    return jnp.concatenate([w, jnp.zeros((n - w.shape[0],) + w.shape[1:], w.dtype)], 0)


RELAYOUT_T = 128


def _to_scan_kernel(x_ref, o_ref, y_scr):
    nbat = x_ref.shape[0]
    for b in range(nbat):
        for c in range(RW_W // LANES):
            y_scr[b, c * LANES:(c + 1) * LANES, :] = x_ref[b, :, c * LANES:(c + 1) * LANES].T
    for n in range(RW_HD):
        z = jnp.concatenate([y_scr[b, pl.ds(n, RW_HEADS, stride=RW_HD), :] for b in range(nbat)], axis=0)
        o_ref[:, n, :] = z.T


def _from_scan_kernel(o_ref, x_ref, y_scr):
    nbat = x_ref.shape[0]
    for n in range(RW_HD):
        zt = o_ref[:, n, :].T
        for b in range(nbat):
            y_scr[b, pl.ds(n, RW_HEADS, stride=RW_HD), :] = zt[b * RW_HEADS:(b + 1) * RW_HEADS]
    for b in range(nbat):
        for c in range(RW_W // LANES):
            x_ref[b, :, c * LANES:(c + 1) * LANES] = y_scr[b, c * LANES:(c + 1) * LANES, :].T


def _relayout_ok(nb, seq):
    return nb * RW_HEADS == LANES and seq % RELAYOUT_T == 0


def _to_scan(x, nb, seq):
    if not _relayout_ok(nb, seq):
        return jnp.transpose(x.reshape(nb, seq, RW_HEADS, RW_HD), (1, 3, 0, 2)).reshape(seq, RW_HD, nb * RW_HEADS)
    return pl.pallas_call(
        _to_scan_kernel,
        grid=(seq // RELAYOUT_T,),
        in_specs=[pl.BlockSpec((nb, RELAYOUT_T, RW_W), lambda t: (0, t, 0))],
        out_specs=pl.BlockSpec((RELAYOUT_T, RW_HD, LANES), lambda t: (t, 0, 0)),
        out_shape=jax.ShapeDtypeStruct((seq, RW_HD, LANES), F32),
        scratch_shapes=[pltpu.VMEM((nb, RW_W, RELAYOUT_T), F32)],
        compiler_params=_cparams(("parallel",)),
        name="to_scan_layout",
    )(x.reshape(nb, seq, RW_W))


def _from_scan(o, nb, seq):
    if not _relayout_ok(nb, seq):
        return jnp.transpose(o.reshape(seq, RW_HD, nb, RW_HEADS), (2, 0, 3, 1)).reshape(nb * seq, RW_W)
    return pl.pallas_call(
        _from_scan_kernel,
        grid=(seq // RELAYOUT_T,),
        in_specs=[pl.BlockSpec((RELAYOUT_T, RW_HD, LANES), lambda t: (t, 0, 0))],
        out_specs=pl.BlockSpec((nb, RELAYOUT_T, RW_W), lambda t: (0, t, 0)),
        out_shape=jax.ShapeDtypeStruct((nb, seq, RW_W), F32),
        scratch_shapes=[pltpu.VMEM((nb, RW_W, RELAYOUT_T), F32)],
        compiler_params=_cparams(("parallel",)),
        name="from_scan_layout",
    )(o).reshape(nb * seq, RW_W)


def _group_mixers(proj, tail, row0, nb, seq, caches, lw, out_rows=None):
    m = nb * seq
    cache_k, cache_v, cache_ki, s_rw, s_shift, s_gla = caches
    past = 0 if cache_k is None else cache_k.shape[1]
    l_real = past + seq
    t0 = _OFF["ka"]
    grp = tail[row0:row0 + m].reshape(nb, seq, -1)

    ka = grp[..., _OFF["ka"] - t0:_OFF["ka"] - t0 + 256]
    va = grp[..., _OFF["va"] - t0:_OFF["va"] - t0 + 256]
    ki = grp[..., _OFF["ki"] - t0:_OFF["ki"] - t0 + LANES]
    if past:
        k_all = jnp.concatenate([cache_k.reshape(nb, past, 256), ka], 1)
        v_all = jnp.concatenate([cache_v.reshape(nb, past, 256), va], 1)
        cki = jnp.concatenate([cache_ki, jnp.zeros((nb, past, LANES - IDX_DIM), F32)], -1)
        ki_all = jnp.concatenate([cki, ki], 1)
    else:
        k_all, v_all, ki_all = ka, va, ki
    kblk = 512 if l_real % 512 == 0 else 384
    l_pad = -(-l_real // kblk) * kblk
    if l_pad > l_real:
        padk = lambda t: jnp.concatenate([t, jnp.zeros((nb, l_pad - l_real, t.shape[-1]), t.dtype)], 1)
        k_all, v_all, ki_all = padk(k_all), padk(v_all), padk(ki_all)
    k_bf = k_all.astype(BF16)
    vt_bf = jnp.swapaxes(v_all, 1, 2).astype(BF16)
    qb = DSA_QB if seq % DSA_QB == 0 else LANES
    if seq % qb == 0:
        nqb = seq // qb
        rb0 = row0 // qb
        mk = lambda c: (lambda b, i: (rb0 + b * nqb + i, c))
        q_ops = ((proj, mk(_OFF["qa"] // ATT_W)), (proj, mk(_OFF["qi"] // 1024)),
                 (tail, mk((_OFF["wi"] - t0) // LANES)))
        o_a = _dsa(q_ops, k_bf, vt_bf, ki_all, nb=nb, nqb=nqb, qb=qb, past=past, l_real=l_real, kblk=kblk,
                   out_rows=out_rows)
    else:
        def padq(t):
            return jnp.concatenate([t, jnp.zeros((nb, qb - seq, t.shape[-1]), F32)], 1).reshape(nb * qb, -1)
        qrows = proj[row0:row0 + m, 0:_OFF["vc"]].reshape(nb, seq, -1)
        mk = lambda b, i: (b, 0)
        q_ops = ((padq(qrows[..., 0:ATT_W]), mk), (padq(qrows[..., ATT_W:]), mk),
                 (padq(grp[..., _OFF["wi"] - t0:_OFF["wi"] - t0 + LANES]), mk))
        o_a = _dsa(q_ops, k_bf, vt_bf, ki_all, nb=nb, nqb=1, qb=qb, past=past, l_real=l_real, kblk=kblk)
        o_a = o_a.reshape(nb, qb, ATT_W)[:, :seq].reshape(m, ATT_W)

    if s_shift is None:
        s_shift = jnp.zeros((nb, RW_PROJ), F32)
    shifts = [t.reshape(nb, 1, -1) for t in _pack_rw(s_shift)]
    tm = 256 if seq % 256 == 0 else seq
    r, k, v, w, a, g = _rw_prep(proj, row0, nb, seq, shifts, lw["rw_mus"], lw["rw_w0"], lw["rw_a0"],
                                lw["rw_w2p"], lw["rw_a2p"], lw["rw_g2"], tm, out_rows)
    if s_rw is None:
        s0 = jnp.zeros((RW_HD, RW_HD, nb * RW_HEADS), F32)
    else:
        s0 = jnp.transpose(s_rw, (3, 2, 0, 1)).reshape(RW_HD, RW_HD, nb * RW_HEADS)
    tile = lambda p: jnp.tile(p.reshape(RW_HEADS, RW_HD).T, (1, LANES // RW_HEADS))
    params = [tile(lw[n]) for n in ("rw_kk", "rw_ka", "rw_rk", "rw_ln_g", "rw_ln_b")]
    tb = 32 if seq % 32 == 0 else seq
    o_b, s_t = _rw_scan(*[_to_scan(t, nb, seq) for t in (r, k, v, w, a)], params, s0, tb)
    o_b = _from_scan(o_b, nb, seq)
    s_rw_new = jnp.transpose(s_t.reshape(RW_HD, RW_HD, nb, RW_HEADS), (2, 3, 1, 0))
    last = jnp.take(proj, row0 + seq - 1 + seq * jnp.arange(nb), axis=0)[:, _OFF["r"]:_OFF["qc"]]
    base = _OFF["r"]
    shift_new = jnp.concatenate(
        [last[:, 0:3072], last[:, _OFF["xw"] - base:_OFF["xw"] - base + 96],
         last[:, _OFF["xa"] - base:_OFF["xa"] - base + 96], last[:, _OFF["xg"] - base:]], -1)

    if s_gla is None:
        s_gla = jnp.zeros((nb, GLA_HEADS, GLA_DK, GLA_DV), F32)
    tc = 512 if seq % 512 == 0 else seq
    o_c, s_gla_new = _gla(proj, tail, row0, nb, seq, lw["gla_w2p"], lw["gla_b"], lw["gla_ng"], s_gla, tc,
                          out_rows)

    new_state = (ka.reshape(nb, seq, ATT_KV, ATT_HD), va.reshape(nb, seq, ATT_KV, ATT_HD),
                 ki[..., :IDX_DIM], s_rw_new, shift_new, s_gla_new)
    return o_a, o_b, g, o_c, new_state


def kernel(x_prompt, x_sample, cache_k, cache_v, cache_kidx, state_rwkv, state_rwkv_shift, state_gla,
           w_in, rw_mu, rw_w0, rw_w2, rw_a0, rw_a2, rw_g2, rw_kk, rw_ka, rw_rk, rw_ln_g, rw_ln_b,
           gla_w2, gla_b, gla_norm_g, w_branch_a, w_branch_b, w_branch_c, w_out,
           ln1_g, ln1_b, ln2_g, ln2_b, ffn_w1, ffn_w3, ffn_w2):
    bp, tp, _ = x_prompt.shape
    bs, ts, _ = x_sample.shape
    mp, ms = bp * tp, bs * ts
    x = jnp.concatenate([x_prompt.reshape(mp, D_MODEL), x_sample.reshape(ms, D_MODEL)], 0)
    xb = x.astype(BF16)
    m = mp + ms
    tm_big = 1024 if m % 1024 == 0 else 256
    tm_mid = 512 if m % 512 == 0 else 256

    w_in_p = _pack_cols(w_in).astype(BF16)
    new_p = [[] for _ in range(6)]
    new_s = [[] for _ in range(6)]
    for l in range(DEPTH):
        lw = {
            "rw_mus": [t.reshape(1, -1) for t in _pack_rw(rw_mu[l])],
            "rw_w0": rw_w0[l].reshape(1, RW_W), "rw_a0": rw_a0[l].reshape(1, RW_W),
            "rw_w2p": _pad_rows(rw_w2[l], 128), "rw_a2p": _pad_rows(rw_a2[l], 128), "rw_g2": rw_g2[l],
            "rw_kk": rw_kk[l], "rw_ka": rw_ka[l], "rw_rk": rw_rk[l],
            "rw_ln_g": rw_ln_g[l], "rw_ln_b": rw_ln_b[l],
            "gla_w2p": _pad_rows(gla_w2[l], LANES), "gla_b": gla_b[l].reshape(1, GLA_KW),
            "gla_ng": gla_norm_g[l].reshape(1, GLA_VW),
        }
        proj = _matmul(xb, w_in_p[l, :, :_OFF["ka"]], tm_big, 1024)
        tail = _matmul(xb, w_in_p[l, :, _OFF["ka"]:], tm_big, PROJ_W - _OFF["ka"])
        oa_p, ob_p, g_p, oc_p, st_p = _group_mixers(proj, tail, 0, bp, tp, (None,) * 6, lw, out_rows=m)
        oa_s, ob_s, g_s, oc_s, st_s = _group_mixers(
            proj, tail, mp, bs, ts,
            (cache_k[l], cache_v[l], cache_kidx[l], state_rwkv[l], state_rwkv_shift[l], state_gla[l]), lw)
        put = lambda full, part: lax.dynamic_update_slice(full, part, (mp, 0))
        merged = _merge(put(oa_p, oa_s), jnp.concatenate([ob_p, ob_s], 0), put(g_p, g_s), put(oc_p, oc_s), proj,
                        w_branch_a[l].astype(BF16), w_branch_b[l].astype(BF16),
                        w_branch_c[l].astype(BF16), 256, D_MODEL)
        h, hb = _hproj(merged, w_out[l].astype(BF16), x, ln1_g[l].reshape(1, -1),
                       ln1_b[l].reshape(1, -1), tm_mid)
        x, xb = _ffn(hb, h, ffn_w1[l].astype(BF16), ffn_w3[l].astype(BF16), ffn_w2[l].astype(BF16),
                     ln2_g[l].reshape(1, -1), ln2_b[l].reshape(1, -1), tm_mid, 512)
        for i in range(6):
            new_p[i].append(st_p[i])
            new_s[i].append(st_s[i])
    sp = [jnp.stack(a) for a in new_p]
    ss = [jnp.stack(a) for a in new_s]
    yp = x[:mp].reshape(bp, tp, D_MODEL)
    ys = x[mp:].reshape(bs, ts, D_MODEL)
    return (yp, ys, sp[0], sp[1], sp[2], sp[3], sp[4], sp[5], ss[0], ss[1], ss[2], ss[3], ss[4], ss[5])
```

```python
import functools

import jax
import jax.numpy as jnp
from jax import lax
from jax.experimental import pallas as pl
from jax.experimental.pallas import tpu as pltpu

F32 = jnp.float32
BF16 = jnp.bfloat16
I32 = jnp.int32
I16 = jnp.int16

D_MODEL = 2048
DEPTH = 4
CHUNK = 64
Q_BLOCK = 128
ATT_HEADS = 8
ATT_HD = 128
ATT_KV = 2
ATT_REP = ATT_HEADS // ATT_KV
ATT_W = ATT_HEADS * ATT_HD
IDX_HEADS = 16
IDX_DIM = 64
IDX_TOPK = 256
IDX_SCALE = (IDX_DIM * IDX_HEADS) ** -0.5
RW_HEADS = 16
RW_HD = 64
RW_W = RW_HEADS * RW_HD
RW_DECAY_R = 96
RW_A_R = 96
RW_G_R = 256
RW_PROJ = 3 * RW_W + RW_DECAY_R + RW_A_R + RW_G_R
RW_DECAY_SCALE = 0.606531
RW_GN_EPS = 64e-5
GLA_HEADS = 4
GLA_DK = 128
GLA_DV = 256
GLA_KW = GLA_HEADS * GLA_DK
GLA_VW = GLA_HEADS * GLA_DV
GLA_GATE_R = 16
GLA_GATE_NORM = 16.0
GLA_NORM_EPS = 1e-5
FFN_HIDDEN = -(-8 * D_MODEL // (3 * 256)) * 256
DN_ALPHA = (2 * DEPTH) ** 0.25
LN_EPS = 1e-5
LOG2E = 1.4426950408889634

LANES = 128
SUBLANES = 8
PACKED_ROWS = 16
VMEM_LIMIT = 56 * 1024 * 1024

_SEGS = (
    ("qa", 0, 1024, 1024), ("qi", 1536, 1024, 1024), ("vc", 7184, 1024, 1024),
    ("og", 8224, 1024, 1024), ("gates", 9248, 6144, 6144),
    ("r", 2640, 1024, 1024), ("k", 3664, 1024, 1024), ("v", 4688, 1024, 1024),
    ("xw", 5712, 96, 128), ("xa", 5808, 96, 128), ("xg", 5904, 256, 256),
    ("qc", 6160, 512, 512), ("kc", 6672, 512, 512),
    ("ka", 1024, 256, 256), ("va", 1280, 256, 256),
    ("ki", 2560, 64, 128), ("wi", 2624, 16, 128), ("gl", 8208, 16, 128),
)
_OFF = {}
_o = 0
for _n, _s, _w, _p in _SEGS:
    _OFF[_n] = _o
    _o += _p
PROJ_W = _o
KEY_NEG_INF = -0x7F800000
MASK_DIST = 1e32
M_FLOOR = -1e29


def _cparams(sem):
    return pltpu.CompilerParams(dimension_semantics=sem, vmem_limit_bytes=VMEM_LIMIT)


def _split2(x):
    hi = x.astype(BF16)
    lo = (x - hi.astype(F32)).astype(BF16)
    return hi, lo


def _dot3(a, b):
    ah, al = _split2(a)
    bh, bl = _split2(b)
    d = functools.partial(jnp.dot, preferred_element_type=F32)
    return d(ah, bh) + (d(al, bh) + d(ah, bl))


def _nt_dot(a, b):
    return lax.dot_general(a.astype(BF16), b.astype(BF16), (((1,), (1,)), ((), ())),
                           preferred_element_type=F32)


def _tn_dot(a, b):
    return lax.dot_general(a.astype(BF16), b.astype(BF16), (((0,), (0,)), ((), ())),
                           preferred_element_type=F32)


def _sigmoid(x):
    return 1.0 / (1.0 + jnp.exp(-x))


def _log_sigmoid(x):
    return jnp.minimum(x, 0.0) - jnp.log(1.0 + jnp.exp(-jnp.abs(x)))


def _layer_norm(x, g, b):
    mu = jnp.mean(x, -1, keepdims=True)
    xc = x - mu
    var = jnp.mean(xc * xc, -1, keepdims=True)
    return xc * lax.rsqrt(var + LN_EPS) * g + b


def _mm_kernel(a_ref, b_ref, o_ref):
    o_ref[...] = jnp.dot(a_ref[...], b_ref[...], preferred_element_type=F32)


def _matmul(a, b, tm, tn):
    m, k = a.shape
    n = b.shape[1]
    return pl.pallas_call(
        _mm_kernel,
        grid=(m // tm, pl.cdiv(n, tn)),
        in_specs=[pl.BlockSpec((tm, k), lambda i, j: (i, 0)),
                  pl.BlockSpec((k, tn), lambda i, j: (0, j))],
        out_specs=pl.BlockSpec((tm, tn), lambda i, j: (i, j)),
        out_shape=jax.ShapeDtypeStruct((m, n), F32),
        compiler_params=_cparams(("parallel", "arbitrary")),
        name="in_proj",
    )(a, b)


def _merge_kernel(oa_ref, ob_ref, g_ref, oc_ref, ga_ref, gb_ref, gc_ref,
                  wa_ref, wb_ref, wc_ref, o_ref):
    ob = (ob_ref[...] * g_ref[...]).astype(BF16)
    d = functools.partial(jnp.dot, preferred_element_type=F32)
    acc = _sigmoid(ga_ref[...]) * d(oa_ref[...], wa_ref[...])
    acc += _sigmoid(gb_ref[...]) * d(ob, wb_ref[...])
    acc += _sigmoid(gc_ref[...]) * d(oc_ref[...], wc_ref[...])
    o_ref[...] = acc.astype(BF16)


def _merge(oa, ob, g, oc, proj, wa, wb, wc, tm, tn):
    m = oa.shape[0]
    gblk = _OFF["gates"] // tn
    nj = D_MODEL // tn
    row = lambda i, j: (i, 0)
    wspec = pl.BlockSpec((ATT_W, tn), lambda i, j: (0, j))
    return pl.pallas_call(
        _merge_kernel,
        grid=(m // tm, nj),
        in_specs=[pl.BlockSpec((tm, ATT_W), row), pl.BlockSpec((tm, RW_W), row),
                  pl.BlockSpec((tm, RW_W), row), pl.BlockSpec((tm, GLA_VW), row),
                  pl.BlockSpec((tm, tn), lambda i, j: (i, gblk + j)),
                  pl.BlockSpec((tm, tn), lambda i, j: (i, gblk + nj + j)),
                  pl.BlockSpec((tm, tn), lambda i, j: (i, gblk + 2 * nj + j)),
                  wspec, wspec, wspec],
        out_specs=pl.BlockSpec((tm, tn), lambda i, j: (i, j)),
        out_shape=jax.ShapeDtypeStruct((m, D_MODEL), BF16),
        compiler_params=_cparams(("parallel", "arbitrary")),
        name="merge",
    )(oa, ob, g, oc, proj, proj, proj, wa, wb, wc)


def _hproj_kernel(m_ref, w_ref, x_ref, g_ref, b_ref, h_ref, hb_ref):
    z = DN_ALPHA * x_ref[...] + jnp.dot(m_ref[...], w_ref[...], preferred_element_type=F32)
    h = _layer_norm(z, g_ref[...], b_ref[...])
    h_ref[...] = h
    hb_ref[...] = h.astype(BF16)


def _hproj(merged, w_out, x, g, b, tm):
    m = x.shape[0]
    row = lambda i: (i, 0)
    fix = lambda i: (0, 0)
    return pl.pallas_call(
        _hproj_kernel,
        grid=(m // tm,),
        in_specs=[pl.BlockSpec((tm, D_MODEL), row), pl.BlockSpec((D_MODEL, D_MODEL), fix),
                  pl.BlockSpec((tm, D_MODEL), row), pl.BlockSpec((1, D_MODEL), fix),
                  pl.BlockSpec((1, D_MODEL), fix)],
        out_specs=[pl.BlockSpec((tm, D_MODEL), row), pl.BlockSpec((tm, D_MODEL), row)],
        out_shape=[jax.ShapeDtypeStruct((m, D_MODEL), F32),
                   jax.ShapeDtypeStruct((m, D_MODEL), BF16)],
        compiler_params=_cparams(("parallel",)),
        name="out_proj_ln",
    )(merged, w_out, x, g, b)


def _ffn_kernel(hb_ref, h_ref, w1_ref, w3_ref, w2_ref, g_ref, b_ref, y_ref, yb_ref, acc_ref):
    j = pl.program_id(1)

    @pl.when(j == 0)
    def _():
        acc_ref[...] = jnp.zeros_like(acc_ref)

    hb = hb_ref[...]
    u1 = jnp.dot(hb, w1_ref[...], preferred_element_type=F32)
    u3 = jnp.dot(hb, w3_ref[...], preferred_element_type=F32)
    u = (u1 * _sigmoid(u1) * u3).astype(BF16)
    acc_ref[...] += jnp.dot(u, w2_ref[...], preferred_element_type=F32)

    @pl.when(j == pl.num_programs(1) - 1)
    def _():
        y = _layer_norm(DN_ALPHA * h_ref[...] + acc_ref[...], g_ref[...], b_ref[...])
        y_ref[...] = y
        yb_ref[...] = y.astype(BF16)


def _ffn(hb, h, w1, w3, w2, g, b, tm, th):
    m = h.shape[0]
    row = lambda i, j: (i, 0)
    fix = lambda i, j: (0, 0)
    return pl.pallas_call(
        _ffn_kernel,
        grid=(m // tm, FFN_HIDDEN // th),
        in_specs=[pl.BlockSpec((tm, D_MODEL), row), pl.BlockSpec((tm, D_MODEL), row),
                  pl.BlockSpec((D_MODEL, th), lambda i, j: (0, j)),
                  pl.BlockSpec((D_MODEL, th), lambda i, j: (0, j)),
                  pl.BlockSpec((th, D_MODEL), lambda i, j: (j, 0)),
                  pl.BlockSpec((1, D_MODEL), fix), pl.BlockSpec((1, D_MODEL), fix)],
        out_specs=[pl.BlockSpec((tm, D_MODEL), row), pl.BlockSpec((tm, D_MODEL), row)],
        out_shape=[jax.ShapeDtypeStruct((m, D_MODEL), F32),
                   jax.ShapeDtypeStruct((m, D_MODEL), BF16)],
        scratch_shapes=[pltpu.VMEM((tm, D_MODEL), F32)],
        compiler_params=_cparams(("parallel", "arbitrary")),
        name="ffn_ln",
    )(hb, h, w1, w3, w2, g, b)


ATT_SUB = 128
DSA_QB = 256


def _dsa_kernel(qa_ref, qi_ref, wi_ref, k_ref, vt_ref, ki_ref, o_ref,
                kcat_ref, keys_ref, hi_ref, lo_ref, wq_ref, qt_ref, m_ref, l_ref, acc_ref,
                lg_ref, pt_ref, dm_ref, al_ref, red_ref, *, past, l_real, kblk, topk):
    qb = qa_ref.shape[0]
    l_pad = ki_ref.shape[0]
    i = pl.program_id(1)
    kf = float(topk)

    @pl.when(i == 0)
    def _():
        def prep(c, carry):
            r0 = pl.multiple_of(c * kblk, kblk)
            x = ki_ref[pl.ds(r0, kblk), :]
            hi = x.astype(BF16)
            lo = x - hi.astype(F32)
            a = (hi.astype(F32) + pltpu.roll(lo, IDX_DIM, 1)).astype(BF16)
            kcat_ref[pl.ds(r0, kblk), 0:LANES] = a
            kcat_ref[pl.ds(r0, kblk), LANES:2 * LANES] = hi
            return carry
        lax.fori_loop(0, l_pad // kblk, prep, 0)

    qit = qi_ref[...].T
    qhi = qit.astype(BF16)
    qlo = (qit - qhi.astype(F32)).astype(BF16)
    zero = jnp.zeros((IDX_DIM, qb), BF16)
    for h in range(IDX_HEADS):
        hs = slice(h * IDX_DIM, (h + 1) * IDX_DIM)
        cs = slice(h * qb, (h + 1) * qb)
        wq_ref[0:64, cs] = qhi[hs]
        wq_ref[64:128, cs] = qhi[hs]
        wq_ref[128:192, cs] = qlo[hs]
        wq_ref[192:256, cs] = zero
    wit = wi_ref[...].T
    qat = (qa_ref[...] * (ATT_HD ** -0.5 * LOG2E)).T.astype(BF16)
    for h in range(ATT_HEADS):
        g, r = divmod(h, ATT_REP)
        qt_ref[g, :, r * qb:(r + 1) * qb] = qat[h * ATT_HD:(h + 1) * ATT_HD]

    q0 = past + i * qb
    pos = q0 + lax.broadcasted_iota(I32, (1, qb), 1)
    pos_chunk = lax.shift_right_logical(pos, 6)
    adm_keys = jnp.minimum(l_real, (lax.shift_right_logical(q0 + qb - 1, 6) + 1) * CHUNK)
    nkc = lax.div(adm_keys + (kblk - 1), kblk)

    def rows_of(c):
        return pl.ds(pl.multiple_of(c * kblk, kblk), kblk)

    def key_index(c):
        return c * kblk + lax.broadcasted_iota(I32, (kblk, qb), 0)

    def score_body(c, carry):
        kc = kcat_ref[rows_of(c), :]
        acc = jnp.zeros((kblk, qb), F32)
        for hg in range(IDX_HEADS // 4):
            rel = jnp.dot(kc, wq_ref[:, hg * 4 * qb:(hg + 1) * 4 * qb],
                          preferred_element_type=F32)
            for hh in range(4):
                h = hg * 4 + hh
                acc = acc + jnp.maximum(rel[:, hh * qb:(hh + 1) * qb], 0.0) * wit[h:h + 1, :]
        score = acc * IDX_SCALE
        s_idx = key_index(c)
        adm = (lax.shift_right_logical(s_idx, 6) <= pos_chunk) & (s_idx < l_real)
        score = jnp.where(adm, score, -jnp.inf)
        bits = pltpu.bitcast(score, I32)
        key = jnp.where(bits < 0, -(bits & 0x7FFFFFFF), bits)
        keys_ref[rows_of(c), :] = key
        hi_ref[rows_of(c), :] = lax.shift_right_arithmetic(key, 16).astype(I16)
        return carry
    lax.fori_loop(0, nkc, score_body, 0)

    n_acc = 4
    one16 = jnp.ones((), I16)
    zero16 = jnp.zeros((), I16)

    def all_rows(x, op, init, slot):
        red_ref[slot] = x
        cols = []
        for c0 in range(0, qb, LANES):
            even = jnp.full((SUBLANES, LANES), init, F32)
            odd = jnp.full((SUBLANES, LANES), init, F32)
            for r in range(0, SUBLANES, 2):
                even = op(even, red_ref[slot, r:r + 1, c0:c0 + LANES])
                odd = op(odd, red_ref[slot, r + 1:r + 2, c0:c0 + LANES])
            cols.append(op(even, odd))
        return jnp.concatenate(cols, axis=1)

    def all_rows_sum(x, slot=0):
        return all_rows(x, jnp.add, 0.0, slot)

    def count16(ref, cand):
        c16 = jnp.concatenate([cand, cand], axis=0).astype(I16)[None]

        def body(c, accs):
            blk = ref[rows_of(c), :].reshape(kblk // PACKED_ROWS, PACKED_ROWS, qb)
            ind = jnp.where(blk >= c16, one16, zero16)
            accs = list(accs)
            for s in range(kblk // PACKED_ROWS):
                accs[s % n_acc] = accs[s % n_acc] + ind[s]
            return tuple(accs)
        accs = lax.fori_loop(0, nkc, body, (jnp.zeros((PACKED_ROWS, qb), I16),) * n_acc)
        tot = (accs[0].astype(I32) + accs[1].astype(I32)) + (accs[2].astype(I32) + accs[3].astype(I32))
        return all_rows_sum((tot[0:SUBLANES] + tot[SUBLANES:]).astype(F32))

    def search16(ref):
        ans = jnp.full((SUBLANES, qb), -32768, I32)
        cnt = jnp.full((SUBLANES, qb), 1e9, F32)
        for bit in range(15, -1, -1):
            cand = ans + (1 << bit)
            c = count16(ref, cand)
            ok = c >= kf
            ans = jnp.where(ok, cand, ans)
            cnt = jnp.where(ok, c, cnt)
        return ans[0:1], cnt[0:1]

    thr_hi, _ = search16(hi_ref)

    def low_body(c, carry):
        key = keys_ref[rows_of(c), :]
        hi = lax.shift_right_arithmetic(key, 16)
        lo = (key & 0xFFFF) - 32768
        k2 = jnp.where(hi > thr_hi, 32767, jnp.where(hi < thr_hi, -32768, lo))
        lo_ref[rows_of(c), :] = k2.astype(I16)
        return carry
    lax.fori_loop(0, nkc, low_body, 0)
    thr_lo, n_ge = search16(lo_ref)
    thr = thr_hi * 65536 + (thr_lo + 32768)
    thr_eff = jnp.maximum(thr, KEY_NEG_INF + 1)

    tied = (n_ge > kf) & (thr > KEY_NEG_INF)

    @pl.when(jnp.max(jnp.where(tied, 1.0, 0.0)) > 0.0)
    def _():
        def count32(pred_fn):
            def body(c, acc):
                ind = jnp.where(pred_fn(keys_ref[rows_of(c), :], key_index(c)), 1.0, 0.0)
                return acc + jnp.sum(ind.reshape(kblk // SUBLANES, SUBLANES, qb), axis=0)
            acc = lax.fori_loop(0, nkc, body, jnp.zeros((SUBLANES, qb), F32))
            return jnp.sum(acc, axis=0, keepdims=True)
        need = kf - count32(lambda key, s: key > thr)
        nbits = int(l_pad - 1).bit_length()

        def idx_body(it, p):
            cand = p + lax.shift_left(jnp.int32(1), nbits - 1 - it)
            few = count32(lambda key, s: (key == thr) & (s < cand)) < need
            return jnp.where(few, cand, p)
        p_last = lax.fori_loop(0, nbits, idx_body, jnp.zeros((1, qb), I32))

        def demote(c, carry):
            key = keys_ref[rows_of(c), :]
            drop = tied & (key == thr) & (key_index(c) > p_last)
            keys_ref[rows_of(c), :] = jnp.where(drop, thr - 1, key)
            return carry
        lax.fori_loop(0, nkc, demote, 0)

    m_ref[...] = jnp.full(m_ref.shape, M_FLOOR, F32)
    l_ref[...] = jnp.zeros(l_ref.shape, F32)
    acc_ref[...] = jnp.zeros(acc_ref.shape, F32)
    slopes = [float(2.0 ** (-8.0 * (h + 1) / ATT_HEADS)) * LOG2E for h in range(ATT_HEADS)]
    nsub = kblk // ATT_SUB
    rows8 = lambda t: t.reshape(t.shape[0] // SUBLANES, SUBLANES, t.shape[1])

    def logits(g, c):
        lg_ref[g] = jnp.dot(k_ref[rows_of(c), g * ATT_HD:(g + 1) * ATT_HD], qt_ref[g],
                            preferred_element_type=F32)

    def softmax_block(g):
        for r in range(ATT_REP):
            h = g * ATT_REP + r
            cs = slice(r * qb, (r + 1) * qb)
            mx = jnp.full((SUBLANES, qb), M_FLOOR, F32)
            for sb in range(nsub):
                rs = slice(sb * ATT_SUB, (sb + 1) * ATT_SUB)
                lgs = lg_ref[g, rs, cs] - slopes[h] * dm_ref[rs, :]
                lg_ref[g, rs, cs] = lgs
                mx = jnp.maximum(mx, jnp.max(rows8(lgs), axis=0))
            m_old = m_ref[g, :, cs]
            m_new = jnp.maximum(m_old, all_rows(mx, jnp.maximum, M_FLOOR, 1))
            sm = jnp.zeros((SUBLANES, qb), F32)
            for sb in range(nsub):
                rs = slice(sb * ATT_SUB, (sb + 1) * ATT_SUB)
                pr = jnp.exp2(rows8(lg_ref[g, rs, cs]) - m_new[None])
                sm = sm + jnp.sum(pr, axis=0)
                pt_ref[g, rs, cs] = pr.reshape(ATT_SUB, qb).astype(BF16)
            alpha = jnp.exp2(m_old - m_new)
            l_ref[g, :, cs] = alpha * l_ref[g, :, cs] + all_rows_sum(sm, 2)
            m_ref[g, :, cs] = m_new
            al_ref[g, :, cs] = alpha

    def weighted_values(g, c):
        pv = jnp.dot(vt_ref[g * ATT_HD:(g + 1) * ATT_HD, rows_of(c)], pt_ref[g],
                     preferred_element_type=F32)
        acc_ref[g] = (rows8(acc_ref[g]) * al_ref[g][None]).reshape(ATT_HD, ATT_REP * qb) + pv

    logits(0, 0)

    def att_body(c, carry):
        key = keys_ref[rows_of(c), :]
        dist = jnp.abs(pos - key_index(c)).astype(F32)
        dm_ref[...] = jnp.where(key >= thr_eff, dist, MASK_DIST)
        logits(1, c)
        softmax_block(0)
        weighted_values(0, c)
        logits(0, jnp.minimum(c + 1, nkc - 1))
        softmax_block(1)
        weighted_values(1, c)
        return carry
    lax.fori_loop(0, nkc, att_body, 0)

    for h in range(ATT_HEADS):
        g, r = divmod(h, ATT_REP)
        cs = slice(r * qb, (r + 1) * qb)
        ot = (rows8(acc_ref[g, :, cs]) / l_ref[g, :, cs][None]).reshape(ATT_HD, qb)
        o_ref[:, h * ATT_HD:(h + 1) * ATT_HD] = ot.T.astype(BF16)


def _dsa(q_ops, k_all, vt_all, ki_all, *, nb, nqb, qb, past, l_real, kblk, out_rows=None):
    l_pad = k_all.shape[1]
    topk = min(IDX_TOPK, l_real // 4)
    assert kblk >= topk and kblk % ATT_SUB == 0 and l_pad % kblk == 0
    (qa, qa_map), (qi, qi_map), (wi, wi_map) = q_ops
    kern = functools.partial(_dsa_kernel, past=past, l_real=l_real, kblk=kblk, topk=topk)
    return pl.pallas_call(
        kern,
        grid=(nb, nqb),
        in_specs=[pl.BlockSpec((qb, ATT_W), qa_map), pl.BlockSpec((qb, IDX_HEADS * IDX_DIM), qi_map),
                  pl.BlockSpec((qb, LANES), wi_map),
                  pl.BlockSpec((None, l_pad, ATT_KV * ATT_HD), lambda b, i: (b, 0, 0)),
                  pl.BlockSpec((None, ATT_KV * ATT_HD, l_pad), lambda b, i: (b, 0, 0)),
                  pl.BlockSpec((None, l_pad, LANES), lambda b, i: (b, 0, 0))],
        out_specs=pl.BlockSpec((qb, ATT_W), lambda b, i: (b * nqb + i, 0)),
        out_shape=jax.ShapeDtypeStruct((out_rows or nb * nqb * qb, ATT_W), BF16),
        scratch_shapes=[pltpu.VMEM((l_pad, 2 * LANES), BF16),
                        pltpu.VMEM((l_pad, qb), I32),
                        pltpu.VMEM((l_pad, qb), I16),
                        pltpu.VMEM((l_pad, qb), I16),
                        pltpu.VMEM((2 * LANES, IDX_HEADS * qb), BF16),
                        pltpu.VMEM((ATT_KV, ATT_HD, ATT_REP * qb), BF16),
                        pltpu.VMEM((ATT_KV, SUBLANES, ATT_REP * qb), F32),
                        pltpu.VMEM((ATT_KV, SUBLANES, ATT_REP * qb), F32),
                        pltpu.VMEM((ATT_KV, ATT_HD, ATT_REP * qb), F32),
                        pltpu.VMEM((ATT_KV, kblk, ATT_REP * qb), F32),
                        pltpu.VMEM((ATT_KV, kblk, ATT_REP * qb), BF16),
                        pltpu.VMEM((kblk, qb), F32),
                        pltpu.VMEM((ATT_KV, SUBLANES, ATT_REP * qb), F32),
                        pltpu.VMEM((3, SUBLANES, qb), F32)],
        compiler_params=_cparams(("parallel", "arbitrary")),
        name="dsa",
    )(qa, qi, wi, k_all, vt_all, ki_all)


def _rw_prep_kernel(r_ref, k_ref, v_ref, lr_ref, hr_ref, hk_ref, hv_ref, hlr_ref,
                    sr_ref, sk_ref, sv_ref, slr_ref, mr_ref, mk_ref, mv_ref, mlr_ref,
                    w0_ref, a0_ref, w2_ref, a2_ref, g2_ref,
                    ro_ref, ko_ref, vo_ref, wo_ref, ao_ref, go_ref, *, tiles_per_seq, time_minor):
    first_tile = (pl.program_id(0) % tiles_per_seq) == 0
    put = (lambda t: t.T) if time_minor else (lambda t: t)

    def mix(x_ref, halo_ref, shift_ref, mu_ref):
        x = x_ref[...]
        first = jnp.where(first_tile, shift_ref[...], halo_ref[SUBLANES - 1:SUBLANES, :])
        rows = lax.broadcasted_iota(I32, x.shape, 0)
        prev = jnp.where(rows == 0, first, pltpu.roll(x, 1, 0))
        return x + (prev - x) * mu_ref[...]

    ro_ref[...] = put(mix(r_ref, hr_ref, sr_ref, mr_ref))
    ko_ref[...] = put(mix(k_ref, hk_ref, sk_ref, mk_ref))
    vo_ref[...] = put(mix(v_ref, hv_ref, sv_ref, mv_ref))
    lr = mix(lr_ref, hlr_ref, slr_ref, mlr_ref)
    xw = lr[:, 0:128]
    xa = lr[:, 128:256]
    xg = lr[:, 256:512]
    wpre = w0_ref[...] + _dot3(jnp.tanh(xw), w2_ref[...])
    wo_ref[...] = put(jnp.exp(-RW_DECAY_SCALE * _sigmoid(wpre)))
    ao_ref[...] = put(_sigmoid(a0_ref[...] + _dot3(xa, a2_ref[...])))
    go_ref[...] = _dot3(_sigmoid(xg), g2_ref[...])


def _rw_prep(proj, row0, nb, seq, shifts, mus, w0, a0, w2p, a2p, g2, tm, out_rows=None):
    m = nb * seq
    tps = seq // tm
    rb0 = row0 // tm
    hb = tm // SUBLANES
    cr, ck, cv = _OFF["r"] // RW_W, _OFF["k"] // RW_W, _OFF["v"] // RW_W
    clr = _OFF["xw"] // 512

    def main(c):
        return lambda i: (rb0 + i, c)

    def halo(c):
        return lambda i: (jnp.maximum((rb0 + i) * hb - 1, 0), c)

    bidx = lambda i: (i // tps, 0, 0)
    fix = lambda i: (0, 0)
    wide = [pl.BlockSpec((tm, RW_W), main(cr)), pl.BlockSpec((tm, RW_W), main(ck)),
            pl.BlockSpec((tm, RW_W), main(cv)), pl.BlockSpec((tm, 512), main(clr)),
            pl.BlockSpec((SUBLANES, RW_W), halo(cr)), pl.BlockSpec((SUBLANES, RW_W), halo(ck)),
            pl.BlockSpec((SUBLANES, RW_W), halo(cv)), pl.BlockSpec((SUBLANES, 512), halo(clr))]
    sh = [pl.BlockSpec((None, 1, RW_W), bidx)] * 3 + [pl.BlockSpec((None, 1, 512), bidx)]
    mu = [pl.BlockSpec((1, RW_W), fix)] * 3 + [pl.BlockSpec((1, 512), fix)]
    par = [pl.BlockSpec((1, RW_W), fix), pl.BlockSpec((1, RW_W), fix),
           pl.BlockSpec((128, RW_W), fix), pl.BlockSpec((128, RW_W), fix),
           pl.BlockSpec((RW_G_R, RW_W), fix)]
    token_major = pl.BlockSpec((tm, RW_W), lambda i: (i, 0))
    time_minor = _relayout_ok(nb, seq) and tm % LANES == 0
    if time_minor:
        out = jax.ShapeDtypeStruct((nb, RW_W, seq), F32)
        seq_spec = pl.BlockSpec((None, RW_W, tm), lambda i: (i // tps, 0, i % tps))
    else:
        out = jax.ShapeDtypeStruct((m, RW_W), F32)
        seq_spec = token_major
    return pl.pallas_call(
        functools.partial(_rw_prep_kernel, tiles_per_seq=tps, time_minor=time_minor),
        grid=(m // tm,),
        in_specs=wide + sh + mu + par,
        out_specs=[seq_spec] * 5 + [token_major],
        out_shape=[out] * 5 + [jax.ShapeDtypeStruct((out_rows or m, RW_W), F32)],
        compiler_params=_cparams(("parallel",)),
        name="rwkv_prep",
    )(proj, proj, proj, proj, proj, proj, proj, proj, *shifts, *mus, w0, a0, w2p, a2p, g2)


def _rw_scan_kernel(r_ref, k_ref, v_ref, w_ref, a_ref, kkp_ref, kap_ref, rkp_ref,
                    lng_ref, lnb_ref, s0_ref, o_ref, st_ref, s_scr, tmp_scr):
    tb = pl.program_id(1)
    nsteps = r_ref.shape[0]

    @pl.when(tb == 0)
    def _():
        s_scr[...] = s0_ref[...]

    def step(t, carry):
        k = k_ref[t]
        a = a_ref[t]
        kk = k * kkp_ref[...]
        nrm = jnp.sqrt(jnp.sum(kk * kk, axis=0, keepdims=True))
        kk = kk / jnp.maximum(nrm, 1e-12)
        km = k * (1.0 + (a - 1.0) * kap_ref[...])
        tmp_scr[0] = kk
        tmp_scr[1] = kk * a
        tmp_scr[2] = km
        parts = [jnp.zeros((RW_HD, LANES), F32) for _ in range(2)]
        for kx in range(RW_HD):
            parts[kx % 2] = parts[kx % 2] + s_scr[kx] * tmp_scr[0, kx:kx + 1, :]
        sa = -(parts[0] + parts[1])
        v = v_ref[t]
        outs = [jnp.zeros((RW_HD, LANES), F32) for _ in range(2)]
        for kx in range(RW_HD):
            s_new = (s_scr[kx] * w_ref[t, kx:kx + 1, :] + sa * tmp_scr[1, kx:kx + 1, :]
                     + v * tmp_scr[2, kx:kx + 1, :])
            s_scr[kx] = s_new
            outs[kx % 2] = outs[kx % 2] + s_new * r_ref[t, kx:kx + 1, :]
        o = outs[0] + outs[1]
        mu = jnp.mean(o, axis=0, keepdims=True)
        oc = o - mu
        var = jnp.mean(oc * oc, axis=0, keepdims=True)
        on = oc * lax.rsqrt(var + RW_GN_EPS) * lng_ref[...] + lnb_ref[...]
        bonus = jnp.sum(r_ref[t] * tmp_scr[2] * rkp_ref[...], axis=0, keepdims=True) * v
        o_ref[t] = on + bonus
        return carry
    lax.fori_loop(0, nsteps, step, 0)

    @pl.when(tb == pl.num_programs(1) - 1)
    def _():
        st_ref[...] = s_scr[...]


def _rw_scan(r, k, v, w, a, params, s0, tb):
    seq, _, pairs = r.shape
    assert pairs % LANES == 0, "batch * RW_HEADS must fill whole lane groups"
    seqb = pl.BlockSpec((tb, RW_HD, LANES), lambda g, t: (t, 0, g))
    par = pl.BlockSpec((RW_HD, LANES), lambda g, t: (0, 0))
    st = pl.BlockSpec((RW_HD, RW_HD, LANES), lambda g, t: (0, 0, g))
    return pl.pallas_call(
        _rw_scan_kernel,
        grid=(pairs // LANES, seq // tb),
        in_specs=[seqb] * 5 + [par] * 5 + [st],
        out_specs=[seqb, st],
        out_shape=[jax.ShapeDtypeStruct((seq, RW_HD, pairs), F32),
                   jax.ShapeDtypeStruct((RW_HD, RW_HD, pairs), F32)],
        scratch_shapes=[pltpu.VMEM((RW_HD, RW_HD, LANES), F32),
                        pltpu.VMEM((3, RW_HD, LANES), F32)],
        compiler_params=_cparams(("parallel", "arbitrary")),
        name="rwkv_scan",
    )(r, k, v, w, a, *params, s0)


GLA_SUB = 16


def _gla_kernel(q_ref, k_ref, v_ref, gl_ref, og_ref, w2_ref, gb_ref, ng_ref, s0_ref,
                o_ref, st_ref, s_scr, la_scr, att_scr):
    tblk = pl.program_id(1)
    tc = q_ref.shape[0]

    @pl.when(tblk == 0)
    def _():
        for h in range(GLA_HEADS):
            s_scr[h] = s0_ref[h].T

    la_scr[...] = _log_sigmoid(_dot3(gl_ref[...], w2_ref[...]) + gb_ref[...]) * (1.0 / GLA_GATE_NORM)
    rows = lax.broadcasted_iota(I32, (CHUNK, GLA_DK), 0)
    lane = lax.broadcasted_iota(I32, (GLA_SUB, CHUNK), 1)
    jrow = lax.broadcasted_iota(I32, (GLA_SUB, GLA_DK), 0)
    nsub = CHUNK // GLA_SUB

    def head_chunk(h, rs):
        ks = slice(h * GLA_DK, (h + 1) * GLA_DK)
        vs = slice(h * GLA_DV, (h + 1) * GLA_DV)
        b = la_scr[rs, ks]
        sh = 1
        while sh < CHUNK:
            b = b + jnp.where(rows >= sh, pltpu.roll(b, sh, 0), 0.0)
            sh *= 2
        q = q_ref[rs, ks] * (GLA_DK ** -0.5)
        k = k_ref[rs, ks]
        v = v_ref[rs, vs]
        s_old = s_scr[h]
        inter = _nt_dot(q * jnp.exp(b), s_old)
        b_last = b[CHUNK - 1:CHUNK, :]

        for jb in range(nsub):
            js = slice(jb * GLA_SUB, (jb + 1) * GLA_SUB)
            bj = b[js]
            kj = k[js]
            blk = jnp.zeros((GLA_SUB, CHUNK), F32)
            for ii in range(GLA_SUB):
                i_row = jb * GLA_SUB + ii
                e = jnp.exp(jnp.minimum(b[i_row:i_row + 1, :] - bj, 0.0))
                val = jnp.where(jrow <= ii, q[i_row:i_row + 1, :] * kj * e, 0.0)
                col = jnp.sum(val, axis=1, keepdims=True)
                blk = jnp.where(lane == i_row, col, blk)
            if jb + 1 < nsub:
                i0 = (jb + 1) * GLA_SUB
                be = b[i0 - 1:i0, :]
                kt = kj * jnp.exp(be - bj)
                later = rows >= i0
                qt = jnp.where(later, q * jnp.exp(jnp.where(later, b - be, 0.0)), 0.0)
                blk = blk + _nt_dot(kt, qt)
            att_scr[h, js, :] = blk
        o = inter + _tn_dot(att_scr[h], v)
        khat = k * jnp.exp(b_last - b)
        s_scr[h] = s_old * jnp.exp(b_last) + _tn_dot(v, khat)
        o = o * lax.rsqrt(jnp.mean(o * o, -1, keepdims=True) + GLA_NORM_EPS) * ng_ref[:, vs]
        og = og_ref[rs, vs]
        o_ref[rs, vs] = (o * (og * _sigmoid(og))).astype(BF16)

    def chunk_body(c, carry):
        rs = pl.ds(pl.multiple_of(c * CHUNK, CHUNK), CHUNK)
        for h in range(GLA_HEADS):
            head_chunk(h, rs)
        return carry
    lax.fori_loop(0, tc // CHUNK, chunk_body, 0)

    @pl.when(tblk == pl.num_programs(1) - 1)
    def _():
        for h in range(GLA_HEADS):
            st_ref[h] = s_scr[h].T


def _gla(proj, tail, row0, nb, seq, w2p, gb, ng, s0, tc, out_rows=None):
    ntb = seq // tc
    rb0 = row0 // tc
    cq, ck = _OFF["qc"] // GLA_KW, _OFF["kc"] // GLA_KW
    cv, cg = _OFF["vc"] // GLA_VW, _OFF["og"] // GLA_VW
    cgl = (_OFF["gl"] - _OFF["ka"]) // LANES

    def rows(c0):
        return lambda b, t: (rb0 + b * ntb + t, c0)

    fix = lambda b, t: (0, 0)
    st = pl.BlockSpec((None, GLA_HEADS, GLA_DK, GLA_DV), lambda b, t: (b, 0, 0, 0))
    return pl.pallas_call(
        _gla_kernel,
        grid=(nb, ntb),
        in_specs=[pl.BlockSpec((tc, GLA_KW), rows(cq)), pl.BlockSpec((tc, GLA_KW), rows(ck)),
                  pl.BlockSpec((tc, GLA_VW), rows(cv)), pl.BlockSpec((tc, LANES), rows(cgl)),
                  pl.BlockSpec((tc, GLA_VW), rows(cg)),
                  pl.BlockSpec((LANES, GLA_KW), fix), pl.BlockSpec((1, GLA_KW), fix),
                  pl.BlockSpec((1, GLA_VW), fix), st],
        out_specs=[pl.BlockSpec((tc, GLA_VW), lambda b, t: (b * ntb + t, 0)), st],
        out_shape=[jax.ShapeDtypeStruct((out_rows or nb * seq, GLA_VW), BF16),
                   jax.ShapeDtypeStruct((nb, GLA_HEADS, GLA_DK, GLA_DV), F32)],
        scratch_shapes=[pltpu.VMEM((GLA_HEADS, GLA_DV, GLA_DK), F32), pltpu.VMEM((tc, GLA_KW), F32),
                        pltpu.VMEM((GLA_HEADS, CHUNK, CHUNK), F32)],
        compiler_params=_cparams(("parallel", "arbitrary")),
        name="gla",
    )(proj, proj, proj, tail, proj, w2p, gb, ng, s0)


_MAIN_SEGS = tuple(sg for sg in _SEGS if _OFF[sg[0]] < _OFF["ka"])
_TAIL_SEGS = tuple(sg for sg in _SEGS if _OFF[sg[0]] >= _OFF["ka"])


def _pack_cols(w, segs=_SEGS, dtype=None):
    dtype = dtype or w.dtype
    parts = []
    for _, s, wd, p in segs:
        parts.append(w[..., s:s + wd].astype(dtype))
        if p > wd:
            parts.append(jnp.zeros(w.shape[:-1] + (p - wd,), dtype))
    return jnp.concatenate(parts, axis=-1)


def _pack_rw(vec):
    r, k, v = vec[..., 0:1024], vec[..., 1024:2048], vec[..., 2048:3072]
    z = jnp.zeros(vec.shape[:-1] + (32,), vec.dtype)
    lr = jnp.concatenate([vec[..., 3072:3168], z, vec[..., 3168:3264], z, vec[..., 3264:3520]], -1)
    return r, k, v, lr


def _pad_rows(w, n):
    return jnp.concatenate([w, jnp.zeros((n - w.shape[0],) + w.shape[1:], w.dtype)], 0)


RELAYOUT_T = 128


def _to_scan_kernel(y_ref, o_ref):
    nbat = y_ref.shape[0]
    for n in range(RW_HD):
        z = jnp.concatenate([y_ref[b, pl.ds(n, RW_HEADS, stride=RW_HD), :] for b in range(nbat)], axis=0)
        o_ref[:, n, :] = z.T


def _from_scan_kernel(o_ref, x_ref, y_scr):
    nbat = x_ref.shape[0]
    for n in range(RW_HD):
        zt = o_ref[:, n, :].T
        for b in range(nbat):
            y_scr[b, pl.ds(n, RW_HEADS, stride=RW_HD), :] = zt[b * RW_HEADS:(b + 1) * RW_HEADS]
    for b in range(nbat):
        for c in range(RW_W // LANES):
            x_ref[b, :, c * LANES:(c + 1) * LANES] = y_scr[b, c * LANES:(c + 1) * LANES, :].T


def _relayout_ok(nb, seq):
    return nb * RW_HEADS == LANES and seq % RELAYOUT_T == 0


def _to_scan(x, nb, seq):
    if x.ndim == 2:
        return jnp.transpose(x.reshape(nb, seq, RW_HEADS, RW_HD), (1, 3, 0, 2)).reshape(seq, RW_HD, nb * RW_HEADS)
    return pl.pallas_call(
        _to_scan_kernel,
        grid=(seq // RELAYOUT_T,),
        in_specs=[pl.BlockSpec((nb, RW_W, RELAYOUT_T), lambda t: (0, 0, t))],
        out_specs=pl.BlockSpec((RELAYOUT_T, RW_HD, LANES), lambda t: (t, 0, 0)),
        out_shape=jax.ShapeDtypeStruct((seq, RW_HD, LANES), F32),
        compiler_params=_cparams(("parallel",)),
        name="to_scan_layout",
    )(x)


def _from_scan(o, nb, seq):
    if not _relayout_ok(nb, seq):
        return jnp.transpose(o.reshape(seq, RW_HD, nb, RW_HEADS), (2, 0, 3, 1)).reshape(nb * seq, RW_W)
    return pl.pallas_call(
        _from_scan_kernel,
        grid=(seq // RELAYOUT_T,),
        in_specs=[pl.BlockSpec((RELAYOUT_T, RW_HD, LANES), lambda t: (t, 0, 0))],
        out_specs=pl.BlockSpec((nb, RELAYOUT_T, RW_W), lambda t: (0, t, 0)),
        out_shape=jax.ShapeDtypeStruct((nb, seq, RW_W), F32),
        scratch_shapes=[pltpu.VMEM((nb, RW_W, RELAYOUT_T), F32)],
        compiler_params=_cparams(("parallel",)),
        name="from_scan_layout",
    )(o).reshape(nb * seq, RW_W)


def _group_mixers(proj, tail, row0, nb, seq, caches, lw, out_rows=None):
    m = nb * seq
    cache_k, cache_v, cache_ki, s_rw, s_shift, s_gla = caches
    past = 0 if cache_k is None else cache_k.shape[1]
    l_real = past + seq
    t0 = _OFF["ka"]
    grp = tail[row0:row0 + m].reshape(nb, seq, -1)

    ka = grp[..., _OFF["ka"] - t0:_OFF["ka"] - t0 + 256]
    va = grp[..., _OFF["va"] - t0:_OFF["va"] - t0 + 256]
    ki = grp[..., _OFF["ki"] - t0:_OFF["ki"] - t0 + LANES]
    if past:
        k_all = jnp.concatenate([cache_k.reshape(nb, past, 256), ka], 1)
        v_all = jnp.concatenate([cache_v.reshape(nb, past, 256), va], 1)
        cki = jnp.concatenate([cache_ki, jnp.zeros((nb, past, LANES - IDX_DIM), F32)], -1)
        ki_all = jnp.concatenate([cki, ki], 1)
    else:
        k_all, v_all, ki_all = ka, va, ki
    kblk = 512 if l_real % 512 == 0 else 384
    l_pad = -(-l_real // kblk) * kblk
    if l_pad > l_real:
        padk = lambda t: jnp.concatenate([t, jnp.zeros((nb, l_pad - l_real, t.shape[-1]), t.dtype)], 1)
        k_all, v_all, ki_all = padk(k_all), padk(v_all), padk(ki_all)
    k_bf = k_all.astype(BF16)
    vt_bf = jnp.swapaxes(v_all, 1, 2).astype(BF16)
    qb = DSA_QB if seq % DSA_QB == 0 else LANES
    if seq % qb == 0:
        nqb = seq // qb
        rb0 = row0 // qb
        mk = lambda c: (lambda b, i: (rb0 + b * nqb + i, c))
        q_ops = ((proj, mk(_OFF["qa"] // ATT_W)), (proj, mk(_OFF["qi"] // 1024)),
                 (tail, mk((_OFF["wi"] - t0) // LANES)))
        o_a = _dsa(q_ops, k_bf, vt_bf, ki_all, nb=nb, nqb=nqb, qb=qb, past=past, l_real=l_real, kblk=kblk,
                   out_rows=out_rows)
    else:
        def padq(t):
            return jnp.concatenate([t, jnp.zeros((nb, qb - seq, t.shape[-1]), F32)], 1).reshape(nb * qb, -1)
        qrows = proj[row0:row0 + m, 0:_OFF["vc"]].reshape(nb, seq, -1)
        mk = lambda b, i: (b, 0)
        q_ops = ((padq(qrows[..., 0:ATT_W]), mk), (padq(qrows[..., ATT_W:]), mk),
                 (padq(grp[..., _OFF["wi"] - t0:_OFF["wi"] - t0 + LANES]), mk))
        o_a = _dsa(q_ops, k_bf, vt_bf, ki_all, nb=nb, nqb=1, qb=qb, past=past, l_real=l_real, kblk=kblk)
        o_a = o_a.reshape(nb, qb, ATT_W)[:, :seq].reshape(m, ATT_W)

    if s_shift is None:
        s_shift = jnp.zeros((nb, RW_PROJ), F32)
    shifts = [t.reshape(nb, 1, -1) for t in _pack_rw(s_shift)]
    tm = 256 if seq % 256 == 0 else seq
    r, k, v, w, a, g = _rw_prep(proj, row0, nb, seq, shifts, lw["rw_mus"], lw["rw_w0"], lw["rw_a0"],
                                lw["rw_w2p"], lw["rw_a2p"], lw["rw_g2"], tm, out_rows)
    if s_rw is None:
        s0 = jnp.zeros((RW_HD, RW_HD, nb * RW_HEADS), F32)
    else:
        s0 = jnp.transpose(s_rw, (3, 2, 0, 1)).reshape(RW_HD, RW_HD, nb * RW_HEADS)
    tile = lambda p: jnp.tile(p.reshape(RW_HEADS, RW_HD).T, (1, LANES // RW_HEADS))
    params = [tile(lw[n]) for n in ("rw_kk", "rw_ka", "rw_rk", "rw_ln_g", "rw_ln_b")]
    tb = 32 if seq % 32 == 0 else seq
    o_b, s_t = _rw_scan(*[_to_scan(t, nb, seq) for t in (r, k, v, w, a)], params, s0, tb)
    o_b = _from_scan(o_b, nb, seq)
    s_rw_new = jnp.transpose(s_t.reshape(RW_HD, RW_HD, nb, RW_HEADS), (2, 3, 1, 0))
    last = jnp.take(proj, row0 + seq - 1 + seq * jnp.arange(nb), axis=0)[:, _OFF["r"]:_OFF["qc"]]
    base = _OFF["r"]
    shift_new = jnp.concatenate(
        [last[:, 0:3072], last[:, _OFF["xw"] - base:_OFF["xw"] - base + 96],
         last[:, _OFF["xa"] - base:_OFF["xa"] - base + 96], last[:, _OFF["xg"] - base:]], -1)

    if s_gla is None:
        s_gla = jnp.zeros((nb, GLA_HEADS, GLA_DK, GLA_DV), F32)
    tc = 512 if seq % 512 == 0 else seq
    o_c, s_gla_new = _gla(proj, tail, row0, nb, seq, lw["gla_w2p"], lw["gla_b"], lw["gla_ng"], s_gla, tc,
                          out_rows)

    new_state = (ka.reshape(nb, seq, ATT_KV, ATT_HD), va.reshape(nb, seq, ATT_KV, ATT_HD),
                 ki[..., :IDX_DIM], s_rw_new, shift_new, s_gla_new)
    return o_a, o_b, g, o_c, new_state


def kernel(x_prompt, x_sample, cache_k, cache_v, cache_kidx, state_rwkv, state_rwkv_shift, state_gla,
           w_in, rw_mu, rw_w0, rw_w2, rw_a0, rw_a2, rw_g2, rw_kk, rw_ka, rw_rk, rw_ln_g, rw_ln_b,
           gla_w2, gla_b, gla_norm_g, w_branch_a, w_branch_b, w_branch_c, w_out,
           ln1_g, ln1_b, ln2_g, ln2_b, ffn_w1, ffn_w3, ffn_w2):
    bp, tp, _ = x_prompt.shape
    bs, ts, _ = x_sample.shape
    mp, ms = bp * tp, bs * ts
    x = jnp.concatenate([x_prompt.reshape(mp, D_MODEL), x_sample.reshape(ms, D_MODEL)], 0)
    xb = x.astype(BF16)
    m = mp + ms
    tm_big = 1024 if m % 1024 == 0 else 256
    tm_mid = 512 if m % 512 == 0 else 256

    new_p = [[] for _ in range(6)]
    new_s = [[] for _ in range(6)]
    for l in range(DEPTH):
        lw = {
            "rw_mus": [t.reshape(1, -1) for t in _pack_rw(rw_mu[l])],
            "rw_w0": rw_w0[l].reshape(1, RW_W), "rw_a0": rw_a0[l].reshape(1, RW_W),
            "rw_w2p": _pad_rows(rw_w2[l], 128), "rw_a2p": _pad_rows(rw_a2[l], 128), "rw_g2": rw_g2[l],
            "rw_kk": rw_kk[l], "rw_ka": rw_ka[l], "rw_rk": rw_rk[l],
            "rw_ln_g": rw_ln_g[l], "rw_ln_b": rw_ln_b[l],
            "gla_w2p": _pad_rows(gla_w2[l], LANES), "gla_b": gla_b[l].reshape(1, GLA_KW),
            "gla_ng": gla_norm_g[l].reshape(1, GLA_VW),
        }
        proj = _matmul(xb, _pack_cols(w_in[l], _MAIN_SEGS, BF16), tm_big, 1024)
        tail = _matmul(xb, _pack_cols(w_in[l], _TAIL_SEGS, BF16), tm_big, PROJ_W - _OFF["ka"])
        oa_p, ob_p, g_p, oc_p, st_p = _group_mixers(proj, tail, 0, bp, tp, (None,) * 6, lw, out_rows=m)
        oa_s, ob_s, g_s, oc_s, st_s = _group_mixers(
            proj, tail, mp, bs, ts,
            (cache_k[l], cache_v[l], cache_kidx[l], state_rwkv[l], state_rwkv_shift[l], state_gla[l]), lw)
        put = lambda full, part: lax.dynamic_update_slice(full, part, (mp, 0))
        merged = _merge(put(oa_p, oa_s), jnp.concatenate([ob_p, ob_s], 0), put(g_p, g_s), put(oc_p, oc_s), proj,
                        w_branch_a[l].astype(BF16), w_branch_b[l].astype(BF16),
                        w_branch_c[l].astype(BF16), 256, D_MODEL)
        h, hb = _hproj(merged, w_out[l].astype(BF16), x, ln1_g[l].reshape(1, -1),
                       ln1_b[l].reshape(1, -1), tm_mid)
        x, xb = _ffn(hb, h, ffn_w1[l].astype(BF16), ffn_w3[l].astype(BF16), ffn_w2[l].astype(BF16),
                     ln2_g[l].reshape(1, -1), ln2_b[l].reshape(1, -1), tm_mid, 512)
        for i in range(6):
            new_p[i].append(st_p[i])
            new_s[i].append(st_s[i])
    sp = [jnp.stack(a) for a in new_p]
    ss = [jnp.stack(a) for a in new_s]
    yp = x[:mp].reshape(bp, tp, D_MODEL)
    ys = x[mp:].reshape(bs, ts, D_MODEL)
    return (yp, ys, sp[0], sp[1], sp[2], sp[3], sp[4], sp[5], ss[0], ss[1], ss[2], ss[3], ss[4], ss[5])
```

```python
import functools

import jax
import jax.numpy as jnp
from jax import lax
from jax.experimental import pallas as pl
from jax.experimental.pallas import tpu as pltpu

F32 = jnp.float32
BF16 = jnp.bfloat16
I32 = jnp.int32
I16 = jnp.int16

D_MODEL = 2048
DEPTH = 4
CHUNK = 64
Q_BLOCK = 128
ATT_HEADS = 8
ATT_HD = 128
ATT_KV = 2
ATT_REP = ATT_HEADS // ATT_KV
ATT_W = ATT_HEADS * ATT_HD
IDX_HEADS = 16
IDX_DIM = 64
IDX_TOPK = 256
IDX_SCALE = (IDX_DIM * IDX_HEADS) ** -0.5
RW_HEADS = 16
RW_HD = 64
RW_W = RW_HEADS * RW_HD
RW_DECAY_R = 96
RW_A_R = 96
RW_G_R = 256
RW_PROJ = 3 * RW_W + RW_DECAY_R + RW_A_R + RW_G_R
RW_DECAY_SCALE = 0.606531
RW_GN_EPS = 64e-5
GLA_HEADS = 4
GLA_DK = 128
GLA_DV = 256
GLA_KW = GLA_HEADS * GLA_DK
GLA_VW = GLA_HEADS * GLA_DV
GLA_GATE_R = 16
GLA_GATE_NORM = 16.0
GLA_NORM_EPS = 1e-5
FFN_HIDDEN = -(-8 * D_MODEL // (3 * 256)) * 256
DN_ALPHA = (2 * DEPTH) ** 0.25
LN_EPS = 1e-5
LOG2E = 1.4426950408889634

LANES = 128
SUBLANES = 8
PACKED_ROWS = 16
VMEM_LIMIT = 56 * 1024 * 1024

_SEGS = (
    ("qa", 0, 1024, 1024), ("qi", 1536, 1024, 1024), ("vc", 7184, 1024, 1024),
    ("og", 8224, 1024, 1024), ("gates", 9248, 6144, 6144),
    ("r", 2640, 1024, 1024), ("k", 3664, 1024, 1024), ("v", 4688, 1024, 1024),
    ("xw", 5712, 96, 128), ("xa", 5808, 96, 128), ("xg", 5904, 256, 256),
    ("qc", 6160, 512, 512), ("kc", 6672, 512, 512),
    ("ka", 1024, 256, 256), ("va", 1280, 256, 256),
    ("ki", 2560, 64, 128), ("wi", 2624, 16, 128), ("gl", 8208, 16, 128),
)
_OFF = {}
_o = 0
for _n, _s, _w, _p in _SEGS:
    _OFF[_n] = _o
    _o += _p
PROJ_W = _o
KEY_NEG_INF = -0x7F800000
MASK_DIST = 1e32
M_FLOOR = -1e29


def _cparams(sem):
    return pltpu.CompilerParams(dimension_semantics=sem, vmem_limit_bytes=VMEM_LIMIT)


def _split2(x):
    hi = x.astype(BF16)
    lo = (x - hi.astype(F32)).astype(BF16)
    return hi, lo


def _dot3(a, b):
    ah, al = _split2(a)
    bh, bl = _split2(b)
    d = functools.partial(jnp.dot, preferred_element_type=F32)
    return d(ah, bh) + (d(al, bh) + d(ah, bl))


def _nt_dot(a, b):
    return lax.dot_general(a.astype(BF16), b.astype(BF16), (((1,), (1,)), ((), ())),
                           preferred_element_type=F32)


def _tn_dot(a, b):
    return lax.dot_general(a.astype(BF16), b.astype(BF16), (((0,), (0,)), ((), ())),
                           preferred_element_type=F32)


def _sigmoid(x):
    return 1.0 / (1.0 + jnp.exp(-x))


def _log_sigmoid(x):
    return jnp.minimum(x, 0.0) - jnp.log(1.0 + jnp.exp(-jnp.abs(x)))


def _layer_norm(x, g, b):
    mu = jnp.mean(x, -1, keepdims=True)
    xc = x - mu
    var = jnp.mean(xc * xc, -1, keepdims=True)
    return xc * lax.rsqrt(var + LN_EPS) * g + b


def _mm_kernel(a_ref, b_ref, o_ref):
    o_ref[...] = jnp.dot(a_ref[...], b_ref[...], preferred_element_type=F32)


def _matmul(a, b, tm, tn):
    m, k = a.shape
    n = b.shape[1]
    return pl.pallas_call(
        _mm_kernel,
        grid=(m // tm, pl.cdiv(n, tn)),
        in_specs=[pl.BlockSpec((tm, k), lambda i, j: (i, 0)),
                  pl.BlockSpec((k, tn), lambda i, j: (0, j))],
        out_specs=pl.BlockSpec((tm, tn), lambda i, j: (i, j)),
        out_shape=jax.ShapeDtypeStruct((m, n), F32),
        compiler_params=_cparams(("parallel", "arbitrary")),
        name="in_proj",
    )(a, b)


def _merge_kernel(oa_ref, ob_ref, g_ref, oc_ref, ga_ref, gb_ref, gc_ref,
                  wa_ref, wb_ref, wc_ref, o_ref):
    ob = (ob_ref[...] * g_ref[...]).astype(BF16)
    d = functools.partial(jnp.dot, preferred_element_type=F32)
    acc = _sigmoid(ga_ref[...]) * d(oa_ref[...], wa_ref[...])
    acc += _sigmoid(gb_ref[...]) * d(ob, wb_ref[...])
    acc += _sigmoid(gc_ref[...]) * d(oc_ref[...], wc_ref[...])
    o_ref[...] = acc.astype(BF16)


def _merge(oa, ob, g, oc, proj, wa, wb, wc, tm, tn):
    m = oa.shape[0]
    gblk = _OFF["gates"] // tn
    nj = D_MODEL // tn
    row = lambda i, j: (i, 0)
    wspec = pl.BlockSpec((ATT_W, tn), lambda i, j: (0, j))
    return pl.pallas_call(
        _merge_kernel,
        grid=(m // tm, nj),
        in_specs=[pl.BlockSpec((tm, ATT_W), row), pl.BlockSpec((tm, RW_W), row),
                  pl.BlockSpec((tm, RW_W), row), pl.BlockSpec((tm, GLA_VW), row),
                  pl.BlockSpec((tm, tn), lambda i, j: (i, gblk + j)),
                  pl.BlockSpec((tm, tn), lambda i, j: (i, gblk + nj + j)),
                  pl.BlockSpec((tm, tn), lambda i, j: (i, gblk + 2 * nj + j)),
                  wspec, wspec, wspec],
        out_specs=pl.BlockSpec((tm, tn), lambda i, j: (i, j)),
        out_shape=jax.ShapeDtypeStruct((m, D_MODEL), BF16),
        compiler_params=_cparams(("parallel", "arbitrary")),
        name="merge",
    )(oa, ob, g, oc, proj, proj, proj, wa, wb, wc)


def _hproj_kernel(m_ref, w_ref, x_ref, g_ref, b_ref, h_ref, hb_ref):
    z = DN_ALPHA * x_ref[...] + jnp.dot(m_ref[...], w_ref[...], preferred_element_type=F32)
    h = _layer_norm(z, g_ref[...], b_ref[...])
    h_ref[...] = h
    hb_ref[...] = h.astype(BF16)


def _hproj(merged, w_out, x, g, b, tm):
    m = x.shape[0]
    row = lambda i: (i, 0)
    fix = lambda i: (0, 0)
    return pl.pallas_call(
        _hproj_kernel,
        grid=(m // tm,),
        in_specs=[pl.BlockSpec((tm, D_MODEL), row), pl.BlockSpec((D_MODEL, D_MODEL), fix),
                  pl.BlockSpec((tm, D_MODEL), row), pl.BlockSpec((1, D_MODEL), fix),
                  pl.BlockSpec((1, D_MODEL), fix)],
        out_specs=[pl.BlockSpec((tm, D_MODEL), row), pl.BlockSpec((tm, D_MODEL), row)],
        out_shape=[jax.ShapeDtypeStruct((m, D_MODEL), F32),
                   jax.ShapeDtypeStruct((m, D_MODEL), BF16)],
        compiler_params=_cparams(("parallel",)),
        name="out_proj_ln",
    )(merged, w_out, x, g, b)


def _ffn_kernel(hb_ref, h_ref, w1_ref, w3_ref, w2_ref, g_ref, b_ref, y_ref, yb_ref, acc_ref):
    j = pl.program_id(1)

    @pl.when(j == 0)
    def _():
        acc_ref[...] = jnp.zeros_like(acc_ref)

    hb = hb_ref[...]
    u1 = jnp.dot(hb, w1_ref[...], preferred_element_type=F32)
    u3 = jnp.dot(hb, w3_ref[...], preferred_element_type=F32)
    u = (u1 * _sigmoid(u1) * u3).astype(BF16)
    acc_ref[...] += jnp.dot(u, w2_ref[...], preferred_element_type=F32)

    @pl.when(j == pl.num_programs(1) - 1)
    def _():
        y = _layer_norm(DN_ALPHA * h_ref[...] + acc_ref[...], g_ref[...], b_ref[...])
        y_ref[...] = y
        yb_ref[...] = y.astype(BF16)


def _ffn(hb, h, w1, w3, w2, g, b, tm, th):
    m = h.shape[0]
    row = lambda i, j: (i, 0)
    fix = lambda i, j: (0, 0)
    return pl.pallas_call(
        _ffn_kernel,
        grid=(m // tm, FFN_HIDDEN // th),
        in_specs=[pl.BlockSpec((tm, D_MODEL), row), pl.BlockSpec((tm, D_MODEL), row),
                  pl.BlockSpec((D_MODEL, th), lambda i, j: (0, j)),
                  pl.BlockSpec((D_MODEL, th), lambda i, j: (0, j)),
                  pl.BlockSpec((th, D_MODEL), lambda i, j: (j, 0)),
                  pl.BlockSpec((1, D_MODEL), fix), pl.BlockSpec((1, D_MODEL), fix)],
        out_specs=[pl.BlockSpec((tm, D_MODEL), row), pl.BlockSpec((tm, D_MODEL), row)],
        out_shape=[jax.ShapeDtypeStruct((m, D_MODEL), F32),
                   jax.ShapeDtypeStruct((m, D_MODEL), BF16)],
        scratch_shapes=[pltpu.VMEM((tm, D_MODEL), F32)],
        compiler_params=_cparams(("parallel", "arbitrary")),
        name="ffn_ln",
    )(hb, h, w1, w3, w2, g, b)


ATT_SUB = 128
DSA_QB = 256


def _dsa_kernel(qa_ref, qi_ref, wi_ref, k_ref, vt_ref, ki_ref, o_ref,
                kcat_ref, keys_ref, hi_ref, lo_ref, wq_ref, qt_ref, m_ref, l_ref, acc_ref,
                lg_ref, pt_ref, dm_ref, al_ref, red_ref, *, past, l_real, kblk, topk):
    qb = qa_ref.shape[0]
    l_pad = ki_ref.shape[0]
    i = pl.program_id(1)
    kf = float(topk)

    @pl.when(i == 0)
    def _():
        def prep(c, carry):
            r0 = pl.multiple_of(c * kblk, kblk)
            x = ki_ref[pl.ds(r0, kblk), :]
            hi = x.astype(BF16)
            lo = x - hi.astype(F32)
            a = (hi.astype(F32) + pltpu.roll(lo, IDX_DIM, 1)).astype(BF16)
            kcat_ref[pl.ds(r0, kblk), 0:LANES] = a
            kcat_ref[pl.ds(r0, kblk), LANES:2 * LANES] = hi
            return carry
        lax.fori_loop(0, l_pad // kblk, prep, 0)

    qit = qi_ref[...].T
    qhi = qit.astype(BF16)
    qlo = (qit - qhi.astype(F32)).astype(BF16)
    zero = jnp.zeros((IDX_DIM, qb), BF16)
    for h in range(IDX_HEADS):
        hs = slice(h * IDX_DIM, (h + 1) * IDX_DIM)
        cs = slice(h * qb, (h + 1) * qb)
        wq_ref[0:64, cs] = qhi[hs]
        wq_ref[64:128, cs] = qhi[hs]
        wq_ref[128:192, cs] = qlo[hs]
        wq_ref[192:256, cs] = zero
    wit = wi_ref[...].T
    qat = (qa_ref[...] * (ATT_HD ** -0.5 * LOG2E)).T.astype(BF16)
    for h in range(ATT_HEADS):
        g, r = divmod(h, ATT_REP)
        qt_ref[g, :, r * qb:(r + 1) * qb] = qat[h * ATT_HD:(h + 1) * ATT_HD]

    q0 = past + i * qb
    pos = q0 + lax.broadcasted_iota(I32, (1, qb), 1)
    pos_chunk = lax.shift_right_logical(pos, 6)
    adm_keys = jnp.minimum(l_real, (lax.shift_right_logical(q0 + qb - 1, 6) + 1) * CHUNK)
    nkc = lax.div(adm_keys + (kblk - 1), kblk)

    def rows_of(c):
        return pl.ds(pl.multiple_of(c * kblk, kblk), kblk)

    def key_index(c):
        return c * kblk + lax.broadcasted_iota(I32, (kblk, qb), 0)

    def score_body(c, carry):
        kc = kcat_ref[rows_of(c), :]
        acc = jnp.zeros((kblk, qb), F32)
        for hg in range(IDX_HEADS // 4):
            rel = jnp.dot(kc, wq_ref[:, hg * 4 * qb:(hg + 1) * 4 * qb],
                          preferred_element_type=F32)
            for hh in range(4):
                h = hg * 4 + hh
                acc = acc + jnp.maximum(rel[:, hh * qb:(hh + 1) * qb], 0.0) * wit[h:h + 1, :]
        score = acc * IDX_SCALE
        s_idx = key_index(c)
        adm = (lax.shift_right_logical(s_idx, 6) <= pos_chunk) & (s_idx < l_real)
        score = jnp.where(adm, score, -jnp.inf)
        bits = pltpu.bitcast(score, I32)
        key = jnp.where(bits < 0, -(bits & 0x7FFFFFFF), bits)
        keys_ref[rows_of(c), :] = key
        hi_ref[rows_of(c), :] = lax.shift_right_arithmetic(key, 16).astype(I16)
        return carry
    lax.fori_loop(0, nkc, score_body, 0)

    n_acc = 2
    one16 = jnp.ones((), I16)
    zero16 = jnp.zeros((), I16)

    def all_rows(x, op, init, slot):
        red_ref[slot] = x
        cols = []
        for c0 in range(0, qb, LANES):
            even = jnp.full((SUBLANES, LANES), init, F32)
            odd = jnp.full((SUBLANES, LANES), init, F32)
            for r in range(0, SUBLANES, 2):
                even = op(even, red_ref[slot, r:r + 1, c0:c0 + LANES])
                odd = op(odd, red_ref[slot, r + 1:r + 2, c0:c0 + LANES])
            cols.append(op(even, odd))
        return jnp.concatenate(cols, axis=1)

    def all_rows_sum(x, slot=0):
        return all_rows(x, jnp.add, 0.0, slot)

    n_cand = 3

    def count16(ref, cands):
        c16 = [jnp.concatenate([cd, cd], axis=0).astype(I16)[None] for cd in cands]

        def body(c, accs):
            blk = ref[rows_of(c), :].reshape(kblk // PACKED_ROWS, PACKED_ROWS, qb)
            accs = list(accs)
            for j in range(n_cand):
                ind = jnp.where(blk >= c16[j], one16, zero16)
                for s in range(kblk // PACKED_ROWS):
                    a = j * n_acc + s % n_acc
                    accs[a] = accs[a] + ind[s]
            return tuple(accs)
        accs = lax.fori_loop(0, nkc, body, (jnp.zeros((PACKED_ROWS, qb), I16),) * (n_acc * n_cand))
        outs = []
        for j in range(n_cand):
            a = [accs[j * n_acc + s].astype(I32) for s in range(n_acc)]
            tot = sum(a[1:], a[0])
            outs.append(all_rows_sum((tot[0:SUBLANES] + tot[SUBLANES:]).astype(F32), j))
        return outs

    def search16(ref):
        ans = jnp.full((SUBLANES, qb), -32768, I32)
        cnt = jnp.full((SUBLANES, qb), 1e9, F32)
        for bit in range(14, -1, -2):
            cands = [ans + (j + 1) * (1 << bit) for j in range(n_cand)]
            counts = count16(ref, cands)
            for cd, c in zip(cands, counts):
                ok = c >= kf
                ans = jnp.where(ok, cd, ans)
                cnt = jnp.where(ok, c, cnt)
        return ans[0:1], cnt[0:1]

    thr_hi, _ = search16(hi_ref)

    def low_body(c, carry):
        key = keys_ref[rows_of(c), :]
        hi = lax.shift_right_arithmetic(key, 16)
        lo = (key & 0xFFFF) - 32768
        k2 = jnp.where(hi > thr_hi, 32767, jnp.where(hi < thr_hi, -32768, lo))
        lo_ref[rows_of(c), :] = k2.astype(I16)
        return carry
    lax.fori_loop(0, nkc, low_body, 0)
    thr_lo, n_ge = search16(lo_ref)
    thr = thr_hi * 65536 + (thr_lo + 32768)
    thr_eff = jnp.maximum(thr, KEY_NEG_INF + 1)

    tied = (n_ge > kf) & (thr > KEY_NEG_INF)

    @pl.when(jnp.max(jnp.where(tied, 1.0, 0.0)) > 0.0)
    def _():
        def count32(pred_fn):
            def body(c, acc):
                ind = jnp.where(pred_fn(keys_ref[rows_of(c), :], key_index(c)), 1.0, 0.0)
                return acc + jnp.sum(ind.reshape(kblk // SUBLANES, SUBLANES, qb), axis=0)
            acc = lax.fori_loop(0, nkc, body, jnp.zeros((SUBLANES, qb), F32))
            return jnp.sum(acc, axis=0, keepdims=True)
        need = kf - count32(lambda key, s: key > thr)
        nbits = int(l_pad - 1).bit_length()

        def idx_body(it, p):
            cand = p + lax.shift_left(jnp.int32(1), nbits - 1 - it)
            few = count32(lambda key, s: (key == thr) & (s < cand)) < need
            return jnp.where(few, cand, p)
        p_last = lax.fori_loop(0, nbits, idx_body, jnp.zeros((1, qb), I32))

        def demote(c, carry):
            key = keys_ref[rows_of(c), :]
            drop = tied & (key == thr) & (key_index(c) > p_last)
            keys_ref[rows_of(c), :] = jnp.where(drop, thr - 1, key)
            return carry
        lax.fori_loop(0, nkc, demote, 0)

    m_ref[...] = jnp.full(m_ref.shape, M_FLOOR, F32)
    l_ref[...] = jnp.zeros(l_ref.shape, F32)
    acc_ref[...] = jnp.zeros(acc_ref.shape, F32)
    slopes = [float(2.0 ** (-8.0 * (h + 1) / ATT_HEADS)) * LOG2E for h in range(ATT_HEADS)]
    nsub = kblk // ATT_SUB
    rows8 = lambda t: t.reshape(t.shape[0] // SUBLANES, SUBLANES, t.shape[1])

    def logits(g, c):
        lg_ref[g] = jnp.dot(k_ref[rows_of(c), g * ATT_HD:(g + 1) * ATT_HD], qt_ref[g],
                            preferred_element_type=F32)

    def softmax_block(g):
        for r in range(ATT_REP):
            h = g * ATT_REP + r
            cs = slice(r * qb, (r + 1) * qb)
            mx = jnp.full((SUBLANES, qb), M_FLOOR, F32)
            for sb in range(nsub):
                rs = slice(sb * ATT_SUB, (sb + 1) * ATT_SUB)
                lgs = lg_ref[g, rs, cs] - slopes[h] * dm_ref[rs, :]
                lg_ref[g, rs, cs] = lgs
                mx = jnp.maximum(mx, jnp.max(rows8(lgs), axis=0))
            m_old = m_ref[g, :, cs]
            m_new = jnp.maximum(m_old, all_rows(mx, jnp.maximum, M_FLOOR, 1))
            sm = jnp.zeros((SUBLANES, qb), F32)
            for sb in range(nsub):
                rs = slice(sb * ATT_SUB, (sb + 1) * ATT_SUB)
                pr = jnp.exp2(rows8(lg_ref[g, rs, cs]) - m_new[None])
                sm = sm + jnp.sum(pr, axis=0)
                pt_ref[g, rs, cs] = pr.reshape(ATT_SUB, qb).astype(BF16)
            alpha = jnp.exp2(m_old - m_new)
            l_ref[g, :, cs] = alpha * l_ref[g, :, cs] + all_rows_sum(sm, 2)
            m_ref[g, :, cs] = m_new
            al_ref[g, :, cs] = alpha

    def weighted_values(g, c):
        pv = jnp.dot(vt_ref[g * ATT_HD:(g + 1) * ATT_HD, rows_of(c)], pt_ref[g],
                     preferred_element_type=F32)
        acc_ref[g] = (rows8(acc_ref[g]) * al_ref[g][None]).reshape(ATT_HD, ATT_REP * qb) + pv

    logits(0, 0)

    def att_body(c, carry):
        key = keys_ref[rows_of(c), :]
        dist = jnp.abs(pos - key_index(c)).astype(F32)
        dm_ref[...] = jnp.where(key >= thr_eff, dist, MASK_DIST)
        logits(1, c)
        softmax_block(0)
        weighted_values(0, c)
        logits(0, jnp.minimum(c + 1, nkc - 1))
        softmax_block(1)
        weighted_values(1, c)
        return carry
    lax.fori_loop(0, nkc, att_body, 0)

    for h in range(ATT_HEADS):
        g, r = divmod(h, ATT_REP)
        cs = slice(r * qb, (r + 1) * qb)
        ot = (rows8(acc_ref[g, :, cs]) / l_ref[g, :, cs][None]).reshape(ATT_HD, qb)
        o_ref[:, h * ATT_HD:(h + 1) * ATT_HD] = ot.T.astype(BF16)


def _dsa(q_ops, k_all, vt_all, ki_all, *, nb, nqb, qb, past, l_real, kblk, out_rows=None):
    l_pad = k_all.shape[1]
    topk = min(IDX_TOPK, l_real // 4)
    assert kblk >= topk and kblk % ATT_SUB == 0 and l_pad % kblk == 0
    (qa, qa_map), (qi, qi_map), (wi, wi_map) = q_ops
    kern = functools.partial(_dsa_kernel, past=past, l_real=l_real, kblk=kblk, topk=topk)
    return pl.pallas_call(
        kern,
        grid=(nb, nqb),
        in_specs=[pl.BlockSpec((qb, ATT_W), qa_map), pl.BlockSpec((qb, IDX_HEADS * IDX_DIM), qi_map),
                  pl.BlockSpec((qb, LANES), wi_map),
                  pl.BlockSpec((None, l_pad, ATT_KV * ATT_HD), lambda b, i: (b, 0, 0)),
                  pl.BlockSpec((None, ATT_KV * ATT_HD, l_pad), lambda b, i: (b, 0, 0)),
                  pl.BlockSpec((None, l_pad, LANES), lambda b, i: (b, 0, 0))],
        out_specs=pl.BlockSpec((qb, ATT_W), lambda b, i: (b * nqb + i, 0)),
        out_shape=jax.ShapeDtypeStruct((out_rows or nb * nqb * qb, ATT_W), BF16),
        scratch_shapes=[pltpu.VMEM((l_pad, 2 * LANES), BF16),
                        pltpu.VMEM((l_pad, qb), I32),
                        pltpu.VMEM((l_pad, qb), I16),
                        pltpu.VMEM((l_pad, qb), I16),
                        pltpu.VMEM((2 * LANES, IDX_HEADS * qb), BF16),
                        pltpu.VMEM((ATT_KV, ATT_HD, ATT_REP * qb), BF16),
                        pltpu.VMEM((ATT_KV, SUBLANES, ATT_REP * qb), F32),
                        pltpu.VMEM((ATT_KV, SUBLANES, ATT_REP * qb), F32),
                        pltpu.VMEM((ATT_KV, ATT_HD, ATT_REP * qb), F32),
                        pltpu.VMEM((ATT_KV, kblk, ATT_REP * qb), F32),
                        pltpu.VMEM((ATT_KV, kblk, ATT_REP * qb), BF16),
                        pltpu.VMEM((kblk, qb), F32),
                        pltpu.VMEM((ATT_KV, SUBLANES, ATT_REP * qb), F32),
                        pltpu.VMEM((3, SUBLANES, qb), F32)],
        compiler_params=_cparams(("parallel", "arbitrary")),
        name="dsa",
    )(qa, qi, wi, k_all, vt_all, ki_all)


def _rw_prep_kernel(r_ref, k_ref, v_ref, lr_ref, hr_ref, hk_ref, hv_ref, hlr_ref,
                    sr_ref, sk_ref, sv_ref, slr_ref, mr_ref, mk_ref, mv_ref, mlr_ref,
                    w0_ref, a0_ref, w2_ref, a2_ref, g2_ref,
                    ro_ref, ko_ref, vo_ref, wo_ref, ao_ref, go_ref, *, tiles_per_seq, time_minor):
    first_tile = (pl.program_id(0) % tiles_per_seq) == 0
    put = (lambda t: t.T) if time_minor else (lambda t: t)

    def mix(x_ref, halo_ref, shift_ref, mu_ref):
        x = x_ref[...]
        first = jnp.where(first_tile, shift_ref[...], halo_ref[SUBLANES - 1:SUBLANES, :])
        rows = lax.broadcasted_iota(I32, x.shape, 0)
        prev = jnp.where(rows == 0, first, pltpu.roll(x, 1, 0))
        return x + (prev - x) * mu_ref[...]

    ro_ref[...] = put(mix(r_ref, hr_ref, sr_ref, mr_ref))
    ko_ref[...] = put(mix(k_ref, hk_ref, sk_ref, mk_ref))
    vo_ref[...] = put(mix(v_ref, hv_ref, sv_ref, mv_ref))
    lr = mix(lr_ref, hlr_ref, slr_ref, mlr_ref)
    xw = lr[:, 0:128]
    xa = lr[:, 128:256]
    xg = lr[:, 256:512]
    wpre = w0_ref[...] + _dot3(jnp.tanh(xw), w2_ref[...])
    wo_ref[...] = put(jnp.exp(-RW_DECAY_SCALE * _sigmoid(wpre)))
    ao_ref[...] = put(_sigmoid(a0_ref[...] + _dot3(xa, a2_ref[...])))
    go_ref[...] = _dot3(_sigmoid(xg), g2_ref[...])


def _rw_prep(proj, row0, nb, seq, shifts, mus, w0, a0, w2p, a2p, g2, tm, out_rows=None):
    m = nb * seq
    tps = seq // tm
    rb0 = row0 // tm
    hb = tm // SUBLANES
    cr, ck, cv = _OFF["r"] // RW_W, _OFF["k"] // RW_W, _OFF["v"] // RW_W
    clr = _OFF["xw"] // 512

    def main(c):
        return lambda i: (rb0 + i, c)

    def halo(c):
        return lambda i: (jnp.maximum((rb0 + i) * hb - 1, 0), c)

    bidx = lambda i: (i // tps, 0, 0)
    fix = lambda i: (0, 0)
    wide = [pl.BlockSpec((tm, RW_W), main(cr)), pl.BlockSpec((tm, RW_W), main(ck)),
            pl.BlockSpec((tm, RW_W), main(cv)), pl.BlockSpec((tm, 512), main(clr)),
            pl.BlockSpec((SUBLANES, RW_W), halo(cr)), pl.BlockSpec((SUBLANES, RW_W), halo(ck)),
            pl.BlockSpec((SUBLANES, RW_W), halo(cv)), pl.BlockSpec((SUBLANES, 512), halo(clr))]
    sh = [pl.BlockSpec((None, 1, RW_W), bidx)] * 3 + [pl.BlockSpec((None, 1, 512), bidx)]
    mu = [pl.BlockSpec((1, RW_W), fix)] * 3 + [pl.BlockSpec((1, 512), fix)]
    par = [pl.BlockSpec((1, RW_W), fix), pl.BlockSpec((1, RW_W), fix),
           pl.BlockSpec((128, RW_W), fix), pl.BlockSpec((128, RW_W), fix),
           pl.BlockSpec((RW_G_R, RW_W), fix)]
    token_major = pl.BlockSpec((tm, RW_W), lambda i: (i, 0))
    time_minor = _relayout_ok(nb, seq) and tm % LANES == 0
    if time_minor:
        out = jax.ShapeDtypeStruct((nb, RW_W, seq), F32)
        seq_spec = pl.BlockSpec((None, RW_W, tm), lambda i: (i // tps, 0, i % tps))
    else:
        out = jax.ShapeDtypeStruct((m, RW_W), F32)
        seq_spec = token_major
    return pl.pallas_call(
        functools.partial(_rw_prep_kernel, tiles_per_seq=tps, time_minor=time_minor),
        grid=(m // tm,),
        in_specs=wide + sh + mu + par,
        out_specs=[seq_spec] * 5 + [token_major],
        out_shape=[out] * 5 + [jax.ShapeDtypeStruct((out_rows or m, RW_W), F32)],
        compiler_params=_cparams(("parallel",)),
        name="rwkv_prep",
    )(proj, proj, proj, proj, proj, proj, proj, proj, *shifts, *mus, w0, a0, w2p, a2p, g2)


def _rw_scan_kernel(r_ref, k_ref, v_ref, w_ref, a_ref, kkp_ref, kap_ref, rkp_ref,
                    lng_ref, lnb_ref, s0_ref, o_ref, st_ref, s_scr, tmp_scr):
    tb = pl.program_id(1)
    nsteps = r_ref.shape[0]

    @pl.when(tb == 0)
    def _():
        s_scr[...] = s0_ref[...]

    def step(t, carry):
        k = k_ref[t]
        a = a_ref[t]
        kk = k * kkp_ref[...]
        nrm = jnp.sqrt(jnp.sum(kk * kk, axis=0, keepdims=True))
        kk = kk / jnp.maximum(nrm, 1e-12)
        km = k * (1.0 + (a - 1.0) * kap_ref[...])
        tmp_scr[0] = kk
        tmp_scr[1] = kk * a
        tmp_scr[2] = km
        parts = [jnp.zeros((RW_HD, LANES), F32) for _ in range(2)]
        for kx in range(RW_HD):
            parts[kx % 2] = parts[kx % 2] + s_scr[kx] * tmp_scr[0, kx:kx + 1, :]
        sa = -(parts[0] + parts[1])
        v = v_ref[t]
        outs = [jnp.zeros((RW_HD, LANES), F32) for _ in range(2)]
        for kx in range(RW_HD):
            s_new = (s_scr[kx] * w_ref[t, kx:kx + 1, :] + sa * tmp_scr[1, kx:kx + 1, :]
                     + v * tmp_scr[2, kx:kx + 1, :])
            s_scr[kx] = s_new
            outs[kx % 2] = outs[kx % 2] + s_new * r_ref[t, kx:kx + 1, :]
        o = outs[0] + outs[1]
        mu = jnp.mean(o, axis=0, keepdims=True)
        oc = o - mu
        var = jnp.mean(oc * oc, axis=0, keepdims=True)
        on = oc * lax.rsqrt(var + RW_GN_EPS) * lng_ref[...] + lnb_ref[...]
        bonus = jnp.sum(r_ref[t] * tmp_scr[2] * rkp_ref[...], axis=0, keepdims=True) * v
        o_ref[t] = on + bonus
        return carry
    lax.fori_loop(0, nsteps, step, 0)

    @pl.when(tb == pl.num_programs(1) - 1)
    def _():
        st_ref[...] = s_scr[...]


def _rw_scan(r, k, v, w, a, params, s0, tb):
    seq, _, pairs = r.shape
    assert pairs % LANES == 0, "batch * RW_HEADS must fill whole lane groups"
    seqb = pl.BlockSpec((tb, RW_HD, LANES), lambda g, t: (t, 0, g))
    par = pl.BlockSpec((RW_HD, LANES), lambda g, t: (0, 0))
    st = pl.BlockSpec((RW_HD, RW_HD, LANES), lambda g, t: (0, 0, g))
    return pl.pallas_call(
        _rw_scan_kernel,
        grid=(pairs // LANES, seq // tb),
        in_specs=[seqb] * 5 + [par] * 5 + [st],
        out_specs=[seqb, st],
        out_shape=[jax.ShapeDtypeStruct((seq, RW_HD, pairs), F32),
                   jax.ShapeDtypeStruct((RW_HD, RW_HD, pairs), F32)],
        scratch_shapes=[pltpu.VMEM((RW_HD, RW_HD, LANES), F32),
                        pltpu.VMEM((3, RW_HD, LANES), F32)],
        compiler_params=_cparams(("parallel", "arbitrary")),
        name="rwkv_scan",
    )(r, k, v, w, a, *params, s0)


GLA_SUB = 16


def _gla_kernel(q_ref, k_ref, v_ref, gl_ref, og_ref, w2_ref, gb_ref, ng_ref, s0_ref,
                o_ref, st_ref, s_scr, la_scr, att_scr):
    tblk = pl.program_id(1)
    tc = q_ref.shape[0]

    @pl.when(tblk == 0)
    def _():
        for h in range(GLA_HEADS):
            s_scr[h] = s0_ref[h].T

    la_scr[...] = _log_sigmoid(_dot3(gl_ref[...], w2_ref[...]) + gb_ref[...]) * (1.0 / GLA_GATE_NORM)
    rows = lax.broadcasted_iota(I32, (CHUNK, GLA_DK), 0)
    lane = lax.broadcasted_iota(I32, (GLA_SUB, CHUNK), 1)
    jrow = lax.broadcasted_iota(I32, (GLA_SUB, GLA_DK), 0)
    nsub = CHUNK // GLA_SUB

    def head_chunk(h, rs):
        ks = slice(h * GLA_DK, (h + 1) * GLA_DK)
        vs = slice(h * GLA_DV, (h + 1) * GLA_DV)
        b = la_scr[rs, ks]
        sh = 1
        while sh < CHUNK:
            b = b + jnp.where(rows >= sh, pltpu.roll(b, sh, 0), 0.0)
            sh *= 2
        q = q_ref[rs, ks] * (GLA_DK ** -0.5)
        k = k_ref[rs, ks]
        v = v_ref[rs, vs]
        s_old = s_scr[h]
        inter = _nt_dot(q * jnp.exp(b), s_old)
        b_last = b[CHUNK - 1:CHUNK, :]

        for jb in range(nsub):
            js = slice(jb * GLA_SUB, (jb + 1) * GLA_SUB)
            bj = b[js]
            kj = k[js]
            blk = jnp.zeros((GLA_SUB, CHUNK), F32)
            for ii in range(GLA_SUB):
                i_row = jb * GLA_SUB + ii
                e = jnp.exp(jnp.minimum(b[i_row:i_row + 1, :] - bj, 0.0))
                val = jnp.where(jrow <= ii, q[i_row:i_row + 1, :] * kj * e, 0.0)
                col = jnp.sum(val, axis=1, keepdims=True)
                blk = jnp.where(lane == i_row, col, blk)
            if jb + 1 < nsub:
                i0 = (jb + 1) * GLA_SUB
                be = b[i0 - 1:i0, :]
                kt = kj * jnp.exp(be - bj)
                later = rows >= i0
                qt = jnp.where(later, q * jnp.exp(jnp.where(later, b - be, 0.0)), 0.0)
                blk = blk + _nt_dot(kt, qt)
            att_scr[h, js, :] = blk
        o = inter + _tn_dot(att_scr[h], v)
        khat = k * jnp.exp(b_last - b)
        s_scr[h] = s_old * jnp.exp(b_last) + _tn_dot(v, khat)
        o = o * lax.rsqrt(jnp.mean(o * o, -1, keepdims=True) + GLA_NORM_EPS) * ng_ref[:, vs]
        og = og_ref[rs, vs]
        o_ref[rs, vs] = (o * (og * _sigmoid(og))).astype(BF16)

    def chunk_body(c, carry):
        rs = pl.ds(pl.multiple_of(c * CHUNK, CHUNK), CHUNK)
        for h in range(GLA_HEADS):
            head_chunk(h, rs)
        return carry
    lax.fori_loop(0, tc // CHUNK, chunk_body, 0)

    @pl.when(tblk == pl.num_programs(1) - 1)
    def _():
        for h in range(GLA_HEADS):
            st_ref[h] = s_scr[h].T


def _gla(proj, tail, row0, nb, seq, w2p, gb, ng, s0, tc, out_rows=None):
    ntb = seq // tc
    rb0 = row0 // tc
    cq, ck = _OFF["qc"] // GLA_KW, _OFF["kc"] // GLA_KW
    cv, cg = _OFF["vc"] // GLA_VW, _OFF["og"] // GLA_VW
    cgl = (_OFF["gl"] - _OFF["ka"]) // LANES

    def rows(c0):
        return lambda b, t: (rb0 + b * ntb + t, c0)

    fix = lambda b, t: (0, 0)
    st = pl.BlockSpec((None, GLA_HEADS, GLA_DK, GLA_DV), lambda b, t: (b, 0, 0, 0))
    return pl.pallas_call(
        _gla_kernel,
        grid=(nb, ntb),
        in_specs=[pl.BlockSpec((tc, GLA_KW), rows(cq)), pl.BlockSpec((tc, GLA_KW), rows(ck)),
                  pl.BlockSpec((tc, GLA_VW), rows(cv)), pl.BlockSpec((tc, LANES), rows(cgl)),
                  pl.BlockSpec((tc, GLA_VW), rows(cg)),
                  pl.BlockSpec((LANES, GLA_KW), fix), pl.BlockSpec((1, GLA_KW), fix),
                  pl.BlockSpec((1, GLA_VW), fix), st],
        out_specs=[pl.BlockSpec((tc, GLA_VW), lambda b, t: (b * ntb + t, 0)), st],
        out_shape=[jax.ShapeDtypeStruct((out_rows or nb * seq, GLA_VW), BF16),
                   jax.ShapeDtypeStruct((nb, GLA_HEADS, GLA_DK, GLA_DV), F32)],
        scratch_shapes=[pltpu.VMEM((GLA_HEADS, GLA_DV, GLA_DK), F32), pltpu.VMEM((tc, GLA_KW), F32),
                        pltpu.VMEM((GLA_HEADS, CHUNK, CHUNK), F32)],
        compiler_params=_cparams(("parallel", "arbitrary")),
        name="gla",
    )(proj, proj, proj, tail, proj, w2p, gb, ng, s0)


_MAIN_SEGS = tuple(sg for sg in _SEGS if _OFF[sg[0]] < _OFF["ka"])
_TAIL_SEGS = tuple(sg for sg in _SEGS if _OFF[sg[0]] >= _OFF["ka"])


def _pack_cols(w, segs=_SEGS, dtype=None):
    dtype = dtype or w.dtype
    parts = []
    for _, s, wd, p in segs:
        parts.append(w[..., s:s + wd].astype(dtype))
        if p > wd:
            parts.append(jnp.zeros(w.shape[:-1] + (p - wd,), dtype))
    return jnp.concatenate(parts, axis=-1)


def _pack_rw(vec):
    r, k, v = vec[..., 0:1024], vec[..., 1024:2048], vec[..., 2048:3072]
    z = jnp.zeros(vec.shape[:-1] + (32,), vec.dtype)
    lr = jnp.concatenate([vec[..., 3072:3168], z, vec[..., 3168:3264], z, vec[..., 3264:3520]], -1)
    return r, k, v, lr


def _pad_rows(w, n):
    return jnp.concatenate([w, jnp.zeros((n - w.shape[0],) + w.shape[1:], w.dtype)], 0)


RELAYOUT_T = 128


def _to_scan_kernel(y_ref, o_ref):
    nbat = y_ref.shape[0]
    for n in range(RW_HD):
        z = jnp.concatenate([y_ref[b, pl.ds(n, RW_HEADS, stride=RW_HD), :] for b in range(nbat)], axis=0)
        o_ref[:, n, :] = z.T


def _from_scan_kernel(o_ref, x_ref, y_scr):
    nbat = x_ref.shape[0]
    for n in range(RW_HD):
        zt = o_ref[:, n, :].T
        for b in range(nbat):
            y_scr[b, pl.ds(n, RW_HEADS, stride=RW_HD), :] = zt[b * RW_HEADS:(b + 1) * RW_HEADS]
    for b in range(nbat):
        for c in range(RW_W // LANES):
            x_ref[b, :, c * LANES:(c + 1) * LANES] = y_scr[b, c * LANES:(c + 1) * LANES, :].T


def _relayout_ok(nb, seq):
    return nb * RW_HEADS == LANES and seq % RELAYOUT_T == 0


def _to_scan(x, nb, seq):
    if x.ndim == 2:
        return jnp.transpose(x.reshape(nb, seq, RW_HEADS, RW_HD), (1, 3, 0, 2)).reshape(seq, RW_HD, nb * RW_HEADS)
    return pl.pallas_call(
        _to_scan_kernel,
        grid=(seq // RELAYOUT_T,),
        in_specs=[pl.BlockSpec((nb, RW_W, RELAYOUT_T), lambda t: (0, 0, t))],
        out_specs=pl.BlockSpec((RELAYOUT_T, RW_HD, LANES), lambda t: (t, 0, 0)),
        out_shape=jax.ShapeDtypeStruct((seq, RW_HD, LANES), F32),
        compiler_params=_cparams(("parallel",)),
        name="to_scan_layout",
    )(x)


def _from_scan(o, nb, seq):
    if not _relayout_ok(nb, seq):
        return jnp.transpose(o.reshape(seq, RW_HD, nb, RW_HEADS), (2, 0, 3, 1)).reshape(nb * seq, RW_W)
    return pl.pallas_call(
        _from_scan_kernel,
        grid=(seq // RELAYOUT_T,),
        in_specs=[pl.BlockSpec((RELAYOUT_T, RW_HD, LANES), lambda t: (t, 0, 0))],
        out_specs=pl.BlockSpec((nb, RELAYOUT_T, RW_W), lambda t: (0, t, 0)),
        out_shape=jax.ShapeDtypeStruct((nb, seq, RW_W), F32),
        scratch_shapes=[pltpu.VMEM((nb, RW_W, RELAYOUT_T), F32)],
        compiler_params=_cparams(("parallel",)),
        name="from_scan_layout",
    )(o).reshape(nb * seq, RW_W)


def _group_mixers(proj, tail, row0, nb, seq, caches, lw, out_rows=None):
    m = nb * seq
    cache_k, cache_v, cache_ki, s_rw, s_shift, s_gla = caches
    past = 0 if cache_k is None else cache_k.shape[1]
    l_real = past + seq
    t0 = _OFF["ka"]
    grp = tail[row0:row0 + m].reshape(nb, seq, -1)

    ka = grp[..., _OFF["ka"] - t0:_OFF["ka"] - t0 + 256]
    va = grp[..., _OFF["va"] - t0:_OFF["va"] - t0 + 256]
    ki = grp[..., _OFF["ki"] - t0:_OFF["ki"] - t0 + LANES]
    if past:
        k_all = jnp.concatenate([cache_k.reshape(nb, past, 256), ka], 1)
        v_all = jnp.concatenate([cache_v.reshape(nb, past, 256), va], 1)
        cki = jnp.concatenate([cache_ki, jnp.zeros((nb, past, LANES - IDX_DIM), F32)], -1)
        ki_all = jnp.concatenate([cki, ki], 1)
    else:
        k_all, v_all, ki_all = ka, va, ki
    kblk = 512 if l_real % 512 == 0 else 384
    l_pad = -(-l_real // kblk) * kblk
    if l_pad > l_real:
        padk = lambda t: jnp.concatenate([t, jnp.zeros((nb, l_pad - l_real, t.shape[-1]), t.dtype)], 1)
        k_all, v_all, ki_all = padk(k_all), padk(v_all), padk(ki_all)
    k_bf = k_all.astype(BF16)
    vt_bf = jnp.swapaxes(v_all, 1, 2).astype(BF16)
    qb = DSA_QB if seq % DSA_QB == 0 else LANES
    if seq % qb == 0:
        nqb = seq // qb
        rb0 = row0 // qb
        mk = lambda c: (lambda b, i: (rb0 + b * nqb + i, c))
        q_ops = ((proj, mk(_OFF["qa"] // ATT_W)), (proj, mk(_OFF["qi"] // 1024)),
                 (tail, mk((_OFF["wi"] - t0) // LANES)))
        o_a = _dsa(q_ops, k_bf, vt_bf, ki_all, nb=nb, nqb=nqb, qb=qb, past=past, l_real=l_real, kblk=kblk,
                   out_rows=out_rows)
    else:
        def padq(t):
            return jnp.concatenate([t, jnp.zeros((nb, qb - seq, t.shape[-1]), F32)], 1).reshape(nb * qb, -1)
        qrows = proj[row0:row0 + m, 0:_OFF["vc"]].reshape(nb, seq, -1)
        mk = lambda b, i: (b, 0)
        q_ops = ((padq(qrows[..., 0:ATT_W]), mk), (padq(qrows[..., ATT_W:]), mk),
                 (padq(grp[..., _OFF["wi"] - t0:_OFF["wi"] - t0 + LANES]), mk))
        o_a = _dsa(q_ops, k_bf, vt_bf, ki_all, nb=nb, nqb=1, qb=qb, past=past, l_real=l_real, kblk=kblk)
        o_a = o_a.reshape(nb, qb, ATT_W)[:, :seq].reshape(m, ATT_W)

    if s_shift is None:
        s_shift = jnp.zeros((nb, RW_PROJ), F32)
    shifts = [t.reshape(nb, 1, -1) for t in _pack_rw(s_shift)]
    tm = 256 if seq % 256 == 0 else seq
    r, k, v, w, a, g = _rw_prep(proj, row0, nb, seq, shifts, lw["rw_mus"], lw["rw_w0"], lw["rw_a0"],
                                lw["rw_w2p"], lw["rw_a2p"], lw["rw_g2"], tm, out_rows)
    if s_rw is None:
        s0 = jnp.zeros((RW_HD, RW_HD, nb * RW_HEADS), F32)
    else:
        s0 = jnp.transpose(s_rw, (3, 2, 0, 1)).reshape(RW_HD, RW_HD, nb * RW_HEADS)
    tile = lambda p: jnp.tile(p.reshape(RW_HEADS, RW_HD).T, (1, LANES // RW_HEADS))
    params = [tile(lw[n]) for n in ("rw_kk", "rw_ka", "rw_rk", "rw_ln_g", "rw_ln_b")]
    tb = 64 if seq % 64 == 0 else seq
    o_b, s_t = _rw_scan(*[_to_scan(t, nb, seq) for t in (r, k, v, w, a)], params, s0, tb)
    o_b = _from_scan(o_b, nb, seq)
    s_rw_new = jnp.transpose(s_t.reshape(RW_HD, RW_HD, nb, RW_HEADS), (2, 3, 1, 0))
    last = jnp.take(proj, row0 + seq - 1 + seq * jnp.arange(nb), axis=0)[:, _OFF["r"]:_OFF["qc"]]
    base = _OFF["r"]
    shift_new = jnp.concatenate(
        [last[:, 0:3072], last[:, _OFF["xw"] - base:_OFF["xw"] - base + 96],
         last[:, _OFF["xa"] - base:_OFF["xa"] - base + 96], last[:, _OFF["xg"] - base:]], -1)

    if s_gla is None:
        s_gla = jnp.zeros((nb, GLA_HEADS, GLA_DK, GLA_DV), F32)
    tc = 512 if seq % 512 == 0 else seq
    o_c, s_gla_new = _gla(proj, tail, row0, nb, seq, lw["gla_w2p"], lw["gla_b"], lw["gla_ng"], s_gla, tc,
                          out_rows)

    new_state = (ka.reshape(nb, seq, ATT_KV, ATT_HD), va.reshape(nb, seq, ATT_KV, ATT_HD),
                 ki[..., :IDX_DIM], s_rw_new, shift_new, s_gla_new)
    return o_a, o_b, g, o_c, new_state


def kernel(x_prompt, x_sample, cache_k, cache_v, cache_kidx, state_rwkv, state_rwkv_shift, state_gla,
           w_in, rw_mu, rw_w0, rw_w2, rw_a0, rw_a2, rw_g2, rw_kk, rw_ka, rw_rk, rw_ln_g, rw_ln_b,
           gla_w2, gla_b, gla_norm_g, w_branch_a, w_branch_b, w_branch_c, w_out,
           ln1_g, ln1_b, ln2_g, ln2_b, ffn_w1, ffn_w3, ffn_w2):
    bp, tp, _ = x_prompt.shape
    bs, ts, _ = x_sample.shape
    mp, ms = bp * tp, bs * ts
    x = jnp.concatenate([x_prompt.reshape(mp, D_MODEL), x_sample.reshape(ms, D_MODEL)], 0)
    xb = x.astype(BF16)
    m = mp + ms
    tm_big = 1024 if m % 1024 == 0 else 256
    tm_mid = 512 if m % 512 == 0 else 256

    new_p = [[] for _ in range(6)]
    new_s = [[] for _ in range(6)]
    for l in range(DEPTH):
        lw = {
            "rw_mus": [t.reshape(1, -1) for t in _pack_rw(rw_mu[l])],
            "rw_w0": rw_w0[l].reshape(1, RW_W), "rw_a0": rw_a0[l].reshape(1, RW_W),
            "rw_w2p": _pad_rows(rw_w2[l], 128), "rw_a2p": _pad_rows(rw_a2[l], 128), "rw_g2": rw_g2[l],
            "rw_kk": rw_kk[l], "rw_ka": rw_ka[l], "rw_rk": rw_rk[l],
            "rw_ln_g": rw_ln_g[l], "rw_ln_b": rw_ln_b[l],
            "gla_w2p": _pad_rows(gla_w2[l], LANES), "gla_b": gla_b[l].reshape(1, GLA_KW),
            "gla_ng": gla_norm_g[l].reshape(1, GLA_VW),
        }
        proj = _matmul(xb, _pack_cols(w_in[l], _MAIN_SEGS, BF16), tm_big, 1024)
        tail = _matmul(xb, _pack_cols(w_in[l], _TAIL_SEGS, BF16), tm_big, PROJ_W - _OFF["ka"])
        oa_p, ob_p, g_p, oc_p, st_p = _group_mixers(proj, tail, 0, bp, tp, (None,) * 6, lw, out_rows=m)
        oa_s, ob_s, g_s, oc_s, st_s = _group_mixers(
            proj, tail, mp, bs, ts,
            (cache_k[l], cache_v[l], cache_kidx[l], state_rwkv[l], state_rwkv_shift[l], state_gla[l]), lw)
        put = lambda full, part: lax.dynamic_update_slice(full, part, (mp, 0))
        merged = _merge(put(oa_p, oa_s), jnp.concatenate([ob_p, ob_s], 0), put(g_p, g_s), put(oc_p, oc_s), proj,
                        w_branch_a[l].astype(BF16), w_branch_b[l].astype(BF16),
                        w_branch_c[l].astype(BF16), 256, D_MODEL)
        h, hb = _hproj(merged, w_out[l].astype(BF16), x, ln1_g[l].reshape(1, -1),
                       ln1_b[l].reshape(1, -1), tm_mid)
        x, xb = _ffn(hb, h, ffn_w1[l].astype(BF16), ffn_w3[l].astype(BF16), ffn_w2[l].astype(BF16),
                     ln2_g[l].reshape(1, -1), ln2_b[l].reshape(1, -1), tm_mid, 512)
        for i in range(6):
            new_p[i].append(st_p[i])
            new_s[i].append(st_s[i])
    sp = [jnp.stack(a) for a in new_p]
    ss = [jnp.stack(a) for a in new_s]
    yp = x[:mp].reshape(bp, tp, D_MODEL)
    ys = x[mp:].reshape(bs, ts, D_MODEL)
    return (yp, ys, sp[0], sp[1], sp[2], sp[3], sp[4], sp[5], ss[0], ss[1], ss[2], ss[3], ss[4], ss[5])
```

```python
import functools

import jax
import jax.numpy as jnp
from jax import lax
from jax.experimental import pallas as pl
from jax.experimental.pallas import tpu as pltpu

F32 = jnp.float32
BF16 = jnp.bfloat16
I32 = jnp.int32
I16 = jnp.int16

D_MODEL = 2048
DEPTH = 4
CHUNK = 64
Q_BLOCK = 128
ATT_HEADS = 8
ATT_HD = 128
ATT_KV = 2
ATT_REP = ATT_HEADS // ATT_KV
ATT_W = ATT_HEADS * ATT_HD
IDX_HEADS = 16
IDX_DIM = 64
IDX_TOPK = 256
IDX_SCALE = (IDX_DIM * IDX_HEADS) ** -0.5
RW_HEADS = 16
RW_HD = 64
RW_W = RW_HEADS * RW_HD
RW_DECAY_R = 96
RW_A_R = 96
RW_G_R = 256
RW_PROJ = 3 * RW_W + RW_DECAY_R + RW_A_R + RW_G_R
RW_DECAY_SCALE = 0.606531
RW_GN_EPS = 64e-5
GLA_HEADS = 4
GLA_DK = 128
GLA_DV = 256
GLA_KW = GLA_HEADS * GLA_DK
GLA_VW = GLA_HEADS * GLA_DV
GLA_GATE_R = 16
GLA_GATE_NORM = 16.0
GLA_NORM_EPS = 1e-5
FFN_HIDDEN = -(-8 * D_MODEL // (3 * 256)) * 256
DN_ALPHA = (2 * DEPTH) ** 0.25
LN_EPS = 1e-5
LOG2E = 1.4426950408889634

LANES = 128
SUBLANES = 8
PACKED_ROWS = 16
VMEM_LIMIT = 56 * 1024 * 1024

_SEGS = (
    ("qa", 0, 1024, 1024), ("qi", 1536, 1024, 1024), ("vc", 7184, 1024, 1024),
    ("og", 8224, 1024, 1024), ("gates", 9248, 6144, 6144),
    ("r", 2640, 1024, 1024), ("k", 3664, 1024, 1024), ("v", 4688, 1024, 1024),
    ("xw", 5712, 96, 128), ("xa", 5808, 96, 128), ("xg", 5904, 256, 256),
    ("qc", 6160, 512, 512), ("kc", 6672, 512, 512),
    ("ka", 1024, 256, 256), ("va", 1280, 256, 256),
    ("ki", 2560, 64, 128), ("wi", 2624, 16, 128), ("gl", 8208, 16, 128),
)
_OFF = {}
_o = 0
for _n, _s, _w, _p in _SEGS:
    _OFF[_n] = _o
    _o += _p
PROJ_W = _o
KEY_NEG_INF = -0x7F800000
MASK_DIST = 1e32
M_FLOOR = -1e29


def _cparams(sem):
    return pltpu.CompilerParams(dimension_semantics=sem, vmem_limit_bytes=VMEM_LIMIT)


def _split2(x):
    hi = x.astype(BF16)
    lo = (x - hi.astype(F32)).astype(BF16)
    return hi, lo


def _dot3(a, b):
    ah, al = _split2(a)
    bh, bl = _split2(b)
    d = functools.partial(jnp.dot, preferred_element_type=F32)
    return d(ah, bh) + (d(al, bh) + d(ah, bl))


def _nt_dot(a, b):
    return lax.dot_general(a.astype(BF16), b.astype(BF16), (((1,), (1,)), ((), ())),
                           preferred_element_type=F32)


def _tn_dot(a, b):
    return lax.dot_general(a.astype(BF16), b.astype(BF16), (((0,), (0,)), ((), ())),
                           preferred_element_type=F32)


def _sigmoid(x):
    return 1.0 / (1.0 + jnp.exp(-x))


def _log_sigmoid(x):
    return jnp.minimum(x, 0.0) - jnp.log(1.0 + jnp.exp(-jnp.abs(x)))


def _layer_norm(x, g, b):
    mu = jnp.mean(x, -1, keepdims=True)
    xc = x - mu
    var = jnp.mean(xc * xc, -1, keepdims=True)
    return xc * lax.rsqrt(var + LN_EPS) * g + b


def _mm_kernel(a_ref, b_ref, o_ref):
    o_ref[...] = jnp.dot(a_ref[...], b_ref[...], preferred_element_type=F32)


def _matmul(a, b, tm, tn):
    m, k = a.shape
    n = b.shape[1]
    return pl.pallas_call(
        _mm_kernel,
        grid=(m // tm, pl.cdiv(n, tn)),
        in_specs=[pl.BlockSpec((tm, k), lambda i, j: (i, 0)),
                  pl.BlockSpec((k, tn), lambda i, j: (0, j))],
        out_specs=pl.BlockSpec((tm, tn), lambda i, j: (i, j)),
        out_shape=jax.ShapeDtypeStruct((m, n), F32),
        compiler_params=_cparams(("parallel", "arbitrary")),
        name="in_proj",
    )(a, b)


def _merge_kernel(oa_ref, ob_ref, g_ref, oc_ref, ga_ref, gb_ref, gc_ref,
                  wa_ref, wb_ref, wc_ref, o_ref):
    ob = (ob_ref[...] * g_ref[...]).astype(BF16)
    d = functools.partial(jnp.dot, preferred_element_type=F32)
    acc = _sigmoid(ga_ref[...]) * d(oa_ref[...], wa_ref[...])
    acc += _sigmoid(gb_ref[...]) * d(ob, wb_ref[...])
    acc += _sigmoid(gc_ref[...]) * d(oc_ref[...], wc_ref[...])
    o_ref[...] = acc.astype(BF16)


def _merge(oa, ob, g, oc, proj, wa, wb, wc, tm, tn):
    m = oa.shape[0]
    gblk = _OFF["gates"] // tn
    nj = D_MODEL // tn
    row = lambda i, j: (i, 0)
    wspec = pl.BlockSpec((ATT_W, tn), lambda i, j: (0, j))
    return pl.pallas_call(
        _merge_kernel,
        grid=(m // tm, nj),
        in_specs=[pl.BlockSpec((tm, ATT_W), row), pl.BlockSpec((tm, RW_W), row),
                  pl.BlockSpec((tm, RW_W), row), pl.BlockSpec((tm, GLA_VW), row),
                  pl.BlockSpec((tm, tn), lambda i, j: (i, gblk + j)),
                  pl.BlockSpec((tm, tn), lambda i, j: (i, gblk + nj + j)),
                  pl.BlockSpec((tm, tn), lambda i, j: (i, gblk + 2 * nj + j)),
                  wspec, wspec, wspec],
        out_specs=pl.BlockSpec((tm, tn), lambda i, j: (i, j)),
        out_shape=jax.ShapeDtypeStruct((m, D_MODEL), BF16),
        compiler_params=_cparams(("parallel", "arbitrary")),
        name="merge",
    )(oa, ob, g, oc, proj, proj, proj, wa, wb, wc)


def _hproj_kernel(m_ref, w_ref, x_ref, g_ref, b_ref, h_ref, hb_ref):
    z = DN_ALPHA * x_ref[...] + jnp.dot(m_ref[...], w_ref[...], preferred_element_type=F32)
    h = _layer_norm(z, g_ref[...], b_ref[...])
    h_ref[...] = h
    hb_ref[...] = h.astype(BF16)


def _hproj(merged, w_out, x, g, b, tm):
    m = x.shape[0]
    row = lambda i: (i, 0)
    fix = lambda i: (0, 0)
    return pl.pallas_call(
        _hproj_kernel,
        grid=(m // tm,),
        in_specs=[pl.BlockSpec((tm, D_MODEL), row), pl.BlockSpec((D_MODEL, D_MODEL), fix),
                  pl.BlockSpec((tm, D_MODEL), row), pl.BlockSpec((1, D_MODEL), fix),
                  pl.BlockSpec((1, D_MODEL), fix)],
        out_specs=[pl.BlockSpec((tm, D_MODEL), row), pl.BlockSpec((tm, D_MODEL), row)],
        out_shape=[jax.ShapeDtypeStruct((m, D_MODEL), F32),
                   jax.ShapeDtypeStruct((m, D_MODEL), BF16)],
        compiler_params=_cparams(("parallel",)),
        name="out_proj_ln",
    )(merged, w_out, x, g, b)


def _ffn_kernel(hb_ref, h_ref, w1_ref, w3_ref, w2_ref, g_ref, b_ref, y_ref, yb_ref, acc_ref):
    j = pl.program_id(1)

    @pl.when(j == 0)
    def _():
        acc_ref[...] = jnp.zeros_like(acc_ref)

    hb = hb_ref[...]
    u1 = jnp.dot(hb, w1_ref[...], preferred_element_type=F32)
    u3 = jnp.dot(hb, w3_ref[...], preferred_element_type=F32)
    u = (u1 * _sigmoid(u1) * u3).astype(BF16)
    acc_ref[...] += jnp.dot(u, w2_ref[...], preferred_element_type=F32)

    @pl.when(j == pl.num_programs(1) - 1)
    def _():
        y = _layer_norm(DN_ALPHA * h_ref[...] + acc_ref[...], g_ref[...], b_ref[...])
        y_ref[...] = y
        yb_ref[...] = y.astype(BF16)


def _ffn(hb, h, w1, w3, w2, g, b, tm, th):
    m = h.shape[0]
    row = lambda i, j: (i, 0)
    fix = lambda i, j: (0, 0)
    return pl.pallas_call(
        _ffn_kernel,
        grid=(m // tm, FFN_HIDDEN // th),
        in_specs=[pl.BlockSpec((tm, D_MODEL), row), pl.BlockSpec((tm, D_MODEL), row),
                  pl.BlockSpec((D_MODEL, th), lambda i, j: (0, j)),
                  pl.BlockSpec((D_MODEL, th), lambda i, j: (0, j)),
                  pl.BlockSpec((th, D_MODEL), lambda i, j: (j, 0)),
                  pl.BlockSpec((1, D_MODEL), fix), pl.BlockSpec((1, D_MODEL), fix)],
        out_specs=[pl.BlockSpec((tm, D_MODEL), row), pl.BlockSpec((tm, D_MODEL), row)],
        out_shape=[jax.ShapeDtypeStruct((m, D_MODEL), F32),
                   jax.ShapeDtypeStruct((m, D_MODEL), BF16)],
        scratch_shapes=[pltpu.VMEM((tm, D_MODEL), F32)],
        compiler_params=_cparams(("parallel", "arbitrary")),
        name="ffn_ln",
    )(hb, h, w1, w3, w2, g, b)


ATT_SUB = 128
DSA_QB = 256


def _dsa_kernel(qa_ref, qi_ref, wi_ref, k_ref, vt_ref, ki_ref, o_ref,
                kcat_ref, keys_ref, hi_ref, lo_ref, wq_ref, qt_ref, m_ref, l_ref, acc_ref,
                lg_ref, pt_ref, dm_ref, al_ref, red_ref, *, past, l_real, kblk, topk):
    qb = qa_ref.shape[0]
    l_pad = ki_ref.shape[0]
    i = pl.program_id(1)
    kf = float(topk)

    @pl.when(i == 0)
    def _():
        def prep(c, carry):
            r0 = pl.multiple_of(c * kblk, kblk)
            x = ki_ref[pl.ds(r0, kblk), :]
            hi = x.astype(BF16)
            lo = x - hi.astype(F32)
            a = (hi.astype(F32) + pltpu.roll(lo, IDX_DIM, 1)).astype(BF16)
            kcat_ref[pl.ds(r0, kblk), 0:LANES] = a
            kcat_ref[pl.ds(r0, kblk), LANES:2 * LANES] = hi
            return carry
        lax.fori_loop(0, l_pad // kblk, prep, 0)

    qit = qi_ref[...].T
    qhi = qit.astype(BF16)
    qlo = (qit - qhi.astype(F32)).astype(BF16)
    zero = jnp.zeros((IDX_DIM, qb), BF16)
    for h in range(IDX_HEADS):
        hs = slice(h * IDX_DIM, (h + 1) * IDX_DIM)
        cs = slice(h * qb, (h + 1) * qb)
        wq_ref[0:64, cs] = qhi[hs]
        wq_ref[64:128, cs] = qhi[hs]
        wq_ref[128:192, cs] = qlo[hs]
        wq_ref[192:256, cs] = zero
    wit = wi_ref[...].T
    qat = (qa_ref[...] * (ATT_HD ** -0.5 * LOG2E)).T.astype(BF16)
    for h in range(ATT_HEADS):
        g, r = divmod(h, ATT_REP)
        qt_ref[g, :, r * qb:(r + 1) * qb] = qat[h * ATT_HD:(h + 1) * ATT_HD]

    q0 = past + i * qb
    pos = q0 + lax.broadcasted_iota(I32, (1, qb), 1)
    pos_chunk = lax.shift_right_logical(pos, 6)
    adm_keys = jnp.minimum(l_real, (lax.shift_right_logical(q0 + qb - 1, 6) + 1) * CHUNK)
    nkc = lax.div(adm_keys + (kblk - 1), kblk)

    def rows_of(c):
        return pl.ds(pl.multiple_of(c * kblk, kblk), kblk)

    def key_index(c):
        return c * kblk + lax.broadcasted_iota(I32, (kblk, qb), 0)

    def score_body(c, carry):
        kc = kcat_ref[rows_of(c), :]
        acc = jnp.zeros((kblk, qb), F32)
        for hg in range(IDX_HEADS // 4):
            rel = jnp.dot(kc, wq_ref[:, hg * 4 * qb:(hg + 1) * 4 * qb],
                          preferred_element_type=F32)
            for hh in range(4):
                h = hg * 4 + hh
                acc = acc + jnp.maximum(rel[:, hh * qb:(hh + 1) * qb], 0.0) * wit[h:h + 1, :]
        score = acc * IDX_SCALE
        s_idx = key_index(c)
        adm = (lax.shift_right_logical(s_idx, 6) <= pos_chunk) & (s_idx < l_real)
        score = jnp.where(adm, score, -jnp.inf)
        bits = pltpu.bitcast(score, I32)
        key = jnp.where(bits < 0, -(bits & 0x7FFFFFFF), bits)
        keys_ref[rows_of(c), :] = key
        hi_ref[rows_of(c), :] = lax.shift_right_arithmetic(key, 16).astype(I16)
        return carry
    lax.fori_loop(0, nkc, score_body, 0)

    n_acc = 4
    one16 = jnp.ones((), I16)
    zero16 = jnp.zeros((), I16)

    def all_rows(x, op, init, slot):
        red_ref[slot] = x
        cols = []
        for c0 in range(0, qb, LANES):
            even = jnp.full((SUBLANES, LANES), init, F32)
            odd = jnp.full((SUBLANES, LANES), init, F32)
            for r in range(0, SUBLANES, 2):
                even = op(even, red_ref[slot, r:r + 1, c0:c0 + LANES])
                odd = op(odd, red_ref[slot, r + 1:r + 2, c0:c0 + LANES])
            cols.append(op(even, odd))
        return jnp.concatenate(cols, axis=1)

    def all_rows_sum(x, slot=0):
        return all_rows(x, jnp.add, 0.0, slot)

    def count16(ref, cand):
        c16 = jnp.concatenate([cand, cand], axis=0).astype(I16)[None]

        def body(c, accs):
            blk = ref[rows_of(c), :].reshape(kblk // PACKED_ROWS, PACKED_ROWS, qb)
            ind = jnp.where(blk >= c16, one16, zero16)
            accs = list(accs)
            for s in range(kblk // PACKED_ROWS):
                accs[s % n_acc] = accs[s % n_acc] + ind[s]
            return tuple(accs)
        accs = lax.fori_loop(0, nkc, body, (jnp.zeros((PACKED_ROWS, qb), I16),) * n_acc)
        tot = (accs[0].astype(I32) + accs[1].astype(I32)) + (accs[2].astype(I32) + accs[3].astype(I32))
        return all_rows_sum((tot[0:SUBLANES] + tot[SUBLANES:]).astype(F32))

    def search16(ref):
        ans = jnp.full((SUBLANES, qb), -32768, I32)
        cnt = jnp.full((SUBLANES, qb), 1e9, F32)
        for bit in range(15, -1, -1):
            cand = ans + (1 << bit)
            c = count16(ref, cand)
            ok = c >= kf
            ans = jnp.where(ok, cand, ans)
            cnt = jnp.where(ok, c, cnt)
        return ans[0:1], cnt[0:1]

    thr_hi, _ = search16(hi_ref)

    def low_body(c, carry):
        key = keys_ref[rows_of(c), :]
        hi = lax.shift_right_arithmetic(key, 16)
        lo = (key & 0xFFFF) - 32768
        k2 = jnp.where(hi > thr_hi, 32767, jnp.where(hi < thr_hi, -32768, lo))
        lo_ref[rows_of(c), :] = k2.astype(I16)
        return carry
    lax.fori_loop(0, nkc, low_body, 0)
    thr_lo, n_ge = search16(lo_ref)
    thr = thr_hi * 65536 + (thr_lo + 32768)
    thr_eff = jnp.maximum(thr, KEY_NEG_INF + 1)

    tied = (n_ge > kf) & (thr > KEY_NEG_INF)

    @pl.when(jnp.max(jnp.where(tied, 1.0, 0.0)) > 0.0)
    def _():
        def count32(pred_fn):
            def body(c, acc):
                ind = jnp.where(pred_fn(keys_ref[rows_of(c), :], key_index(c)), 1.0, 0.0)
                return acc + jnp.sum(ind.reshape(kblk // SUBLANES, SUBLANES, qb), axis=0)
            acc = lax.fori_loop(0, nkc, body, jnp.zeros((SUBLANES, qb), F32))
            return jnp.sum(acc, axis=0, keepdims=True)
        need = kf - count32(lambda key, s: key > thr)
        nbits = int(l_pad - 1).bit_length()

        def idx_body(it, p):
            cand = p + lax.shift_left(jnp.int32(1), nbits - 1 - it)
            few = count32(lambda key, s: (key == thr) & (s < cand)) < need
            return jnp.where(few, cand, p)
        p_last = lax.fori_loop(0, nbits, idx_body, jnp.zeros((1, qb), I32))

        def demote(c, carry):
            key = keys_ref[rows_of(c), :]
            drop = tied & (key == thr) & (key_index(c) > p_last)
            keys_ref[rows_of(c), :] = jnp.where(drop, thr - 1, key)
            return carry
        lax.fori_loop(0, nkc, demote, 0)

    m_ref[...] = jnp.full(m_ref.shape, M_FLOOR, F32)
    l_ref[...] = jnp.zeros(l_ref.shape, F32)
    acc_ref[...] = jnp.zeros(acc_ref.shape, F32)
    slopes = [float(2.0 ** (-8.0 * (h + 1) / ATT_HEADS)) * LOG2E for h in range(ATT_HEADS)]
    nsub = kblk // ATT_SUB
    rows8 = lambda t: t.reshape(t.shape[0] // SUBLANES, SUBLANES, t.shape[1])

    def logits(g, c):
        lg_ref[g] = jnp.dot(k_ref[rows_of(c), g * ATT_HD:(g + 1) * ATT_HD], qt_ref[g],
                            preferred_element_type=F32)

    def softmax_block(g):
        for r in range(ATT_REP):
            h = g * ATT_REP + r
            cs = slice(r * qb, (r + 1) * qb)
            mx = jnp.full((SUBLANES, qb), M_FLOOR, F32)
            for sb in range(nsub):
                rs = slice(sb * ATT_SUB, (sb + 1) * ATT_SUB)
                lgs = lg_ref[g, rs, cs] - slopes[h] * dm_ref[rs, :]
                lg_ref[g, rs, cs] = lgs
                mx = jnp.maximum(mx, jnp.max(rows8(lgs), axis=0))
            m_old = m_ref[g, :, cs]
            m_new = jnp.maximum(m_old, all_rows(mx, jnp.maximum, M_FLOOR, 1))
            sm = jnp.zeros((SUBLANES, qb), F32)
            for sb in range(nsub):
                rs = slice(sb * ATT_SUB, (sb + 1) * ATT_SUB)
                pr = jnp.exp2(rows8(lg_ref[g, rs, cs]) - m_new[None])
                sm = sm + jnp.sum(pr, axis=0)
                pt_ref[g, rs, cs] = pr.reshape(ATT_SUB, qb).astype(BF16)
            alpha = jnp.exp2(m_old - m_new)
            l_ref[g, :, cs] = alpha * l_ref[g, :, cs] + all_rows_sum(sm, 2)
            m_ref[g, :, cs] = m_new
            al_ref[g, :, cs] = alpha

    def weighted_values(g, c):
        pv = jnp.dot(vt_ref[g * ATT_HD:(g + 1) * ATT_HD, rows_of(c)], pt_ref[g],
                     preferred_element_type=F32)
        acc_ref[g] = (rows8(acc_ref[g]) * al_ref[g][None]).reshape(ATT_HD, ATT_REP * qb) + pv

    logits(0, 0)

    def att_body(c, carry):
        key = keys_ref[rows_of(c), :]
        dist = jnp.abs(pos - key_index(c)).astype(F32)
        dm_ref[...] = jnp.where(key >= thr_eff, dist, MASK_DIST)
        logits(1, c)
        softmax_block(0)
        weighted_values(0, c)
        logits(0, jnp.minimum(c + 1, nkc - 1))
        softmax_block(1)
        weighted_values(1, c)
        return carry
    lax.fori_loop(0, nkc, att_body, 0)

    for h in range(ATT_HEADS):
        g, r = divmod(h, ATT_REP)
        cs = slice(r * qb, (r + 1) * qb)
        ot = (rows8(acc_ref[g, :, cs]) / l_ref[g, :, cs][None]).reshape(ATT_HD, qb)
        o_ref[:, h * ATT_HD:(h + 1) * ATT_HD] = ot.T.astype(BF16)


def _dsa(q_ops, k_all, vt_all, ki_all, *, nb, nqb, qb, past, l_real, kblk, out_rows=None):
    l_pad = k_all.shape[1]
    topk = min(IDX_TOPK, l_real // 4)
    assert kblk >= topk and kblk % ATT_SUB == 0 and l_pad % kblk == 0
    (qa, qa_map), (qi, qi_map), (wi, wi_map) = q_ops
    kern = functools.partial(_dsa_kernel, past=past, l_real=l_real, kblk=kblk, topk=topk)
    return pl.pallas_call(
        kern,
        grid=(nb, nqb),
        in_specs=[pl.BlockSpec((qb, ATT_W), qa_map), pl.BlockSpec((qb, IDX_HEADS * IDX_DIM), qi_map),
                  pl.BlockSpec((qb, LANES), wi_map),
                  pl.BlockSpec((None, l_pad, ATT_KV * ATT_HD), lambda b, i: (b, 0, 0)),
                  pl.BlockSpec((None, ATT_KV * ATT_HD, l_pad), lambda b, i: (b, 0, 0)),
                  pl.BlockSpec((None, l_pad, LANES), lambda b, i: (b, 0, 0))],
        out_specs=pl.BlockSpec((qb, ATT_W), lambda b, i: (b * nqb + i, 0)),
        out_shape=jax.ShapeDtypeStruct((out_rows or nb * nqb * qb, ATT_W), BF16),
        scratch_shapes=[pltpu.VMEM((l_pad, 2 * LANES), BF16),
                        pltpu.VMEM((l_pad, qb), I32),
                        pltpu.VMEM((l_pad, qb), I16),
                        pltpu.VMEM((l_pad, qb), I16),
                        pltpu.VMEM((2 * LANES, IDX_HEADS * qb), BF16),
                        pltpu.VMEM((ATT_KV, ATT_HD, ATT_REP * qb), BF16),
                        pltpu.VMEM((ATT_KV, SUBLANES, ATT_REP * qb), F32),
                        pltpu.VMEM((ATT_KV, SUBLANES, ATT_REP * qb), F32),
                        pltpu.VMEM((ATT_KV, ATT_HD, ATT_REP * qb), F32),
                        pltpu.VMEM((ATT_KV, kblk, ATT_REP * qb), F32),
                        pltpu.VMEM((ATT_KV, kblk, ATT_REP * qb), BF16),
                        pltpu.VMEM((kblk, qb), F32),
                        pltpu.VMEM((ATT_KV, SUBLANES, ATT_REP * qb), F32),
                        pltpu.VMEM((3, SUBLANES, qb), F32)],
        compiler_params=_cparams(("parallel", "arbitrary")),
        name="dsa",
    )(qa, qi, wi, k_all, vt_all, ki_all)


def _rw_prep_kernel(r_ref, k_ref, v_ref, lr_ref, hr_ref, hk_ref, hv_ref, hlr_ref,
                    sr_ref, sk_ref, sv_ref, slr_ref, mr_ref, mk_ref, mv_ref, mlr_ref,
                    w0_ref, a0_ref, w2_ref, a2_ref, g2_ref,
                    ro_ref, ko_ref, vo_ref, wo_ref, ao_ref, go_ref, *, tiles_per_seq, time_minor):
    first_tile = (pl.program_id(0) % tiles_per_seq) == 0
    put = (lambda t: t.T) if time_minor else (lambda t: t)

    def mix(x_ref, halo_ref, shift_ref, mu_ref):
        x = x_ref[...]
        first = jnp.where(first_tile, shift_ref[...], halo_ref[SUBLANES - 1:SUBLANES, :])
        rows = lax.broadcasted_iota(I32, x.shape, 0)
        prev = jnp.where(rows == 0, first, pltpu.roll(x, 1, 0))
        return x + (prev - x) * mu_ref[...]

    ro_ref[...] = put(mix(r_ref, hr_ref, sr_ref, mr_ref))
    ko_ref[...] = put(mix(k_ref, hk_ref, sk_ref, mk_ref))
    vo_ref[...] = put(mix(v_ref, hv_ref, sv_ref, mv_ref))
    lr = mix(lr_ref, hlr_ref, slr_ref, mlr_ref)
    xw = lr[:, 0:128]
    xa = lr[:, 128:256]
    xg = lr[:, 256:512]
    wpre = w0_ref[...] + _dot3(jnp.tanh(xw), w2_ref[...])
    wo_ref[...] = put(jnp.exp(-RW_DECAY_SCALE * _sigmoid(wpre)))
    ao_ref[...] = put(_sigmoid(a0_ref[...] + _dot3(xa, a2_ref[...])))
    go_ref[...] = _dot3(_sigmoid(xg), g2_ref[...])


def _rw_prep(proj, row0, nb, seq, shifts, mus, w0, a0, w2p, a2p, g2, tm, out_rows=None):
    m = nb * seq
    tps = seq // tm
    rb0 = row0 // tm
    hb = tm // SUBLANES
    cr, ck, cv = _OFF["r"] // RW_W, _OFF["k"] // RW_W, _OFF["v"] // RW_W
    clr = _OFF["xw"] // 512

    def main(c):
        return lambda i: (rb0 + i, c)

    def halo(c):
        return lambda i: (jnp.maximum((rb0 + i) * hb - 1, 0), c)

    bidx = lambda i: (i // tps, 0, 0)
    fix = lambda i: (0, 0)
    wide = [pl.BlockSpec((tm, RW_W), main(cr)), pl.BlockSpec((tm, RW_W), main(ck)),
            pl.BlockSpec((tm, RW_W), main(cv)), pl.BlockSpec((tm, 512), main(clr)),
            pl.BlockSpec((SUBLANES, RW_W), halo(cr)), pl.BlockSpec((SUBLANES, RW_W), halo(ck)),
            pl.BlockSpec((SUBLANES, RW_W), halo(cv)), pl.BlockSpec((SUBLANES, 512), halo(clr))]
    sh = [pl.BlockSpec((None, 1, RW_W), bidx)] * 3 + [pl.BlockSpec((None, 1, 512), bidx)]
    mu = [pl.BlockSpec((1, RW_W), fix)] * 3 + [pl.BlockSpec((1, 512), fix)]
    par = [pl.BlockSpec((1, RW_W), fix), pl.BlockSpec((1, RW_W), fix),
           pl.BlockSpec((128, RW_W), fix), pl.BlockSpec((128, RW_W), fix),
           pl.BlockSpec((RW_G_R, RW_W), fix)]
    token_major = pl.BlockSpec((tm, RW_W), lambda i: (i, 0))
    time_minor = _relayout_ok(nb, seq) and tm % LANES == 0
    if time_minor:
        out = jax.ShapeDtypeStruct((nb, RW_W, seq), F32)
        seq_spec = pl.BlockSpec((None, RW_W, tm), lambda i: (i // tps, 0, i % tps))
    else:
        out = jax.ShapeDtypeStruct((m, RW_W), F32)
        seq_spec = token_major
    return pl.pallas_call(
        functools.partial(_rw_prep_kernel, tiles_per_seq=tps, time_minor=time_minor),
        grid=(m // tm,),
        in_specs=wide + sh + mu + par,
        out_specs=[seq_spec] * 5 + [token_major],
        out_shape=[out] * 5 + [jax.ShapeDtypeStruct((out_rows or m, RW_W), F32)],
        compiler_params=_cparams(("parallel",)),
        name="rwkv_prep",
    )(proj, proj, proj, proj, proj, proj, proj, proj, *shifts, *mus, w0, a0, w2p, a2p, g2)


def _rw_scan_kernel(r_ref, k_ref, v_ref, w_ref, a_ref, kkp_ref, kap_ref, rkp_ref,
                    lng_ref, lnb_ref, s0_ref, o_ref, st_ref, s_scr, tmp_scr):
    tb = pl.program_id(1)
    nsteps = r_ref.shape[0]

    @pl.when(tb == 0)
    def _():
        s_scr[...] = s0_ref[...]

    def step(t, d_prev):
        k = k_ref[t]
        a = a_ref[t]
        r = r_ref[t]
        kk = k * kkp_ref[...]
        nrm = jnp.sqrt(jnp.sum(kk * kk, axis=0, keepdims=True))
        kk = kk / jnp.maximum(nrm, 1e-12)
        km = k * (1.0 + (a - 1.0) * kap_ref[...])
        d = d_prev * w_ref[t]
        inv_d = 1.0 / d
        tmp_scr[0] = kk * d_prev
        tmp_scr[1] = kk * a * inv_d
        tmp_scr[2] = km * inv_d
        tmp_scr[3] = r * d
        parts = [jnp.zeros((RW_HD, LANES), F32) for _ in range(2)]
        for kx in range(RW_HD):
            parts[kx % 2] = parts[kx % 2] + s_scr[kx] * tmp_scr[0, kx:kx + 1, :]
        sa = -(parts[0] + parts[1])
        v = v_ref[t]
        outs = [jnp.zeros((RW_HD, LANES), F32) for _ in range(2)]
        for kx in range(RW_HD):
            s_new = s_scr[kx] + (sa * tmp_scr[1, kx:kx + 1, :] + v * tmp_scr[2, kx:kx + 1, :])
            s_scr[kx] = s_new
            outs[kx % 2] = outs[kx % 2] + s_new * tmp_scr[3, kx:kx + 1, :]
        o = outs[0] + outs[1]
        mu = jnp.mean(o, axis=0, keepdims=True)
        oc = o - mu
        var = jnp.mean(oc * oc, axis=0, keepdims=True)
        on = oc * lax.rsqrt(var + RW_GN_EPS) * lng_ref[...] + lnb_ref[...]
        bonus = jnp.sum(r * km * rkp_ref[...], axis=0, keepdims=True) * v
        o_ref[t] = on + bonus
        return d
    d_end = lax.fori_loop(0, nsteps, step, jnp.ones((RW_HD, LANES), F32))
    tmp_scr[0] = d_end
    for kx in range(RW_HD):
        s_scr[kx] = s_scr[kx] * tmp_scr[0, kx:kx + 1, :]

    @pl.when(tb == pl.num_programs(1) - 1)
    def _():
        st_ref[...] = s_scr[...]


def _rw_scan(r, k, v, w, a, params, s0, tb):
    seq, _, pairs = r.shape
    assert pairs % LANES == 0, "batch * RW_HEADS must fill whole lane groups"
    seqb = pl.BlockSpec((tb, RW_HD, LANES), lambda g, t: (t, 0, g))
    par = pl.BlockSpec((RW_HD, LANES), lambda g, t: (0, 0))
    st = pl.BlockSpec((RW_HD, RW_HD, LANES), lambda g, t: (0, 0, g))
    return pl.pallas_call(
        _rw_scan_kernel,
        grid=(pairs // LANES, seq // tb),
        in_specs=[seqb] * 5 + [par] * 5 + [st],
        out_specs=[seqb, st],
        out_shape=[jax.ShapeDtypeStruct((seq, RW_HD, pairs), F32),
                   jax.ShapeDtypeStruct((RW_HD, RW_HD, pairs), F32)],
        scratch_shapes=[pltpu.VMEM((RW_HD, RW_HD, LANES), F32),
                        pltpu.VMEM((4, RW_HD, LANES), F32)],
        compiler_params=_cparams(("parallel", "arbitrary")),
        name="rwkv_scan",
    )(r, k, v, w, a, *params, s0)


GLA_SUB = 16


def _gla_kernel(q_ref, k_ref, v_ref, gl_ref, og_ref, w2_ref, gb_ref, ng_ref, s0_ref,
                o_ref, st_ref, s_scr, la_scr, att_scr):
    tblk = pl.program_id(1)
    tc = q_ref.shape[0]

    @pl.when(tblk == 0)
    def _():
        for h in range(GLA_HEADS):
            s_scr[h] = s0_ref[h].T

    la_scr[...] = _log_sigmoid(_dot3(gl_ref[...], w2_ref[...]) + gb_ref[...]) * (1.0 / GLA_GATE_NORM)
    rows = lax.broadcasted_iota(I32, (CHUNK, GLA_DK), 0)
    lane = lax.broadcasted_iota(I32, (GLA_SUB, CHUNK), 1)
    jrow = lax.broadcasted_iota(I32, (GLA_SUB, GLA_DK), 0)
    nsub = CHUNK // GLA_SUB

    def head_chunk(h, rs):
        ks = slice(h * GLA_DK, (h + 1) * GLA_DK)
        vs = slice(h * GLA_DV, (h + 1) * GLA_DV)
        b = la_scr[rs, ks]
        sh = 1
        while sh < CHUNK:
            b = b + jnp.where(rows >= sh, pltpu.roll(b, sh, 0), 0.0)
            sh *= 2
        q = q_ref[rs, ks] * (GLA_DK ** -0.5)
        k = k_ref[rs, ks]
        v = v_ref[rs, vs]
        s_old = s_scr[h]
        inter = _nt_dot(q * jnp.exp(b), s_old)
        b_last = b[CHUNK - 1:CHUNK, :]

        for jb in range(nsub):
            js = slice(jb * GLA_SUB, (jb + 1) * GLA_SUB)
            bj = b[js]
            kj = k[js]
            blk = jnp.zeros((GLA_SUB, CHUNK), F32)
            for ii in range(GLA_SUB):
                i_row = jb * GLA_SUB + ii
                e = jnp.exp(jnp.minimum(b[i_row:i_row + 1, :] - bj, 0.0))
                val = jnp.where(jrow <= ii, q[i_row:i_row + 1, :] * kj * e, 0.0)
                col = jnp.sum(val, axis=1, keepdims=True)
                blk = jnp.where(lane == i_row, col, blk)
            if jb + 1 < nsub:
                i0 = (jb + 1) * GLA_SUB
                be = b[i0 - 1:i0, :]
                kt = kj * jnp.exp(be - bj)
                later = rows >= i0
                qt = jnp.where(later, q * jnp.exp(jnp.where(later, b - be, 0.0)), 0.0)
                blk = blk + _nt_dot(kt, qt)
            att_scr[h, js, :] = blk
        o = inter + _tn_dot(att_scr[h], v)
        khat = k * jnp.exp(b_last - b)
        s_scr[h] = s_old * jnp.exp(b_last) + _tn_dot(v, khat)
        o = o * lax.rsqrt(jnp.mean(o * o, -1, keepdims=True) + GLA_NORM_EPS) * ng_ref[:, vs]
        og = og_ref[rs, vs]
        o_ref[rs, vs] = (o * (og * _sigmoid(og))).astype(BF16)

    def chunk_body(c, carry):
        rs = pl.ds(pl.multiple_of(c * CHUNK, CHUNK), CHUNK)
        for h in range(GLA_HEADS):
            head_chunk(h, rs)
        return carry
    lax.fori_loop(0, tc // CHUNK, chunk_body, 0)

    @pl.when(tblk == pl.num_programs(1) - 1)
    def _():
        for h in range(GLA_HEADS):
            st_ref[h] = s_scr[h].T


def _gla(proj, tail, row0, nb, seq, w2p, gb, ng, s0, tc, out_rows=None):
    ntb = seq // tc
    rb0 = row0 // tc
    cq, ck = _OFF["qc"] // GLA_KW, _OFF["kc"] // GLA_KW
    cv, cg = _OFF["vc"] // GLA_VW, _OFF["og"] // GLA_VW
    cgl = (_OFF["gl"] - _OFF["ka"]) // LANES

    def rows(c0):
        return lambda b, t: (rb0 + b * ntb + t, c0)

    fix = lambda b, t: (0, 0)
    st = pl.BlockSpec((None, GLA_HEADS, GLA_DK, GLA_DV), lambda b, t: (b, 0, 0, 0))
    return pl.pallas_call(
        _gla_kernel,
        grid=(nb, ntb),
        in_specs=[pl.BlockSpec((tc, GLA_KW), rows(cq)), pl.BlockSpec((tc, GLA_KW), rows(ck)),
                  pl.BlockSpec((tc, GLA_VW), rows(cv)), pl.BlockSpec((tc, LANES), rows(cgl)),
                  pl.BlockSpec((tc, GLA_VW), rows(cg)),
                  pl.BlockSpec((LANES, GLA_KW), fix), pl.BlockSpec((1, GLA_KW), fix),
                  pl.BlockSpec((1, GLA_VW), fix), st],
        out_specs=[pl.BlockSpec((tc, GLA_VW), lambda b, t: (b * ntb + t, 0)), st],
        out_shape=[jax.ShapeDtypeStruct((out_rows or nb * seq, GLA_VW), BF16),
                   jax.ShapeDtypeStruct((nb, GLA_HEADS, GLA_DK, GLA_DV), F32)],
        scratch_shapes=[pltpu.VMEM((GLA_HEADS, GLA_DV, GLA_DK), F32), pltpu.VMEM((tc, GLA_KW), F32),
                        pltpu.VMEM((GLA_HEADS, CHUNK, CHUNK), F32)],
        compiler_params=_cparams(("parallel", "arbitrary")),
        name="gla",
    )(proj, proj, proj, tail, proj, w2p, gb, ng, s0)


_MAIN_SEGS = tuple(sg for sg in _SEGS if _OFF[sg[0]] < _OFF["ka"])
_TAIL_SEGS = tuple(sg for sg in _SEGS if _OFF[sg[0]] >= _OFF["ka"])


def _pack_cols(w, segs=_SEGS, dtype=None):
    dtype = dtype or w.dtype
    parts = []
    for _, s, wd, p in segs:
        parts.append(w[..., s:s + wd].astype(dtype))
        if p > wd:
            parts.append(jnp.zeros(w.shape[:-1] + (p - wd,), dtype))
    return jnp.concatenate(parts, axis=-1)


def _pack_rw(vec):
    r, k, v = vec[..., 0:1024], vec[..., 1024:2048], vec[..., 2048:3072]
    z = jnp.zeros(vec.shape[:-1] + (32,), vec.dtype)
    lr = jnp.concatenate([vec[..., 3072:3168], z, vec[..., 3168:3264], z, vec[..., 3264:3520]], -1)
    return r, k, v, lr


def _pad_rows(w, n):
    return jnp.concatenate([w, jnp.zeros((n - w.shape[0],) + w.shape[1:], w.dtype)], 0)


RELAYOUT_T = 128


def _to_scan_kernel(y_ref, o_ref):
    nbat = y_ref.shape[0]
    for n in range(RW_HD):
        z = jnp.concatenate([y_ref[b, pl.ds(n, RW_HEADS, stride=RW_HD), :] for b in range(nbat)], axis=0)
        o_ref[:, n, :] = z.T


def _from_scan_kernel(o_ref, x_ref, y_scr):
    nbat = x_ref.shape[0]
    for n in range(RW_HD):
        zt = o_ref[:, n, :].T
        for b in range(nbat):
            y_scr[b, pl.ds(n, RW_HEADS, stride=RW_HD), :] = zt[b * RW_HEADS:(b + 1) * RW_HEADS]
    for b in range(nbat):
        for c in range(RW_W // LANES):
            x_ref[b, :, c * LANES:(c + 1) * LANES] = y_scr[b, c * LANES:(c + 1) * LANES, :].T


def _relayout_ok(nb, seq):
    return nb * RW_HEADS == LANES and seq % RELAYOUT_T == 0


def _to_scan(x, nb, seq):
    if x.ndim == 2:
        return jnp.transpose(x.reshape(nb, seq, RW_HEADS, RW_HD), (1, 3, 0, 2)).reshape(seq, RW_HD, nb * RW_HEADS)
    return pl.pallas_call(
        _to_scan_kernel,
        grid=(seq // RELAYOUT_T,),
        in_specs=[pl.BlockSpec((nb, RW_W, RELAYOUT_T), lambda t: (0, 0, t))],
        out_specs=pl.BlockSpec((RELAYOUT_T, RW_HD, LANES), lambda t: (t, 0, 0)),
        out_shape=jax.ShapeDtypeStruct((seq, RW_HD, LANES), F32),
        compiler_params=_cparams(("parallel",)),
        name="to_scan_layout",
    )(x)


def _from_scan(o, nb, seq):
    if not _relayout_ok(nb, seq):
        return jnp.transpose(o.reshape(seq, RW_HD, nb, RW_HEADS), (2, 0, 3, 1)).reshape(nb * seq, RW_W)
    return pl.pallas_call(
        _from_scan_kernel,
        grid=(seq // RELAYOUT_T,),
        in_specs=[pl.BlockSpec((RELAYOUT_T, RW_HD, LANES), lambda t: (t, 0, 0))],
        out_specs=pl.BlockSpec((nb, RELAYOUT_T, RW_W), lambda t: (0, t, 0)),
        out_shape=jax.ShapeDtypeStruct((nb, seq, RW_W), F32),
        scratch_shapes=[pltpu.VMEM((nb, RW_W, RELAYOUT_T), F32)],
        compiler_params=_cparams(("parallel",)),
        name="from_scan_layout",
    )(o).reshape(nb * seq, RW_W)


def _group_mixers(proj, tail, row0, nb, seq, caches, lw, out_rows=None):
    m = nb * seq
    cache_k, cache_v, cache_ki, s_rw, s_shift, s_gla = caches
    past = 0 if cache_k is None else cache_k.shape[1]
    l_real = past + seq
    t0 = _OFF["ka"]
    grp = tail[row0:row0 + m].reshape(nb, seq, -1)

    ka = grp[..., _OFF["ka"] - t0:_OFF["ka"] - t0 + 256]
    va = grp[..., _OFF["va"] - t0:_OFF["va"] - t0 + 256]
    ki = grp[..., _OFF["ki"] - t0:_OFF["ki"] - t0 + LANES]
    if past:
        k_all = jnp.concatenate([cache_k.reshape(nb, past, 256), ka], 1)
        v_all = jnp.concatenate([cache_v.reshape(nb, past, 256), va], 1)
        cki = jnp.concatenate([cache_ki, jnp.zeros((nb, past, LANES - IDX_DIM), F32)], -1)
        ki_all = jnp.concatenate([cki, ki], 1)
    else:
        k_all, v_all, ki_all = ka, va, ki
    kblk = 512 if l_real % 512 == 0 else 384
    l_pad = -(-l_real // kblk) * kblk
    if l_pad > l_real:
        padk = lambda t: jnp.concatenate([t, jnp.zeros((nb, l_pad - l_real, t.shape[-1]), t.dtype)], 1)
        k_all, v_all, ki_all = padk(k_all), padk(v_all), padk(ki_all)
    k_bf = k_all.astype(BF16)
    vt_bf = jnp.swapaxes(v_all, 1, 2).astype(BF16)
    qb = DSA_QB if seq % DSA_QB == 0 else LANES
    if seq % qb == 0:
        nqb = seq // qb
        rb0 = row0 // qb
        mk = lambda c: (lambda b, i: (rb0 + b * nqb + i, c))
        q_ops = ((proj, mk(_OFF["qa"] // ATT_W)), (proj, mk(_OFF["qi"] // 1024)),
                 (tail, mk((_OFF["wi"] - t0) // LANES)))
        o_a = _dsa(q_ops, k_bf, vt_bf, ki_all, nb=nb, nqb=nqb, qb=qb, past=past, l_real=l_real, kblk=kblk,
                   out_rows=out_rows)
    else:
        def padq(t):
            return jnp.concatenate([t, jnp.zeros((nb, qb - seq, t.shape[-1]), F32)], 1).reshape(nb * qb, -1)
        qrows = proj[row0:row0 + m, 0:_OFF["vc"]].reshape(nb, seq, -1)
        mk = lambda b, i: (b, 0)
        q_ops = ((padq(qrows[..., 0:ATT_W]), mk), (padq(qrows[..., ATT_W:]), mk),
                 (padq(grp[..., _OFF["wi"] - t0:_OFF["wi"] - t0 + LANES]), mk))
        o_a = _dsa(q_ops, k_bf, vt_bf, ki_all, nb=nb, nqb=1, qb=qb, past=past, l_real=l_real, kblk=kblk)
        o_a = o_a.reshape(nb, qb, ATT_W)[:, :seq].reshape(m, ATT_W)

    if s_shift is None:
        s_shift = jnp.zeros((nb, RW_PROJ), F32)
    shifts = [t.reshape(nb, 1, -1) for t in _pack_rw(s_shift)]
    tm = 256 if seq % 256 == 0 else seq
    r, k, v, w, a, g = _rw_prep(proj, row0, nb, seq, shifts, lw["rw_mus"], lw["rw_w0"], lw["rw_a0"],
                                lw["rw_w2p"], lw["rw_a2p"], lw["rw_g2"], tm, out_rows)
    if s_rw is None:
        s0 = jnp.zeros((RW_HD, RW_HD, nb * RW_HEADS), F32)
    else:
        s0 = jnp.transpose(s_rw, (3, 2, 0, 1)).reshape(RW_HD, RW_HD, nb * RW_HEADS)
    tile = lambda p: jnp.tile(p.reshape(RW_HEADS, RW_HD).T, (1, LANES // RW_HEADS))
    params = [tile(lw[n]) for n in ("rw_kk", "rw_ka", "rw_rk", "rw_ln_g", "rw_ln_b")]
    tb = 32 if seq % 32 == 0 else seq
    o_b, s_t = _rw_scan(*[_to_scan(t, nb, seq) for t in (r, k, v, w, a)], params, s0, tb)
    o_b = _from_scan(o_b, nb, seq)
    s_rw_new = jnp.transpose(s_t.reshape(RW_HD, RW_HD, nb, RW_HEADS), (2, 3, 1, 0))
    last = jnp.take(proj, row0 + seq - 1 + seq * jnp.arange(nb), axis=0)[:, _OFF["r"]:_OFF["qc"]]
    base = _OFF["r"]
    shift_new = jnp.concatenate(
        [last[:, 0:3072], last[:, _OFF["xw"] - base:_OFF["xw"] - base + 96],
         last[:, _OFF["xa"] - base:_OFF["xa"] - base + 96], last[:, _OFF["xg"] - base:]], -1)

    if s_gla is None:
        s_gla = jnp.zeros((nb, GLA_HEADS, GLA_DK, GLA_DV), F32)
    tc = 512 if seq % 512 == 0 else seq
    o_c, s_gla_new = _gla(proj, tail, row0, nb, seq, lw["gla_w2p"], lw["gla_b"], lw["gla_ng"], s_gla, tc,
                          out_rows)

    new_state = (ka.reshape(nb, seq, ATT_KV, ATT_HD), va.reshape(nb, seq, ATT_KV, ATT_HD),
                 ki[..., :IDX_DIM], s_rw_new, shift_new, s_gla_new)
    return o_a, o_b, g, o_c, new_state


def kernel(x_prompt, x_sample, cache_k, cache_v, cache_kidx, state_rwkv, state_rwkv_shift, state_gla,
           w_in, rw_mu, rw_w0, rw_w2, rw_a0, rw_a2, rw_g2, rw_kk, rw_ka, rw_rk, rw_ln_g, rw_ln_b,
           gla_w2, gla_b, gla_norm_g, w_branch_a, w_branch_b, w_branch_c, w_out,
           ln1_g, ln1_b, ln2_g, ln2_b, ffn_w1, ffn_w3, ffn_w2):
    bp, tp, _ = x_prompt.shape
    bs, ts, _ = x_sample.shape
    mp, ms = bp * tp, bs * ts
    x = jnp.concatenate([x_prompt.reshape(mp, D_MODEL), x_sample.reshape(ms, D_MODEL)], 0)
    xb = x.astype(BF16)
    m = mp + ms
    tm_big = 1024 if m % 1024 == 0 else 256
    tm_mid = 512 if m % 512 == 0 else 256

    new_p = [[] for _ in range(6)]
    new_s = [[] for _ in range(6)]
    for l in range(DEPTH):
        lw = {
            "rw_mus": [t.reshape(1, -1) for t in _pack_rw(rw_mu[l])],
            "rw_w0": rw_w0[l].reshape(1, RW_W), "rw_a0": rw_a0[l].reshape(1, RW_W),
            "rw_w2p": _pad_rows(rw_w2[l], 128), "rw_a2p": _pad_rows(rw_a2[l], 128), "rw_g2": rw_g2[l],
            "rw_kk": rw_kk[l], "rw_ka": rw_ka[l], "rw_rk": rw_rk[l],
            "rw_ln_g": rw_ln_g[l], "rw_ln_b": rw_ln_b[l],
            "gla_w2p": _pad_rows(gla_w2[l], LANES), "gla_b": gla_b[l].reshape(1, GLA_KW),
            "gla_ng": gla_norm_g[l].reshape(1, GLA_VW),
        }
        proj = _matmul(xb, _pack_cols(w_in[l], _MAIN_SEGS, BF16), tm_big, 1024)
        tail = _matmul(xb, _pack_cols(w_in[l], _TAIL_SEGS, BF16), tm_big, PROJ_W - _OFF["ka"])
        oa_p, ob_p, g_p, oc_p, st_p = _group_mixers(proj, tail, 0, bp, tp, (None,) * 6, lw, out_rows=m)
        oa_s, ob_s, g_s, oc_s, st_s = _group_mixers(
            proj, tail, mp, bs, ts,
            (cache_k[l], cache_v[l], cache_kidx[l], state_rwkv[l], state_rwkv_shift[l], state_gla[l]), lw)
        put = lambda full, part: lax.dynamic_update_slice(full, part, (mp, 0))
        merged = _merge(put(oa_p, oa_s), jnp.concatenate([ob_p, ob_s], 0), put(g_p, g_s), put(oc_p, oc_s), proj,
                        w_branch_a[l].astype(BF16), w_branch_b[l].astype(BF16),
                        w_branch_c[l].astype(BF16), 256, D_MODEL)
        h, hb = _hproj(merged, w_out[l].astype(BF16), x, ln1_g[l].reshape(1, -1),
                       ln1_b[l].reshape(1, -1), tm_mid)
        x, xb = _ffn(hb, h, ffn_w1[l].astype(BF16), ffn_w3[l].astype(BF16), ffn_w2[l].astype(BF16),
                     ln2_g[l].reshape(1, -1), ln2_b[l].reshape(1, -1), tm_mid, 512)
        for i in range(6):
            new_p[i].append(st_p[i])
            new_s[i].append(st_s[i])
    sp = [jnp.stack(a) for a in new_p]
    ss = [jnp.stack(a) for a in new_s]
    yp = x[:mp].reshape(bp, tp, D_MODEL)
    ys = x[mp:].reshape(bs, ts, D_MODEL)
    return (yp, ys, sp[0], sp[1], sp[2], sp[3], sp[4], sp[5], ss[0], ss[1], ss[2], ss[3], ss[4], ss[5])
```

```python
import functools

import jax
import jax.numpy as jnp
from jax import lax
from jax.experimental import pallas as pl
from jax.experimental.pallas import tpu as pltpu

F32 = jnp.float32
BF16 = jnp.bfloat16
I32 = jnp.int32
I16 = jnp.int16

D_MODEL = 2048
DEPTH = 4
CHUNK = 64
ATT_HEADS = 8
ATT_HD = 128
ATT_KV = 2
ATT_REP = ATT_HEADS // ATT_KV
ATT_W = ATT_HEADS * ATT_HD
IDX_HEADS = 16
IDX_DIM = 64
IDX_TOPK = 256
IDX_SCALE = (IDX_DIM * IDX_HEADS) ** -0.5
RW_HEADS = 16
RW_HD = 64
RW_W = RW_HEADS * RW_HD
RW_DECAY_R = 96
RW_A_R = 96
RW_G_R = 256
RW_PROJ = 3 * RW_W + RW_DECAY_R + RW_A_R + RW_G_R
RW_DECAY_SCALE = 0.606531
RW_GN_EPS = 64e-5
GLA_HEADS = 4
GLA_DK = 128
GLA_DV = 256
GLA_KW = GLA_HEADS * GLA_DK
GLA_VW = GLA_HEADS * GLA_DV
GLA_GATE_R = 16
GLA_GATE_NORM = 16.0
GLA_NORM_EPS = 1e-5
FFN_HIDDEN = -(-8 * D_MODEL // (3 * 256)) * 256
DN_ALPHA = (2 * DEPTH) ** 0.25
LN_EPS = 1e-5
LOG2E = 1.4426950408889634

LANES = 128
SUBLANES = 8
PACKED_ROWS = 16
VMEM_LIMIT = 56 * 1024 * 1024

_SEGS = (
    ("qa", 0, 1024, 1024), ("qi", 1536, 1024, 1024), ("vc", 7184, 1024, 1024),
    ("og", 8224, 1024, 1024), ("gates", 9248, 6144, 6144),
    ("r", 2640, 1024, 1024), ("k", 3664, 1024, 1024), ("v", 4688, 1024, 1024),
    ("xw", 5712, 96, 128), ("xa", 5808, 96, 128), ("xg", 5904, 256, 256),
    ("qc", 6160, 512, 512), ("kc", 6672, 512, 512),
    ("ka", 1024, 256, 256), ("va", 1280, 256, 256),
    ("ki", 2560, 64, 128), ("wi", 2624, 16, 128), ("gl", 8208, 16, 128),
)
_OFF = {}
_o = 0
for _n, _s, _w, _p in _SEGS:
    _OFF[_n] = _o
    _o += _p
PROJ_W = _o
KEY_NEG_INF = -0x7F800000
MASK_DIST = 1e32
M_FLOOR = -1e29


def _cparams(sem):
    return pltpu.CompilerParams(dimension_semantics=sem, vmem_limit_bytes=VMEM_LIMIT)


def _split2(x):
    hi = x.astype(BF16)
    lo = (x - hi.astype(F32)).astype(BF16)
    return hi, lo


def _dot3(a, b):
    ah, al = _split2(a)
    bh, bl = _split2(b)
    d = functools.partial(jnp.dot, preferred_element_type=F32)
    return d(ah, bh) + (d(al, bh) + d(ah, bl))


def _nt_dot(a, b):
    return lax.dot_general(a.astype(BF16), b.astype(BF16), (((1,), (1,)), ((), ())),
                           preferred_element_type=F32)


def _tn_dot(a, b):
    return lax.dot_general(a.astype(BF16), b.astype(BF16), (((0,), (0,)), ((), ())),
                           preferred_element_type=F32)


def _sigmoid(x):
    return 1.0 / (1.0 + jnp.exp(-x))


def _log_sigmoid(x):
    return jnp.minimum(x, 0.0) - jnp.log(1.0 + jnp.exp(-jnp.abs(x)))


def _layer_norm(x, g, b):
    mu = jnp.mean(x, -1, keepdims=True)
    xc = x - mu
    var = jnp.mean(xc * xc, -1, keepdims=True)
    return xc * lax.rsqrt(var + LN_EPS) * g + b


def _mm_kernel(a_ref, b_ref, o_ref):
    o_ref[...] = jnp.dot(a_ref[...], b_ref[...], preferred_element_type=F32)


def _matmul(a, b, tm, tn):
    m, k = a.shape
    n = b.shape[1]
    return pl.pallas_call(
        _mm_kernel,
        grid=(m // tm, pl.cdiv(n, tn)),
        in_specs=[pl.BlockSpec((tm, k), lambda i, j: (i, 0)),
                  pl.BlockSpec((k, tn), lambda i, j: (0, j))],
        out_specs=pl.BlockSpec((tm, tn), lambda i, j: (i, j)),
        out_shape=jax.ShapeDtypeStruct((m, n), F32),
        compiler_params=_cparams(("parallel", "arbitrary")),
        name="in_proj",
    )(a, b)


def _merge_kernel(oa_ref, ob_ref, g_ref, oc_ref, ga_ref, gb_ref, gc_ref,
                  wa_ref, wb_ref, wc_ref, o_ref):
    ob = (ob_ref[...] * g_ref[...]).astype(BF16)
    d = functools.partial(jnp.dot, preferred_element_type=F32)
    acc = _sigmoid(ga_ref[...]) * d(oa_ref[...], wa_ref[...])
    acc += _sigmoid(gb_ref[...]) * d(ob, wb_ref[...])
    acc += _sigmoid(gc_ref[...]) * d(oc_ref[...], wc_ref[...])
    o_ref[...] = acc.astype(BF16)


def _merge(oa, ob, g, oc, proj, wa, wb, wc, tm, tn):
    m = oa.shape[0]
    gblk = _OFF["gates"] // tn
    nj = D_MODEL // tn
    row = lambda i, j: (i, 0)
    wspec = pl.BlockSpec((ATT_W, tn), lambda i, j: (0, j))
    return pl.pallas_call(
        _merge_kernel,
        grid=(m // tm, nj),
        in_specs=[pl.BlockSpec((tm, ATT_W), row), pl.BlockSpec((tm, RW_W), row),
                  pl.BlockSpec((tm, RW_W), row), pl.BlockSpec((tm, GLA_VW), row),
                  pl.BlockSpec((tm, tn), lambda i, j: (i, gblk + j)),
                  pl.BlockSpec((tm, tn), lambda i, j: (i, gblk + nj + j)),
                  pl.BlockSpec((tm, tn), lambda i, j: (i, gblk + 2 * nj + j)),
                  wspec, wspec, wspec],
        out_specs=pl.BlockSpec((tm, tn), lambda i, j: (i, j)),
        out_shape=jax.ShapeDtypeStruct((m, D_MODEL), BF16),
        compiler_params=_cparams(("parallel", "arbitrary")),
        name="merge",
    )(oa, ob, g, oc, proj, proj, proj, wa, wb, wc)


def _hproj_kernel(m_ref, w_ref, x_ref, g_ref, b_ref, h_ref, hb_ref):
    z = DN_ALPHA * x_ref[...] + jnp.dot(m_ref[...], w_ref[...], preferred_element_type=F32)
    h = _layer_norm(z, g_ref[...], b_ref[...])
    h_ref[...] = h
    hb_ref[...] = h.astype(BF16)


def _hproj(merged, w_out, x, g, b, tm):
    m = x.shape[0]
    row = lambda i: (i, 0)
    fix = lambda i: (0, 0)
    return pl.pallas_call(
        _hproj_kernel,
        grid=(m // tm,),
        in_specs=[pl.BlockSpec((tm, D_MODEL), row), pl.BlockSpec((D_MODEL, D_MODEL), fix),
                  pl.BlockSpec((tm, D_MODEL), row), pl.BlockSpec((1, D_MODEL), fix),
                  pl.BlockSpec((1, D_MODEL), fix)],
        out_specs=[pl.BlockSpec((tm, D_MODEL), row), pl.BlockSpec((tm, D_MODEL), row)],
        out_shape=[jax.ShapeDtypeStruct((m, D_MODEL), F32),
                   jax.ShapeDtypeStruct((m, D_MODEL), BF16)],
        compiler_params=_cparams(("parallel",)),
        name="out_proj_ln",
    )(merged, w_out, x, g, b)


def _ffn_kernel(hb_ref, h_ref, w1_ref, w3_ref, w2_ref, g_ref, b_ref, y_ref, yb_ref, acc_ref):
    j = pl.program_id(1)

    @pl.when(j == 0)
    def _():
        acc_ref[...] = jnp.zeros_like(acc_ref)

    hb = hb_ref[...]
    u1 = jnp.dot(hb, w1_ref[...], preferred_element_type=F32)
    u3 = jnp.dot(hb, w3_ref[...], preferred_element_type=F32)
    u = (u1 * _sigmoid(u1) * u3).astype(BF16)
    acc_ref[...] += jnp.dot(u, w2_ref[...], preferred_element_type=F32)

    @pl.when(j == pl.num_programs(1) - 1)
    def _():
        y = _layer_norm(DN_ALPHA * h_ref[...] + acc_ref[...], g_ref[...], b_ref[...])
        y_ref[...] = y
        yb_ref[...] = y.astype(BF16)


def _ffn(hb, h, w1, w3, w2, g, b, tm, th):
    m = h.shape[0]
    row = lambda i, j: (i, 0)
    fix = lambda i, j: (0, 0)
    return pl.pallas_call(
        _ffn_kernel,
        grid=(m // tm, FFN_HIDDEN // th),
        in_specs=[pl.BlockSpec((tm, D_MODEL), row), pl.BlockSpec((tm, D_MODEL), row),
                  pl.BlockSpec((D_MODEL, th), lambda i, j: (0, j)),
                  pl.BlockSpec((D_MODEL, th), lambda i, j: (0, j)),
                  pl.BlockSpec((th, D_MODEL), lambda i, j: (j, 0)),
                  pl.BlockSpec((1, D_MODEL), fix), pl.BlockSpec((1, D_MODEL), fix)],
        out_specs=[pl.BlockSpec((tm, D_MODEL), row), pl.BlockSpec((tm, D_MODEL), row)],
        out_shape=[jax.ShapeDtypeStruct((m, D_MODEL), F32),
                   jax.ShapeDtypeStruct((m, D_MODEL), BF16)],
        scratch_shapes=[pltpu.VMEM((tm, D_MODEL), F32)],
        compiler_params=_cparams(("parallel", "arbitrary")),
        name="ffn_ln",
    )(hb, h, w1, w3, w2, g, b)


ATT_SUB = 128
DSA_QB = 256


def _dsa_kernel(qa_ref, qi_ref, wi_ref, k_ref, vt_ref, ki_ref, o_ref,
                kcat_ref, keys_ref, hi_ref, lo_ref, wq_ref, qt_ref, m_ref, l_ref, acc_ref,
                lg_ref, pt_ref, dm_ref, al_ref, *, past, l_real, kblk, topk):
    qb = qa_ref.shape[0]
    l_pad = ki_ref.shape[0]
    i = pl.program_id(1)
    kf = float(topk)

    @pl.when(i == 0)
    def _():
        def prep(c, carry):
            r0 = pl.multiple_of(c * kblk, kblk)
            x = ki_ref[pl.ds(r0, kblk), :]
            hi = x.astype(BF16)
            lo = x - hi.astype(F32)
            a = (hi.astype(F32) + pltpu.roll(lo, IDX_DIM, 1)).astype(BF16)
            kcat_ref[pl.ds(r0, kblk), 0:LANES] = a
            kcat_ref[pl.ds(r0, kblk), LANES:2 * LANES] = hi
            return carry
        lax.fori_loop(0, l_pad // kblk, prep, 0)

    qit = qi_ref[...].T
    qhi = qit.astype(BF16)
    qlo = (qit - qhi.astype(F32)).astype(BF16)
    zero = jnp.zeros((IDX_DIM, qb), BF16)
    for h in range(IDX_HEADS):
        hs = slice(h * IDX_DIM, (h + 1) * IDX_DIM)
        cs = slice(h * qb, (h + 1) * qb)
        wq_ref[0:64, cs] = qhi[hs]
        wq_ref[64:128, cs] = qhi[hs]
        wq_ref[128:192, cs] = qlo[hs]
        wq_ref[192:256, cs] = zero
    wit = wi_ref[...].T
    qat = (qa_ref[...] * (ATT_HD ** -0.5 * LOG2E)).T.astype(BF16)
    for h in range(ATT_HEADS):
        g, r = divmod(h, ATT_REP)
        qt_ref[g, :, r * qb:(r + 1) * qb] = qat[h * ATT_HD:(h + 1) * ATT_HD]

    q0 = past + i * qb
    pos = q0 + lax.broadcasted_iota(I32, (1, qb), 1)
    pos_chunk = lax.shift_right_logical(pos, 6)
    adm_keys = jnp.minimum(l_real, (lax.shift_right_logical(q0 + qb - 1, 6) + 1) * CHUNK)
    nkc = lax.div(adm_keys + (kblk - 1), kblk)

    def rows_of(c):
        return pl.ds(pl.multiple_of(c * kblk, kblk), kblk)

    def key_index(c):
        return c * kblk + lax.broadcasted_iota(I32, (kblk, qb), 0)

    def score_body(c, carry):
        kc = kcat_ref[rows_of(c), :]
        acc = jnp.zeros((kblk, qb), F32)
        for hg in range(IDX_HEADS // 4):
            rel = jnp.dot(kc, wq_ref[:, hg * 4 * qb:(hg + 1) * 4 * qb],
                          preferred_element_type=F32)
            for hh in range(4):
                h = hg * 4 + hh
                acc = acc + jnp.maximum(rel[:, hh * qb:(hh + 1) * qb], 0.0) * wit[h:h + 1, :]
        score = acc * IDX_SCALE
        s_idx = key_index(c)
        adm = (lax.shift_right_logical(s_idx, 6) <= pos_chunk) & (s_idx < l_real)
        score = jnp.where(adm, score, -jnp.inf)
        bits = pltpu.bitcast(score, I32)
        key = jnp.where(bits < 0, -(bits & 0x7FFFFFFF), bits)
        keys_ref[rows_of(c), :] = key
        hi_ref[rows_of(c), :] = lax.shift_right_arithmetic(key, 16).astype(I16)
        return carry
    lax.fori_loop(0, nkc, score_body, 0)

    n_acc = 4
    one16 = jnp.ones((), I16)
    zero16 = jnp.zeros((), I16)

    def count16(ref, cand):
        c16 = cand.astype(I16)

        def body(c, accs):
            ind = jnp.where(ref[rows_of(c), :] >= c16, one16, zero16)
            accs = list(accs)
            for s in range(kblk // PACKED_ROWS):
                accs[s % n_acc] = accs[s % n_acc] + ind[s * PACKED_ROWS:(s + 1) * PACKED_ROWS]
            return tuple(accs)
        accs = lax.fori_loop(0, nkc, body, (jnp.zeros((PACKED_ROWS, qb), I16),) * n_acc)
        tot = (accs[0].astype(I32) + accs[1].astype(I32)) + (accs[2].astype(I32) + accs[3].astype(I32))
        return jnp.sum(tot, axis=0, keepdims=True).astype(F32)

    def search16(ref):
        ans = jnp.full((1, qb), -32768, I32)
        cnt = jnp.full((1, qb), 1e9, F32)
        for bit in range(15, -1, -1):
            cand = ans + (1 << bit)
            c = count16(ref, cand)
            ok = c >= kf
            ans = jnp.where(ok, cand, ans)
            cnt = jnp.where(ok, c, cnt)
        return ans, cnt

    thr_hi, _ = search16(hi_ref)

    def low_body(c, carry):
        key = keys_ref[rows_of(c), :]
        hi = lax.shift_right_arithmetic(key, 16)
        lo = (key & 0xFFFF) - 32768
        k2 = jnp.where(hi > thr_hi, 32767, jnp.where(hi < thr_hi, -32768, lo))
        lo_ref[rows_of(c), :] = k2.astype(I16)
        return carry
    lax.fori_loop(0, nkc, low_body, 0)
    thr_lo, n_ge = search16(lo_ref)
    thr = thr_hi * 65536 + (thr_lo + 32768)
    thr_eff = jnp.maximum(thr, KEY_NEG_INF + 1)

    tied = (n_ge > kf) & (thr > KEY_NEG_INF)

    @pl.when(jnp.max(jnp.where(tied, 1.0, 0.0)) > 0.0)
    def _():
        def count32(pred_fn):
            def body(c, acc):
                ind = jnp.where(pred_fn(keys_ref[rows_of(c), :], key_index(c)), 1.0, 0.0)
                return acc + jnp.sum(ind.reshape(kblk // SUBLANES, SUBLANES, qb), axis=0)
            acc = lax.fori_loop(0, nkc, body, jnp.zeros((SUBLANES, qb), F32))
            return jnp.sum(acc, axis=0, keepdims=True)
        need = kf - count32(lambda key, s: key > thr)
        nbits = int(l_pad - 1).bit_length()

        def idx_body(it, p):
            cand = p + lax.shift_left(jnp.int32(1), nbits - 1 - it)
            few = count32(lambda key, s: (key == thr) & (s < cand)) < need
            return jnp.where(few, cand, p)
        p_last = lax.fori_loop(0, nbits, idx_body, jnp.zeros((1, qb), I32))

        def demote(c, carry):
            key = keys_ref[rows_of(c), :]
            drop = tied & (key == thr) & (key_index(c) > p_last)
            keys_ref[rows_of(c), :] = jnp.where(drop, thr - 1, key)
            return carry
        lax.fori_loop(0, nkc, demote, 0)

    m_ref[...] = jnp.full(m_ref.shape, M_FLOOR, F32)
    l_ref[...] = jnp.zeros(l_ref.shape, F32)
    acc_ref[...] = jnp.zeros(acc_ref.shape, F32)
    slopes = [float(2.0 ** (-8.0 * (h + 1) / ATT_HEADS)) * LOG2E for h in range(ATT_HEADS)]
    nsub = kblk // ATT_SUB
    fold = lambda t, op: op(t.reshape(ATT_SUB // SUBLANES, SUBLANES, qb), axis=0)

    def logits(g, c):
        lg_ref[g] = jnp.dot(k_ref[rows_of(c), g * ATT_HD:(g + 1) * ATT_HD], qt_ref[g],
                            preferred_element_type=F32)

    def softmax_block(g):
        for r in range(ATT_REP):
            h = g * ATT_REP + r
            cs = slice(r * qb, (r + 1) * qb)
            mx = jnp.full((SUBLANES, qb), M_FLOOR, F32)
            for sb in range(nsub):
                rs = slice(sb * ATT_SUB, (sb + 1) * ATT_SUB)
                lgs = lg_ref[g, rs, cs] - slopes[h] * dm_ref[rs, :]
                lg_ref[g, rs, cs] = lgs
                mx = jnp.maximum(mx, fold(lgs, jnp.max))
            m_old = m_ref[g, :, cs]
            m_new = jnp.maximum(m_old, jnp.max(mx, axis=0, keepdims=True))
            sm = jnp.zeros((SUBLANES, qb), F32)
            for sb in range(nsub):
                rs = slice(sb * ATT_SUB, (sb + 1) * ATT_SUB)
                pr = jnp.exp2(lg_ref[g, rs, cs] - m_new)
                sm = sm + fold(pr, jnp.sum)
                pt_ref[g, rs, cs] = pr.astype(BF16)
            alpha = jnp.exp2(m_old - m_new)
            l_ref[g, :, cs] = alpha * l_ref[g, :, cs] + jnp.sum(sm, axis=0, keepdims=True)
            m_ref[g, :, cs] = m_new
            al_ref[g, :, cs] = alpha

    def weighted_values(g, c):
        pv = jnp.dot(vt_ref[g * ATT_HD:(g + 1) * ATT_HD, rows_of(c)], pt_ref[g],
                     preferred_element_type=F32)
        acc_ref[g] = acc_ref[g] * al_ref[g] + pv

    logits(0, 0)

    def att_body(c, carry):
        key = keys_ref[rows_of(c), :]
        dist = jnp.abs(pos - key_index(c)).astype(F32)
        dm_ref[...] = jnp.where(key >= thr_eff, dist, MASK_DIST)
        logits(1, c)
        softmax_block(0)
        weighted_values(0, c)
        logits(0, jnp.minimum(c + 1, nkc - 1))
        softmax_block(1)
        weighted_values(1, c)
        return carry
    lax.fori_loop(0, nkc, att_body, 0)

    for h in range(ATT_HEADS):
        g, r = divmod(h, ATT_REP)
        cs = slice(r * qb, (r + 1) * qb)
        ot = acc_ref[g, :, cs] / l_ref[g, :, cs]
        o_ref[:, h * ATT_HD:(h + 1) * ATT_HD] = ot.T.astype(BF16)


def _dsa(q_ops, k_all, vt_all, ki_all, *, nb, nqb, qb, past, l_real, kblk, out_rows=None):
    l_pad = k_all.shape[1]
    topk = min(IDX_TOPK, l_real // 4)
    assert kblk >= topk and kblk % ATT_SUB == 0 and l_pad % kblk == 0
    (qa, qa_map), (qi, qi_map), (wi, wi_map) = q_ops
    kern = functools.partial(_dsa_kernel, past=past, l_real=l_real, kblk=kblk, topk=topk)
    return pl.pallas_call(
        kern,
        grid=(nb, nqb),
        in_specs=[pl.BlockSpec((qb, ATT_W), qa_map), pl.BlockSpec((qb, IDX_HEADS * IDX_DIM), qi_map),
                  pl.BlockSpec((qb, LANES), wi_map),
                  pl.BlockSpec((None, l_pad, ATT_KV * ATT_HD), lambda b, i: (b, 0, 0)),
                  pl.BlockSpec((None, ATT_KV * ATT_HD, l_pad), lambda b, i: (b, 0, 0)),
                  pl.BlockSpec((None, l_pad, LANES), lambda b, i: (b, 0, 0))],
        out_specs=pl.BlockSpec((qb, ATT_W), lambda b, i: (b * nqb + i, 0)),
        out_shape=jax.ShapeDtypeStruct((out_rows or nb * nqb * qb, ATT_W), BF16),
        scratch_shapes=[pltpu.VMEM((l_pad, 2 * LANES), BF16),
                        pltpu.VMEM((l_pad, qb), I32),
                        pltpu.VMEM((l_pad, qb), I16),
                        pltpu.VMEM((l_pad, qb), I16),
                        pltpu.VMEM((2 * LANES, IDX_HEADS * qb), BF16),
                        pltpu.VMEM((ATT_KV, ATT_HD, ATT_REP * qb), BF16),
                        pltpu.VMEM((ATT_KV, 1, ATT_REP * qb), F32),
                        pltpu.VMEM((ATT_KV, 1, ATT_REP * qb), F32),
                        pltpu.VMEM((ATT_KV, ATT_HD, ATT_REP * qb), F32),
                        pltpu.VMEM((ATT_KV, kblk, ATT_REP * qb), F32),
                        pltpu.VMEM((ATT_KV, kblk, ATT_REP * qb), BF16),
                        pltpu.VMEM((kblk, qb), F32),
                        pltpu.VMEM((ATT_KV, 1, ATT_REP * qb), F32)],
        compiler_params=_cparams(("parallel", "arbitrary")),
        name="dsa",
    )(qa, qi, wi, k_all, vt_all, ki_all)


def _rw_prep_kernel(r_ref, k_ref, v_ref, lr_ref, hr_ref, hk_ref, hv_ref, hlr_ref,
                    sr_ref, sk_ref, sv_ref, slr_ref, mr_ref, mk_ref, mv_ref, mlr_ref,
                    w0_ref, a0_ref, w2_ref, a2_ref, g2_ref,
                    ro_ref, ko_ref, vo_ref, wo_ref, ao_ref, go_ref, *, tiles_per_seq, time_minor):
    first_tile = (pl.program_id(0) % tiles_per_seq) == 0
    put = (lambda t: t.T) if time_minor else (lambda t: t)

    def mix(x_ref, halo_ref, shift_ref, mu_ref):
        x = x_ref[...]
        first = jnp.where(first_tile, shift_ref[...], halo_ref[SUBLANES - 1:SUBLANES, :])
        rows = lax.broadcasted_iota(I32, x.shape, 0)
        prev = jnp.where(rows == 0, first, pltpu.roll(x, 1, 0))
        return x + (prev - x) * mu_ref[...]

    ro_ref[...] = put(mix(r_ref, hr_ref, sr_ref, mr_ref))
    ko_ref[...] = put(mix(k_ref, hk_ref, sk_ref, mk_ref))
    vo_ref[...] = put(mix(v_ref, hv_ref, sv_ref, mv_ref))
    lr = mix(lr_ref, hlr_ref, slr_ref, mlr_ref)
    xw = lr[:, 0:128]
    xa = lr[:, 128:256]
    xg = lr[:, 256:512]
    wpre = w0_ref[...] + _dot3(jnp.tanh(xw), w2_ref[...])
    wo_ref[...] = put(jnp.exp(-RW_DECAY_SCALE * _sigmoid(wpre)))
    ao_ref[...] = put(_sigmoid(a0_ref[...] + _dot3(xa, a2_ref[...])))
    go_ref[...] = _dot3(_sigmoid(xg), g2_ref[...])


def _rw_prep(proj, row0, nb, seq, shifts, mus, w0, a0, w2p, a2p, g2, tm, out_rows=None):
    m = nb * seq
    tps = seq // tm
    rb0 = row0 // tm
    hb = tm // SUBLANES
    cr, ck, cv = _OFF["r"] // RW_W, _OFF["k"] // RW_W, _OFF["v"] // RW_W
    clr = _OFF["xw"] // 512

    def main(c):
        return lambda i: (rb0 + i, c)

    def halo(c):
        return lambda i: (jnp.maximum((rb0 + i) * hb - 1, 0), c)

    bidx = lambda i: (i // tps, 0, 0)
    fix = lambda i: (0, 0)
    wide = [pl.BlockSpec((tm, RW_W), main(cr)), pl.BlockSpec((tm, RW_W), main(ck)),
            pl.BlockSpec((tm, RW_W), main(cv)), pl.BlockSpec((tm, 512), main(clr)),
            pl.BlockSpec((SUBLANES, RW_W), halo(cr)), pl.BlockSpec((SUBLANES, RW_W), halo(ck)),
            pl.BlockSpec((SUBLANES, RW_W), halo(cv)), pl.BlockSpec((SUBLANES, 512), halo(clr))]
    sh = [pl.BlockSpec((None, 1, RW_W), bidx)] * 3 + [pl.BlockSpec((None, 1, 512), bidx)]
    mu = [pl.BlockSpec((1, RW_W), fix)] * 3 + [pl.BlockSpec((1, 512), fix)]
    par = [pl.BlockSpec((1, RW_W), fix), pl.BlockSpec((1, RW_W), fix),
           pl.BlockSpec((128, RW_W), fix), pl.BlockSpec((128, RW_W), fix),
           pl.BlockSpec((RW_G_R, RW_W), fix)]
    token_major = pl.BlockSpec((tm, RW_W), lambda i: (i, 0))
    time_minor = _relayout_ok(nb, seq) and tm % LANES == 0
    if time_minor:
        out = jax.ShapeDtypeStruct((nb, RW_W, seq), F32)
        seq_spec = pl.BlockSpec((None, RW_W, tm), lambda i: (i // tps, 0, i % tps))
    else:
        out = jax.ShapeDtypeStruct((m, RW_W), F32)
        seq_spec = token_major
    return pl.pallas_call(
        functools.partial(_rw_prep_kernel, tiles_per_seq=tps, time_minor=time_minor),
        grid=(m // tm,),
        in_specs=wide + sh + mu + par,
        out_specs=[seq_spec] * 5 + [token_major],
        out_shape=[out] * 5 + [jax.ShapeDtypeStruct((out_rows or m, RW_W), F32)],
        compiler_params=_cparams(("parallel",)),
        name="rwkv_prep",
    )(proj, proj, proj, proj, proj, proj, proj, proj, *shifts, *mus, w0, a0, w2p, a2p, g2)


def _rw_scan_kernel(r_ref, k_ref, v_ref, w_ref, a_ref, kkp_ref, kap_ref, rkp_ref,
                    lng_ref, lnb_ref, s0_ref, o_ref, st_ref, s_scr, tmp_scr):
    tb = pl.program_id(1)
    nsteps = r_ref.shape[0]

    @pl.when(tb == 0)
    def _():
        s_scr[...] = s0_ref[...]

    def step(t, d_prev):
        k = k_ref[t]
        a = a_ref[t]
        r = r_ref[t]
        kk = k * kkp_ref[...]
        nrm = jnp.sqrt(jnp.sum(kk * kk, axis=0, keepdims=True))
        kk = kk / jnp.maximum(nrm, 1e-12)
        km = k * (1.0 + (a - 1.0) * kap_ref[...])
        d = d_prev * w_ref[t]
        inv_d = 1.0 / d
        tmp_scr[0] = kk * d_prev
        tmp_scr[1] = kk * a * inv_d
        tmp_scr[2] = km * inv_d
        tmp_scr[3] = r * d
        parts = [jnp.zeros((RW_HD, LANES), F32) for _ in range(2)]
        for kx in range(RW_HD):
            parts[kx % 2] = parts[kx % 2] + s_scr[kx] * tmp_scr[0, kx:kx + 1, :]
        sa = -(parts[0] + parts[1])
        v = v_ref[t]
        outs = [jnp.zeros((RW_HD, LANES), F32) for _ in range(2)]
        for kx in range(RW_HD):
            s_new = s_scr[kx] + (sa * tmp_scr[1, kx:kx + 1, :] + v * tmp_scr[2, kx:kx + 1, :])
            s_scr[kx] = s_new
            outs[kx % 2] = outs[kx % 2] + s_new * tmp_scr[3, kx:kx + 1, :]
        o = outs[0] + outs[1]
        mu = jnp.mean(o, axis=0, keepdims=True)
        oc = o - mu
        var = jnp.mean(oc * oc, axis=0, keepdims=True)
        on = oc * lax.rsqrt(var + RW_GN_EPS) * lng_ref[...] + lnb_ref[...]
        bonus = jnp.sum(r * km * rkp_ref[...], axis=0, keepdims=True) * v
        o_ref[t] = on + bonus
        return d
    d_end = lax.fori_loop(0, nsteps, step, jnp.ones((RW_HD, LANES), F32))
    tmp_scr[0] = d_end
    for kx in range(RW_HD):
        s_scr[kx] = s_scr[kx] * tmp_scr[0, kx:kx + 1, :]

    @pl.when(tb == pl.num_programs(1) - 1)
    def _():
        st_ref[...] = s_scr[...]


def _rw_scan(r, k, v, w, a, params, s0, tb):
    seq, _, pairs = r.shape
    assert pairs % LANES == 0, "batch * RW_HEADS must fill whole lane groups"
    seqb = pl.BlockSpec((tb, RW_HD, LANES), lambda g, t: (t, 0, g))
    par = pl.BlockSpec((RW_HD, LANES), lambda g, t: (0, 0))
    st = pl.BlockSpec((RW_HD, RW_HD, LANES), lambda g, t: (0, 0, g))
    return pl.pallas_call(
        _rw_scan_kernel,
        grid=(pairs // LANES, seq // tb),
        in_specs=[seqb] * 5 + [par] * 5 + [st],
        out_specs=[seqb, st],
        out_shape=[jax.ShapeDtypeStruct((seq, RW_HD, pairs), F32),
                   jax.ShapeDtypeStruct((RW_HD, RW_HD, pairs), F32)],
        scratch_shapes=[pltpu.VMEM((RW_HD, RW_HD, LANES), F32),
                        pltpu.VMEM((4, RW_HD, LANES), F32)],
        compiler_params=_cparams(("parallel", "arbitrary")),
        name="rwkv_scan",
    )(r, k, v, w, a, *params, s0)


GLA_SUB = 16


def _gla_kernel(q_ref, k_ref, v_ref, gl_ref, og_ref, w2_ref, gb_ref, ng_ref, s0_ref,
                o_ref, st_ref, s_scr, la_scr, att_scr):
    tblk = pl.program_id(1)
    tc = q_ref.shape[0]

    @pl.when(tblk == 0)
    def _():
        for h in range(GLA_HEADS):
            s_scr[h] = s0_ref[h].T

    la_scr[...] = _log_sigmoid(_dot3(gl_ref[...], w2_ref[...]) + gb_ref[...]) * (1.0 / GLA_GATE_NORM)
    rows = lax.broadcasted_iota(I32, (CHUNK, GLA_DK), 0)
    lane = lax.broadcasted_iota(I32, (GLA_SUB, CHUNK), 1)
    jrow = lax.broadcasted_iota(I32, (GLA_SUB, GLA_DK), 0)
    nsub = CHUNK // GLA_SUB

    def head_chunk(h, rs):
        ks = slice(h * GLA_DK, (h + 1) * GLA_DK)
        vs = slice(h * GLA_DV, (h + 1) * GLA_DV)
        b = la_scr[rs, ks]
        sh = 1
        while sh < CHUNK:
            b = b + jnp.where(rows >= sh, pltpu.roll(b, sh, 0), 0.0)
            sh *= 2
        q = q_ref[rs, ks] * (GLA_DK ** -0.5)
        k = k_ref[rs, ks]
        v = v_ref[rs, vs]
        s_old = s_scr[h]
        inter = _nt_dot(q * jnp.exp(b), s_old)
        b_last = b[CHUNK - 1:CHUNK, :]

        for jb in range(nsub):
            js = slice(jb * GLA_SUB, (jb + 1) * GLA_SUB)
            bj = b[js]
            kj = k[js]
            blk = jnp.zeros((GLA_SUB, CHUNK), F32)
            for ii in range(GLA_SUB):
                i_row = jb * GLA_SUB + ii
                e = jnp.exp(jnp.minimum(b[i_row:i_row + 1, :] - bj, 0.0))
                val = jnp.where(jrow <= ii, q[i_row:i_row + 1, :] * kj * e, 0.0)
                col = jnp.sum(val, axis=1, keepdims=True)
                blk = jnp.where(lane == i_row, col, blk)
            if jb + 1 < nsub:
                i0 = (jb + 1) * GLA_SUB
                be = b[i0 - 1:i0, :]
                kt = kj * jnp.exp(be - bj)
                later = rows >= i0
                qt = jnp.where(later, q * jnp.exp(jnp.where(later, b - be, 0.0)), 0.0)
                blk = blk + _nt_dot(kt, qt)
            att_scr[h, js, :] = blk
        o = inter + _tn_dot(att_scr[h], v)
        khat = k * jnp.exp(b_last - b)
        s_scr[h] = s_old * jnp.exp(b_last) + _tn_dot(v, khat)
        o = o * lax.rsqrt(jnp.mean(o * o, -1, keepdims=True) + GLA_NORM_EPS) * ng_ref[:, vs]
        og = og_ref[rs, vs]
        o_ref[rs, vs] = (o * (og * _sigmoid(og))).astype(BF16)

    def chunk_body(c, carry):
        rs = pl.ds(pl.multiple_of(c * CHUNK, CHUNK), CHUNK)
        for h in range(GLA_HEADS):
            head_chunk(h, rs)
        return carry
    lax.fori_loop(0, tc // CHUNK, chunk_body, 0)

    @pl.when(tblk == pl.num_programs(1) - 1)
    def _():
        for h in range(GLA_HEADS):
            st_ref[h] = s_scr[h].T


def _gla(proj, tail, row0, nb, seq, w2p, gb, ng, s0, tc, out_rows=None):
    ntb = seq // tc
    rb0 = row0 // tc
    cq, ck = _OFF["qc"] // GLA_KW, _OFF["kc"] // GLA_KW
    cv, cg = _OFF["vc"] // GLA_VW, _OFF["og"] // GLA_VW
    cgl = (_OFF["gl"] - _OFF["ka"]) // LANES

    def rows(c0):
        return lambda b, t: (rb0 + b * ntb + t, c0)

    fix = lambda b, t: (0, 0)
    st = pl.BlockSpec((None, GLA_HEADS, GLA_DK, GLA_DV), lambda b, t: (b, 0, 0, 0))
    return pl.pallas_call(
        _gla_kernel,
        grid=(nb, ntb),
        in_specs=[pl.BlockSpec((tc, GLA_KW), rows(cq)), pl.BlockSpec((tc, GLA_KW), rows(ck)),
                  pl.BlockSpec((tc, GLA_VW), rows(cv)), pl.BlockSpec((tc, LANES), rows(cgl)),
                  pl.BlockSpec((tc, GLA_VW), rows(cg)),
                  pl.BlockSpec((LANES, GLA_KW), fix), pl.BlockSpec((1, GLA_KW), fix),
                  pl.BlockSpec((1, GLA_VW), fix), st],
        out_specs=[pl.BlockSpec((tc, GLA_VW), lambda b, t: (b * ntb + t, 0)), st],
        out_shape=[jax.ShapeDtypeStruct((out_rows or nb * seq, GLA_VW), BF16),
                   jax.ShapeDtypeStruct((nb, GLA_HEADS, GLA_DK, GLA_DV), F32)],
        scratch_shapes=[pltpu.VMEM((GLA_HEADS, GLA_DV, GLA_DK), F32), pltpu.VMEM((tc, GLA_KW), F32),
                        pltpu.VMEM((GLA_HEADS, CHUNK, CHUNK), F32)],
        compiler_params=_cparams(("parallel", "arbitrary")),
        name="gla",
    )(proj, proj, proj, tail, proj, w2p, gb, ng, s0)


_MAIN_SEGS = tuple(sg for sg in _SEGS if _OFF[sg[0]] < _OFF["ka"])
_TAIL_SEGS = tuple(sg for sg in _SEGS if _OFF[sg[0]] >= _OFF["ka"])


def _pack_cols(w, segs=_SEGS, dtype=None):
    dtype = dtype or w.dtype
    parts = []
    for _, s, wd, p in segs:
        parts.append(w[..., s:s + wd].astype(dtype))
        if p > wd:
            parts.append(jnp.zeros(w.shape[:-1] + (p - wd,), dtype))
    return jnp.concatenate(parts, axis=-1)


def _pack_rw(vec):
    r, k, v = vec[..., 0:1024], vec[..., 1024:2048], vec[..., 2048:3072]
    z = jnp.zeros(vec.shape[:-1] + (32,), vec.dtype)
    lr = jnp.concatenate([vec[..., 3072:3168], z, vec[..., 3168:3264], z, vec[..., 3264:3520]], -1)
    return r, k, v, lr


def _pad_rows(w, n):
    return jnp.concatenate([w, jnp.zeros((n - w.shape[0],) + w.shape[1:], w.dtype)], 0)


RELAYOUT_T = 128


def _to_scan_kernel(y_ref, o_ref):
    nbat = y_ref.shape[0]
    for n in range(RW_HD):
        z = jnp.concatenate([y_ref[b, pl.ds(n, RW_HEADS, stride=RW_HD), :] for b in range(nbat)], axis=0)
        o_ref[:, n, :] = z.T


def _from_scan_kernel(o_ref, x_ref, y_scr):
    nbat = x_ref.shape[0]
    for n in range(RW_HD):
        zt = o_ref[:, n, :].T
        for b in range(nbat):
            y_scr[b, pl.ds(n, RW_HEADS, stride=RW_HD), :] = zt[b * RW_HEADS:(b + 1) * RW_HEADS]
    for b in range(nbat):
        for c in range(RW_W // LANES):
            x_ref[b, :, c * LANES:(c + 1) * LANES] = y_scr[b, c * LANES:(c + 1) * LANES, :].T


def _relayout_ok(nb, seq):
    return nb * RW_HEADS == LANES and seq % RELAYOUT_T == 0


def _to_scan(x, nb, seq):
    if x.ndim == 2:
        return jnp.transpose(x.reshape(nb, seq, RW_HEADS, RW_HD), (1, 3, 0, 2)).reshape(seq, RW_HD, nb * RW_HEADS)
    return pl.pallas_call(
        _to_scan_kernel,
        grid=(seq // RELAYOUT_T,),
        in_specs=[pl.BlockSpec((nb, RW_W, RELAYOUT_T), lambda t: (0, 0, t))],
        out_specs=pl.BlockSpec((RELAYOUT_T, RW_HD, LANES), lambda t: (t, 0, 0)),
        out_shape=jax.ShapeDtypeStruct((seq, RW_HD, LANES), F32),
        compiler_params=_cparams(("parallel",)),
        name="to_scan_layout",
    )(x)


def _from_scan(o, nb, seq):
    if not _relayout_ok(nb, seq):
        return jnp.transpose(o.reshape(seq, RW_HD, nb, RW_HEADS), (2, 0, 3, 1)).reshape(nb * seq, RW_W)
    return pl.pallas_call(
        _from_scan_kernel,
        grid=(seq // RELAYOUT_T,),
        in_specs=[pl.BlockSpec((RELAYOUT_T, RW_HD, LANES), lambda t: (t, 0, 0))],
        out_specs=pl.BlockSpec((nb, RELAYOUT_T, RW_W), lambda t: (0, t, 0)),
        out_shape=jax.ShapeDtypeStruct((nb, seq, RW_W), F32),
        scratch_shapes=[pltpu.VMEM((nb, RW_W, RELAYOUT_T), F32)],
        compiler_params=_cparams(("parallel",)),
        name="from_scan_layout",
    )(o).reshape(nb * seq, RW_W)


def _group_mixers(proj, tail, row0, nb, seq, caches, lw, out_rows=None):
    m = nb * seq
    cache_k, cache_v, cache_ki, s_rw, s_shift, s_gla = caches
    past = 0 if cache_k is None else cache_k.shape[1]
    l_real = past + seq
    t0 = _OFF["ka"]
    grp = tail[row0:row0 + m].reshape(nb, seq, -1)

    ka = grp[..., _OFF["ka"] - t0:_OFF["ka"] - t0 + 256]
    va = grp[..., _OFF["va"] - t0:_OFF["va"] - t0 + 256]
    ki = grp[..., _OFF["ki"] - t0:_OFF["ki"] - t0 + LANES]
    if past:
        k_all = jnp.concatenate([cache_k.reshape(nb, past, 256), ka], 1)
        v_all = jnp.concatenate([cache_v.reshape(nb, past, 256), va], 1)
        cki = jnp.concatenate([cache_ki, jnp.zeros((nb, past, LANES - IDX_DIM), F32)], -1)
        ki_all = jnp.concatenate([cki, ki], 1)
    else:
        k_all, v_all, ki_all = ka, va, ki
    kblk = 512 if l_real % 512 == 0 else 384
    l_pad = -(-l_real // kblk) * kblk
    if l_pad > l_real:
        padk = lambda t: jnp.concatenate([t, jnp.zeros((nb, l_pad - l_real, t.shape[-1]), t.dtype)], 1)
        k_all, v_all, ki_all = padk(k_all), padk(v_all), padk(ki_all)
    k_bf = k_all.astype(BF16)
    vt_bf = jnp.swapaxes(v_all, 1, 2).astype(BF16)
    qb = DSA_QB if seq % DSA_QB == 0 else LANES
    if seq % qb == 0:
        nqb = seq // qb
        rb0 = row0 // qb
        mk = lambda c: (lambda b, i: (rb0 + b * nqb + i, c))
        q_ops = ((proj, mk(_OFF["qa"] // ATT_W)), (proj, mk(_OFF["qi"] // 1024)),
                 (tail, mk((_OFF["wi"] - t0) // LANES)))
        o_a = _dsa(q_ops, k_bf, vt_bf, ki_all, nb=nb, nqb=nqb, qb=qb, past=past, l_real=l_real, kblk=kblk,
                   out_rows=out_rows)
    else:
        def padq(t):
            return jnp.concatenate([t, jnp.zeros((nb, qb - seq, t.shape[-1]), F32)], 1).reshape(nb * qb, -1)
        qrows = proj[row0:row0 + m, 0:_OFF["vc"]].reshape(nb, seq, -1)
        mk = lambda b, i: (b, 0)
        q_ops = ((padq(qrows[..., 0:ATT_W]), mk), (padq(qrows[..., ATT_W:]), mk),
                 (padq(grp[..., _OFF["wi"] - t0:_OFF["wi"] - t0 + LANES]), mk))
        o_a = _dsa(q_ops, k_bf, vt_bf, ki_all, nb=nb, nqb=1, qb=qb, past=past, l_real=l_real, kblk=kblk)
        o_a = o_a.reshape(nb, qb, ATT_W)[:, :seq].reshape(m, ATT_W)

    if s_shift is None:
        s_shift = jnp.zeros((nb, RW_PROJ), F32)
    shifts = [t.reshape(nb, 1, -1) for t in _pack_rw(s_shift)]
    tm = 256 if seq % 256 == 0 else seq
    r, k, v, w, a, g = _rw_prep(proj, row0, nb, seq, shifts, lw["rw_mus"], lw["rw_w0"], lw["rw_a0"],
                                lw["rw_w2p"], lw["rw_a2p"], lw["rw_g2"], tm, out_rows)
    if s_rw is None:
        s0 = jnp.zeros((RW_HD, RW_HD, nb * RW_HEADS), F32)
    else:
        s0 = jnp.transpose(s_rw, (3, 2, 0, 1)).reshape(RW_HD, RW_HD, nb * RW_HEADS)
    tile = lambda p: jnp.tile(p.reshape(RW_HEADS, RW_HD).T, (1, LANES // RW_HEADS))
    params = [tile(lw[n]) for n in ("rw_kk", "rw_ka", "rw_rk", "rw_ln_g", "rw_ln_b")]
    tb = 32 if seq % 32 == 0 else seq
    o_b, s_t = _rw_scan(*[_to_scan(t, nb, seq) for t in (r, k, v, w, a)], params, s0, tb)
    o_b = _from_scan(o_b, nb, seq)
    s_rw_new = jnp.transpose(s_t.reshape(RW_HD, RW_HD, nb, RW_HEADS), (2, 3, 1, 0))
    last = jnp.take(proj, row0 + seq - 1 + seq * jnp.arange(nb), axis=0)[:, _OFF["r"]:_OFF["qc"]]
    base = _OFF["r"]
    shift_new = jnp.concatenate(
        [last[:, 0:3072], last[:, _OFF["xw"] - base:_OFF["xw"] - base + 96],
         last[:, _OFF["xa"] - base:_OFF["xa"] - base + 96], last[:, _OFF["xg"] - base:]], -1)

    if s_gla is None:
        s_gla = jnp.zeros((nb, GLA_HEADS, GLA_DK, GLA_DV), F32)
    tc = 512 if seq % 512 == 0 else seq
    o_c, s_gla_new = _gla(proj, tail, row0, nb, seq, lw["gla_w2p"], lw["gla_b"], lw["gla_ng"], s_gla, tc,
                          out_rows)

    new_state = (ka.reshape(nb, seq, ATT_KV, ATT_HD), va.reshape(nb, seq, ATT_KV, ATT_HD),
                 ki[..., :IDX_DIM], s_rw_new, shift_new, s_gla_new)
    return o_a, o_b, g, o_c, new_state


def kernel(x_prompt, x_sample, cache_k, cache_v, cache_kidx, state_rwkv, state_rwkv_shift, state_gla,
           w_in, rw_mu, rw_w0, rw_w2, rw_a0, rw_a2, rw_g2, rw_kk, rw_ka, rw_rk, rw_ln_g, rw_ln_b,
           gla_w2, gla_b, gla_norm_g, w_branch_a, w_branch_b, w_branch_c, w_out,
           ln1_g, ln1_b, ln2_g, ln2_b, ffn_w1, ffn_w3, ffn_w2):
    bp, tp, _ = x_prompt.shape
    bs, ts, _ = x_sample.shape
    mp, ms = bp * tp, bs * ts
    x = jnp.concatenate([x_prompt.reshape(mp, D_MODEL), x_sample.reshape(ms, D_MODEL)], 0)
    xb = x.astype(BF16)
    m = mp + ms
    tm_big = 1024 if m % 1024 == 0 else 256
    tm_mid = 512 if m % 512 == 0 else 256

    new_p = [[] for _ in range(6)]
    new_s = [[] for _ in range(6)]
    for l in range(DEPTH):
        lw = {
            "rw_mus": [t.reshape(1, -1) for t in _pack_rw(rw_mu[l])],
            "rw_w0": rw_w0[l].reshape(1, RW_W), "rw_a0": rw_a0[l].reshape(1, RW_W),
            "rw_w2p": _pad_rows(rw_w2[l], 128), "rw_a2p": _pad_rows(rw_a2[l], 128), "rw_g2": rw_g2[l],
            "rw_kk": rw_kk[l], "rw_ka": rw_ka[l], "rw_rk": rw_rk[l],
            "rw_ln_g": rw_ln_g[l], "rw_ln_b": rw_ln_b[l],
            "gla_w2p": _pad_rows(gla_w2[l], LANES), "gla_b": gla_b[l].reshape(1, GLA_KW),
            "gla_ng": gla_norm_g[l].reshape(1, GLA_VW),
        }
        proj = _matmul(xb, _pack_cols(w_in[l], _MAIN_SEGS, BF16), tm_big, 1024)
        tail = _matmul(xb, _pack_cols(w_in[l], _TAIL_SEGS, BF16), tm_big, PROJ_W - _OFF["ka"])
        oa_p, ob_p, g_p, oc_p, st_p = _group_mixers(proj, tail, 0, bp, tp, (None,) * 6, lw, out_rows=m)
        oa_s, ob_s, g_s, oc_s, st_s = _group_mixers(
            proj, tail, mp, bs, ts,
            (cache_k[l], cache_v[l], cache_kidx[l], state_rwkv[l], state_rwkv_shift[l], state_gla[l]), lw)
        put = lambda full, part: lax.dynamic_update_slice(full, part, (mp, 0))
        merged = _merge(put(oa_p, oa_s), jnp.concatenate([ob_p, ob_s], 0), put(g_p, g_s), put(oc_p, oc_s), proj,
                        w_branch_a[l].astype(BF16), w_branch_b[l].astype(BF16),
                        w_branch_c[l].astype(BF16), 256, D_MODEL)
        h, hb = _hproj(merged, w_out[l].astype(BF16), x, ln1_g[l].reshape(1, -1),
                       ln1_b[l].reshape(1, -1), tm_mid)
        x, xb = _ffn(hb, h, ffn_w1[l].astype(BF16), ffn_w3[l].astype(BF16), ffn_w2[l].astype(BF16),
                     ln2_g[l].reshape(1, -1), ln2_b[l].reshape(1, -1), tm_mid, 512)
        for i in range(6):
            new_p[i].append(st_p[i])
            new_s[i].append(st_s[i])
    sp = [jnp.stack(a) for a in new_p]
    ss = [jnp.stack(a) for a in new_s]
    yp = x[:mp].reshape(bp, tp, D_MODEL)
    ys = x[mp:].reshape(bs, ts, D_MODEL)
    return (yp, ys, sp[0], sp[1], sp[2], sp[3], sp[4], sp[5], ss[0], ss[1], ss[2], ss[3], ss[4], ss[5])
```

```python
import functools

import jax
import jax.numpy as jnp
from jax import lax
from jax.experimental import pallas as pl
from jax.experimental.pallas import tpu as pltpu

F32 = jnp.float32
BF16 = jnp.bfloat16
I32 = jnp.int32
I16 = jnp.int16

D_MODEL = 2048
DEPTH = 4
CHUNK = 64
ATT_HEADS = 8
ATT_HD = 128
ATT_KV = 2
ATT_REP = ATT_HEADS // ATT_KV
ATT_W = ATT_HEADS * ATT_HD
IDX_HEADS = 16
IDX_DIM = 64
IDX_TOPK = 256
IDX_SCALE = (IDX_DIM * IDX_HEADS) ** -0.5
RW_HEADS = 16
RW_HD = 64
RW_W = RW_HEADS * RW_HD
RW_DECAY_R = 96
RW_A_R = 96
RW_G_R = 256
RW_PROJ = 3 * RW_W + RW_DECAY_R + RW_A_R + RW_G_R
RW_DECAY_SCALE = 0.606531
RW_GN_EPS = 64e-5
GLA_HEADS = 4
GLA_DK = 128
GLA_DV = 256
GLA_KW = GLA_HEADS * GLA_DK
GLA_VW = GLA_HEADS * GLA_DV
GLA_GATE_R = 16
GLA_GATE_NORM = 16.0
GLA_NORM_EPS = 1e-5
FFN_HIDDEN = -(-8 * D_MODEL // (3 * 256)) * 256
DN_ALPHA = (2 * DEPTH) ** 0.25
LN_EPS = 1e-5
LOG2E = 1.4426950408889634

LANES = 128
SUBLANES = 8
PACKED_ROWS = 16
VMEM_LIMIT = 56 * 1024 * 1024

_SEGS = (
    ("qa", 0, 1024, 1024), ("qi", 1536, 1024, 1024), ("vc", 7184, 1024, 1024),
    ("og", 8224, 1024, 1024), ("gates", 9248, 6144, 6144),
    ("r", 2640, 1024, 1024), ("k", 3664, 1024, 1024), ("v", 4688, 1024, 1024),
    ("xw", 5712, 96, 128), ("xa", 5808, 96, 128), ("xg", 5904, 256, 256),
    ("qc", 6160, 512, 512), ("kc", 6672, 512, 512),
    ("ka", 1024, 256, 256), ("va", 1280, 256, 256),
    ("ki", 2560, 64, 128), ("wi", 2624, 16, 128), ("gl", 8208, 16, 128),
)
_OFF = {}
_o = 0
for _n, _s, _w, _p in _SEGS:
    _OFF[_n] = _o
    _o += _p
PROJ_W = _o
KEY_NEG_INF = -0x7F800000
MASK_DIST = 1e32
M_FLOOR = -1e29


def _cparams(sem):
    return pltpu.CompilerParams(dimension_semantics=sem, vmem_limit_bytes=VMEM_LIMIT)


def _split2(x):
    hi = x.astype(BF16)
    lo = (x - hi.astype(F32)).astype(BF16)
    return hi, lo


def _dot3(a, b):
    ah, al = _split2(a)
    bh, bl = _split2(b)
    d = functools.partial(jnp.dot, preferred_element_type=F32)
    return d(ah, bh) + (d(al, bh) + d(ah, bl))


def _nt_dot(a, b):
    return lax.dot_general(a.astype(BF16), b.astype(BF16), (((1,), (1,)), ((), ())),
                           preferred_element_type=F32)


def _tn_dot(a, b):
    return lax.dot_general(a.astype(BF16), b.astype(BF16), (((0,), (0,)), ((), ())),
                           preferred_element_type=F32)


def _sigmoid(x):
    return 1.0 / (1.0 + jnp.exp(-x))


def _log_sigmoid(x):
    return jnp.minimum(x, 0.0) - jnp.log(1.0 + jnp.exp(-jnp.abs(x)))


def _layer_norm(x, g, b):
    mu = jnp.mean(x, -1, keepdims=True)
    xc = x - mu
    var = jnp.mean(xc * xc, -1, keepdims=True)
    return xc * lax.rsqrt(var + LN_EPS) * g + b


def _mm_kernel(a_ref, b_ref, o_ref):
    o_ref[...] = jnp.dot(a_ref[...], b_ref[...], preferred_element_type=F32)


def _matmul(a, b, tm, tn):
    m, k = a.shape
    n = b.shape[1]
    return pl.pallas_call(
        _mm_kernel,
        grid=(m // tm, pl.cdiv(n, tn)),
        in_specs=[pl.BlockSpec((tm, k), lambda i, j: (i, 0)),
                  pl.BlockSpec((k, tn), lambda i, j: (0, j))],
        out_specs=pl.BlockSpec((tm, tn), lambda i, j: (i, j)),
        out_shape=jax.ShapeDtypeStruct((m, n), F32),
        compiler_params=_cparams(("parallel", "arbitrary")),
        name="in_proj",
    )(a, b)


def _merge_kernel(oa_ref, ob_ref, g_ref, oc_ref, ga_ref, gb_ref, gc_ref,
                  wa_ref, wb_ref, wc_ref, o_ref):
    ob = (ob_ref[...] * g_ref[...]).astype(BF16)
    d = functools.partial(jnp.dot, preferred_element_type=F32)
    acc = _sigmoid(ga_ref[...]) * d(oa_ref[...], wa_ref[...])
    acc += _sigmoid(gb_ref[...]) * d(ob, wb_ref[...])
    acc += _sigmoid(gc_ref[...]) * d(oc_ref[...], wc_ref[...])
    o_ref[...] = acc.astype(BF16)


def _merge(oa, ob, g, oc, proj, wa, wb, wc, tm, tn):
    m = oa.shape[0]
    gblk = _OFF["gates"] // tn
    nj = D_MODEL // tn
    row = lambda i, j: (i, 0)
    wspec = pl.BlockSpec((ATT_W, tn), lambda i, j: (0, j))
    return pl.pallas_call(
        _merge_kernel,
        grid=(m // tm, nj),
        in_specs=[pl.BlockSpec((tm, ATT_W), row), pl.BlockSpec((tm, RW_W), row),
                  pl.BlockSpec((tm, RW_W), row), pl.BlockSpec((tm, GLA_VW), row),
                  pl.BlockSpec((tm, tn), lambda i, j: (i, gblk + j)),
                  pl.BlockSpec((tm, tn), lambda i, j: (i, gblk + nj + j)),
                  pl.BlockSpec((tm, tn), lambda i, j: (i, gblk + 2 * nj + j)),
                  wspec, wspec, wspec],
        out_specs=pl.BlockSpec((tm, tn), lambda i, j: (i, j)),
        out_shape=jax.ShapeDtypeStruct((m, D_MODEL), BF16),
        compiler_params=_cparams(("parallel", "arbitrary")),
        name="merge",
    )(oa, ob, g, oc, proj, proj, proj, wa, wb, wc)


def _hproj_kernel(m_ref, w_ref, x_ref, g_ref, b_ref, h_ref, hb_ref):
    z = DN_ALPHA * x_ref[...] + jnp.dot(m_ref[...], w_ref[...], preferred_element_type=F32)
    h = _layer_norm(z, g_ref[...], b_ref[...])
    h_ref[...] = h
    hb_ref[...] = h.astype(BF16)


def _hproj(merged, w_out, x, g, b, tm):
    m = x.shape[0]
    row = lambda i: (i, 0)
    fix = lambda i: (0, 0)
    return pl.pallas_call(
        _hproj_kernel,
        grid=(m // tm,),
        in_specs=[pl.BlockSpec((tm, D_MODEL), row), pl.BlockSpec((D_MODEL, D_MODEL), fix),
                  pl.BlockSpec((tm, D_MODEL), row), pl.BlockSpec((1, D_MODEL), fix),
                  pl.BlockSpec((1, D_MODEL), fix)],
        out_specs=[pl.BlockSpec((tm, D_MODEL), row), pl.BlockSpec((tm, D_MODEL), row)],
        out_shape=[jax.ShapeDtypeStruct((m, D_MODEL), F32),
                   jax.ShapeDtypeStruct((m, D_MODEL), BF16)],
        compiler_params=_cparams(("parallel",)),
        name="out_proj_ln",
    )(merged, w_out, x, g, b)


def _ffn_kernel(hb_ref, h_ref, w1_ref, w3_ref, w2_ref, g_ref, b_ref, y_ref, yb_ref, acc_ref):
    j = pl.program_id(1)

    @pl.when(j == 0)
    def _():
        acc_ref[...] = jnp.zeros_like(acc_ref)

    hb = hb_ref[...]
    u1 = jnp.dot(hb, w1_ref[...], preferred_element_type=F32)
    u3 = jnp.dot(hb, w3_ref[...], preferred_element_type=F32)
    u = (u1 * _sigmoid(u1) * u3).astype(BF16)
    acc_ref[...] += jnp.dot(u, w2_ref[...], preferred_element_type=F32)

    @pl.when(j == pl.num_programs(1) - 1)
    def _():
        y = _layer_norm(DN_ALPHA * h_ref[...] + acc_ref[...], g_ref[...], b_ref[...])
        y_ref[...] = y
        yb_ref[...] = y.astype(BF16)


def _ffn(hb, h, w1, w3, w2, g, b, tm, th):
    m = h.shape[0]
    row = lambda i, j: (i, 0)
    fix = lambda i, j: (0, 0)
    return pl.pallas_call(
        _ffn_kernel,
        grid=(m // tm, FFN_HIDDEN // th),
        in_specs=[pl.BlockSpec((tm, D_MODEL), row), pl.BlockSpec((tm, D_MODEL), row),
                  pl.BlockSpec((D_MODEL, th), lambda i, j: (0, j)),
                  pl.BlockSpec((D_MODEL, th), lambda i, j: (0, j)),
                  pl.BlockSpec((th, D_MODEL), lambda i, j: (j, 0)),
                  pl.BlockSpec((1, D_MODEL), fix), pl.BlockSpec((1, D_MODEL), fix)],
        out_specs=[pl.BlockSpec((tm, D_MODEL), row), pl.BlockSpec((tm, D_MODEL), row)],
        out_shape=[jax.ShapeDtypeStruct((m, D_MODEL), F32),
                   jax.ShapeDtypeStruct((m, D_MODEL), BF16)],
        scratch_shapes=[pltpu.VMEM((tm, D_MODEL), F32)],
        compiler_params=_cparams(("parallel", "arbitrary")),
        name="ffn_ln",
    )(hb, h, w1, w3, w2, g, b)


ATT_SUB = 128
DSA_QB = 256


def _dsa_kernel(qa_ref, qi_ref, wi_ref, k_ref, vt_ref, ki_ref, o_ref,
                kcat_ref, keys_ref, hi_ref, lo_ref, wq_ref, qt_ref, m_ref, l_ref, acc_ref,
                lg_ref, pt_ref, dm_ref, al_ref, *, past, l_real, kblk, topk):
    qb = qa_ref.shape[0]
    l_pad = ki_ref.shape[0]
    i = pl.program_id(1)
    kf = float(topk)

    @pl.when(i == 0)
    def _():
        def prep(c, carry):
            r0 = pl.multiple_of(c * kblk, kblk)
            x = ki_ref[pl.ds(r0, kblk), :]
            hi = x.astype(BF16)
            lo = x - hi.astype(F32)
            a = (hi.astype(F32) + pltpu.roll(lo, IDX_DIM, 1)).astype(BF16)
            kcat_ref[pl.ds(r0, kblk), 0:LANES] = a
            kcat_ref[pl.ds(r0, kblk), LANES:2 * LANES] = hi
            return carry
        lax.fori_loop(0, l_pad // kblk, prep, 0)

    qit = qi_ref[...].T
    qhi = qit.astype(BF16)
    qlo = (qit - qhi.astype(F32)).astype(BF16)
    zero = jnp.zeros((IDX_DIM, qb), BF16)
    for h in range(IDX_HEADS):
        hs = slice(h * IDX_DIM, (h + 1) * IDX_DIM)
        cs = slice(h * qb, (h + 1) * qb)
        wq_ref[0:64, cs] = qhi[hs]
        wq_ref[64:128, cs] = qhi[hs]
        wq_ref[128:192, cs] = qlo[hs]
        wq_ref[192:256, cs] = zero
    wit = wi_ref[...].T
    qat = (qa_ref[...] * (ATT_HD ** -0.5 * LOG2E)).T.astype(BF16)
    for h in range(ATT_HEADS):
        g, r = divmod(h, ATT_REP)
        qt_ref[g, :, r * qb:(r + 1) * qb] = qat[h * ATT_HD:(h + 1) * ATT_HD]

    q0 = past + i * qb
    pos = q0 + lax.broadcasted_iota(I32, (1, qb), 1)
    pos_chunk = lax.shift_right_logical(pos, 6)
    adm_keys = jnp.minimum(l_real, (lax.shift_right_logical(q0 + qb - 1, 6) + 1) * CHUNK)
    nkc = lax.div(adm_keys + (kblk - 1), kblk)

    def rows_of(c):
        return pl.ds(pl.multiple_of(c * kblk, kblk), kblk)

    def key_index(c):
        return c * kblk + lax.broadcasted_iota(I32, (kblk, qb), 0)

    def score_body(c, carry):
        kc = kcat_ref[rows_of(c), :]
        acc = jnp.zeros((kblk, qb), F32)
        for hg in range(IDX_HEADS // 4):
            rel = jnp.dot(kc, wq_ref[:, hg * 4 * qb:(hg + 1) * 4 * qb],
                          preferred_element_type=F32)
            for hh in range(4):
                h = hg * 4 + hh
                acc = acc + jnp.maximum(rel[:, hh * qb:(hh + 1) * qb], 0.0) * wit[h:h + 1, :]
        score = acc * IDX_SCALE
        s_idx = key_index(c)
        adm = (lax.shift_right_logical(s_idx, 6) <= pos_chunk) & (s_idx < l_real)
        score = jnp.where(adm, score, -jnp.inf)
        bits = pltpu.bitcast(score, I32)
        key = jnp.where(bits < 0, -(bits & 0x7FFFFFFF), bits)
        keys_ref[rows_of(c), :] = key
        hi_ref[rows_of(c), :] = lax.shift_right_arithmetic(key, 16).astype(I16)
        return carry
    lax.fori_loop(0, nkc, score_body, 0)

    n_acc = 4
    one16 = jnp.ones((), I16)
    zero16 = jnp.zeros((), I16)

    def count16(ref, cand):
        c16 = cand.astype(I16)

        def body(c, accs):
            ind = jnp.where(ref[rows_of(c), :] >= c16, one16, zero16)
            accs = list(accs)
            for s in range(kblk // PACKED_ROWS):
                accs[s % n_acc] = accs[s % n_acc] + ind[s * PACKED_ROWS:(s + 1) * PACKED_ROWS]
            return tuple(accs)
        accs = lax.fori_loop(0, nkc, body, (jnp.zeros((PACKED_ROWS, qb), I16),) * n_acc)
        tot = (accs[0].astype(I32) + accs[1].astype(I32)) + (accs[2].astype(I32) + accs[3].astype(I32))
        return jnp.sum(tot, axis=0, keepdims=True).astype(F32)

    def search16(ref):
        ans = jnp.full((1, qb), -32768, I32)
        cnt = jnp.full((1, qb), 1e9, F32)
        for bit in range(15, -1, -1):
            cand = ans + (1 << bit)
            c = count16(ref, cand)
            ok = c >= kf
            ans = jnp.where(ok, cand, ans)
            cnt = jnp.where(ok, c, cnt)
        return ans, cnt

    thr_hi, _ = search16(hi_ref)

    def low_body(c, carry):
        key = keys_ref[rows_of(c), :]
        hi = lax.shift_right_arithmetic(key, 16)
        lo = (key & 0xFFFF) - 32768
        k2 = jnp.where(hi > thr_hi, 32767, jnp.where(hi < thr_hi, -32768, lo))
        lo_ref[rows_of(c), :] = k2.astype(I16)
        return carry
    lax.fori_loop(0, nkc, low_body, 0)
    thr_lo, n_ge = search16(lo_ref)
    thr = thr_hi * 65536 + (thr_lo + 32768)
    thr_eff = jnp.maximum(thr, KEY_NEG_INF + 1)

    tied = (n_ge > kf) & (thr > KEY_NEG_INF)

    @pl.when(jnp.max(jnp.where(tied, 1.0, 0.0)) > 0.0)
    def _():
        def count32(pred_fn):
            def body(c, acc):
                ind = jnp.where(pred_fn(keys_ref[rows_of(c), :], key_index(c)), 1.0, 0.0)
                return acc + jnp.sum(ind.reshape(kblk // SUBLANES, SUBLANES, qb), axis=0)
            acc = lax.fori_loop(0, nkc, body, jnp.zeros((SUBLANES, qb), F32))
            return jnp.sum(acc, axis=0, keepdims=True)
        need = kf - count32(lambda key, s: key > thr)
        nbits = int(l_pad - 1).bit_length()

        def idx_body(it, p):
            cand = p + lax.shift_left(jnp.int32(1), nbits - 1 - it)
            few = count32(lambda key, s: (key == thr) & (s < cand)) < need
            return jnp.where(few, cand, p)
        p_last = lax.fori_loop(0, nbits, idx_body, jnp.zeros((1, qb), I32))

        def demote(c, carry):
            key = keys_ref[rows_of(c), :]
            drop = tied & (key == thr) & (key_index(c) > p_last)
            keys_ref[rows_of(c), :] = jnp.where(drop, thr - 1, key)
            return carry
        lax.fori_loop(0, nkc, demote, 0)

    m_ref[...] = jnp.full(m_ref.shape, M_FLOOR, F32)
    l_ref[...] = jnp.zeros(l_ref.shape, F32)
    acc_ref[...] = jnp.zeros(acc_ref.shape, F32)
    slopes = [float(2.0 ** (-8.0 * (h + 1) / ATT_HEADS)) * LOG2E for h in range(ATT_HEADS)]
    nsub = kblk // ATT_SUB
    fold = lambda t, op: op(t.reshape(ATT_SUB // SUBLANES, SUBLANES, qb), axis=0)

    def logits(g, c):
        lg_ref[g] = jnp.dot(k_ref[rows_of(c), g * ATT_HD:(g + 1) * ATT_HD], qt_ref[g],
                            preferred_element_type=F32)

    def softmax_block(g):
        for r in range(ATT_REP):
            h = g * ATT_REP + r
            cs = slice(r * qb, (r + 1) * qb)
            mx = jnp.full((SUBLANES, qb), M_FLOOR, F32)
            for sb in range(nsub):
                rs = slice(sb * ATT_SUB, (sb + 1) * ATT_SUB)
                lgs = lg_ref[g, rs, cs] - slopes[h] * dm_ref[rs, :]
                lg_ref[g, rs, cs] = lgs
                mx = jnp.maximum(mx, fold(lgs, jnp.max))
            m_old = m_ref[g, :, cs]
            m_new = jnp.maximum(m_old, jnp.max(mx, axis=0, keepdims=True))
            sm = jnp.zeros((SUBLANES, qb), F32)
            for sb in range(nsub):
                rs = slice(sb * ATT_SUB, (sb + 1) * ATT_SUB)
                pr = jnp.exp2(lg_ref[g, rs, cs] - m_new)
                sm = sm + fold(pr, jnp.sum)
                pt_ref[g, rs, cs] = pr.astype(BF16)
            alpha = jnp.exp2(m_old - m_new)
            l_ref[g, :, cs] = alpha * l_ref[g, :, cs] + jnp.sum(sm, axis=0, keepdims=True)
            m_ref[g, :, cs] = m_new
            al_ref[g, :, cs] = alpha

    def weighted_values(g, c):
        pv = jnp.dot(vt_ref[g * ATT_HD:(g + 1) * ATT_HD, rows_of(c)], pt_ref[g],
                     preferred_element_type=F32)
        acc_ref[g] = acc_ref[g] * al_ref[g] + pv

    logits(0, 0)

    def att_body(c, carry):
        key = keys_ref[rows_of(c), :]
        dist = jnp.abs(pos - key_index(c)).astype(F32)
        dm_ref[...] = jnp.where(key >= thr_eff, dist, MASK_DIST)
        logits(1, c)
        softmax_block(0)
        weighted_values(0, c)
        logits(0, jnp.minimum(c + 1, nkc - 1))
        softmax_block(1)
        weighted_values(1, c)
        return carry
    lax.fori_loop(0, nkc, att_body, 0)

    for h in range(ATT_HEADS):
        g, r = divmod(h, ATT_REP)
        cs = slice(r * qb, (r + 1) * qb)
        ot = acc_ref[g, :, cs] / l_ref[g, :, cs]
        o_ref[:, h * ATT_HD:(h + 1) * ATT_HD] = ot.T.astype(BF16)


def _dsa(q_ops, k_all, vt_all, ki_all, *, nb, nqb, qb, past, l_real, kblk, out_rows=None):
    l_pad = k_all.shape[1]
    topk = min(IDX_TOPK, l_real // 4)
    assert kblk >= topk and kblk % ATT_SUB == 0 and l_pad % kblk == 0
    (qa, qa_map), (qi, qi_map), (wi, wi_map) = q_ops
    kern = functools.partial(_dsa_kernel, past=past, l_real=l_real, kblk=kblk, topk=topk)
    return pl.pallas_call(
        kern,
        grid=(nb, nqb),
        in_specs=[pl.BlockSpec((qb, ATT_W), qa_map), pl.BlockSpec((qb, IDX_HEADS * IDX_DIM), qi_map),
                  pl.BlockSpec((qb, LANES), wi_map),
                  pl.BlockSpec((None, l_pad, ATT_KV * ATT_HD), lambda b, i: (b, 0, 0)),
                  pl.BlockSpec((None, ATT_KV * ATT_HD, l_pad), lambda b, i: (b, 0, 0)),
                  pl.BlockSpec((None, l_pad, LANES), lambda b, i: (b, 0, 0))],
        out_specs=pl.BlockSpec((qb, ATT_W), lambda b, i: (b * nqb + i, 0)),
        out_shape=jax.ShapeDtypeStruct((out_rows or nb * nqb * qb, ATT_W), BF16),
        scratch_shapes=[pltpu.VMEM((l_pad, 2 * LANES), BF16),
                        pltpu.VMEM((l_pad, qb), I32),
                        pltpu.VMEM((l_pad, qb), I16),
                        pltpu.VMEM((l_pad, qb), I16),
                        pltpu.VMEM((2 * LANES, IDX_HEADS * qb), BF16),
                        pltpu.VMEM((ATT_KV, ATT_HD, ATT_REP * qb), BF16),
                        pltpu.VMEM((ATT_KV, 1, ATT_REP * qb), F32),
                        pltpu.VMEM((ATT_KV, 1, ATT_REP * qb), F32),
                        pltpu.VMEM((ATT_KV, ATT_HD, ATT_REP * qb), F32),
                        pltpu.VMEM((ATT_KV, kblk, ATT_REP * qb), F32),
                        pltpu.VMEM((ATT_KV, kblk, ATT_REP * qb), BF16),
                        pltpu.VMEM((kblk, qb), F32),
                        pltpu.VMEM((ATT_KV, 1, ATT_REP * qb), F32)],
        compiler_params=_cparams(("parallel", "arbitrary")),
        name="dsa",
    )(qa, qi, wi, k_all, vt_all, ki_all)


def _rw_prep_kernel(r_ref, k_ref, v_ref, lr_ref, hr_ref, hk_ref, hv_ref, hlr_ref,
                    sr_ref, sk_ref, sv_ref, slr_ref, mr_ref, mk_ref, mv_ref, mlr_ref,
                    w0_ref, a0_ref, w2_ref, a2_ref, g2_ref,
                    ro_ref, ko_ref, vo_ref, wo_ref, ao_ref, go_ref, *, tiles_per_seq, time_minor):
    first_tile = (pl.program_id(0) % tiles_per_seq) == 0
    put = (lambda t: t.T) if time_minor else (lambda t: t)

    def mix(x_ref, halo_ref, shift_ref, mu_ref):
        x = x_ref[...]
        first = jnp.where(first_tile, shift_ref[...], halo_ref[SUBLANES - 1:SUBLANES, :])
        rows = lax.broadcasted_iota(I32, x.shape, 0)
        prev = jnp.where(rows == 0, first, pltpu.roll(x, 1, 0))
        return x + (prev - x) * mu_ref[...]

    ro_ref[...] = put(mix(r_ref, hr_ref, sr_ref, mr_ref))
    ko_ref[...] = put(mix(k_ref, hk_ref, sk_ref, mk_ref))
    vo_ref[...] = put(mix(v_ref, hv_ref, sv_ref, mv_ref))
    lr = mix(lr_ref, hlr_ref, slr_ref, mlr_ref)
    xw = lr[:, 0:128]
    xa = lr[:, 128:256]
    xg = lr[:, 256:512]
    wpre = w0_ref[...] + _dot3(jnp.tanh(xw), w2_ref[...])
    wo_ref[...] = put(jnp.exp(-RW_DECAY_SCALE * _sigmoid(wpre)))
    ao_ref[...] = put(_sigmoid(a0_ref[...] + _dot3(xa, a2_ref[...])))
    go_ref[...] = _dot3(_sigmoid(xg), g2_ref[...])


def _rw_prep(proj, row0, nb, seq, shifts, mus, w0, a0, w2p, a2p, g2, tm, out_rows=None):
    m = nb * seq
    tps = seq // tm
    rb0 = row0 // tm
    hb = tm // SUBLANES
    cr, ck, cv = _OFF["r"] // RW_W, _OFF["k"] // RW_W, _OFF["v"] // RW_W
    clr = _OFF["xw"] // 512

    def main(c):
        return lambda i: (rb0 + i, c)

    def halo(c):
        return lambda i: (jnp.maximum((rb0 + i) * hb - 1, 0), c)

    bidx = lambda i: (i // tps, 0, 0)
    fix = lambda i: (0, 0)
    wide = [pl.BlockSpec((tm, RW_W), main(cr)), pl.BlockSpec((tm, RW_W), main(ck)),
            pl.BlockSpec((tm, RW_W), main(cv)), pl.BlockSpec((tm, 512), main(clr)),
            pl.BlockSpec((SUBLANES, RW_W), halo(cr)), pl.BlockSpec((SUBLANES, RW_W), halo(ck)),
            pl.BlockSpec((SUBLANES, RW_W), halo(cv)), pl.BlockSpec((SUBLANES, 512), halo(clr))]
    sh = [pl.BlockSpec((None, 1, RW_W), bidx)] * 3 + [pl.BlockSpec((None, 1, 512), bidx)]
    mu = [pl.BlockSpec((1, RW_W), fix)] * 3 + [pl.BlockSpec((1, 512), fix)]
    par = [pl.BlockSpec((1, RW_W), fix), pl.BlockSpec((1, RW_W), fix),
           pl.BlockSpec((128, RW_W), fix), pl.BlockSpec((128, RW_W), fix),
           pl.BlockSpec((RW_G_R, RW_W), fix)]
    token_major = pl.BlockSpec((tm, RW_W), lambda i: (i, 0))
    time_minor = _relayout_ok(nb, seq) and tm % LANES == 0
    if time_minor:
        out = jax.ShapeDtypeStruct((nb, RW_W, seq), F32)
        seq_spec = pl.BlockSpec((None, RW_W, tm), lambda i: (i // tps, 0, i % tps))
    else:
        out = jax.ShapeDtypeStruct((m, RW_W), F32)
        seq_spec = token_major
    return pl.pallas_call(
        functools.partial(_rw_prep_kernel, tiles_per_seq=tps, time_minor=time_minor),
        grid=(m // tm,),
        in_specs=wide + sh + mu + par,
        out_specs=[seq_spec] * 5 + [token_major],
        out_shape=[out] * 5 + [jax.ShapeDtypeStruct((out_rows or m, RW_W), F32)],
        compiler_params=_cparams(("parallel",)),
        name="rwkv_prep",
    )(proj, proj, proj, proj, proj, proj, proj, proj, *shifts, *mus, w0, a0, w2p, a2p, g2)


def _rw_scan_kernel(r_ref, k_ref, v_ref, w_ref, a_ref, kkp_ref, kap_ref, rkp_ref,
                    lng_ref, lnb_ref, s0_ref, o_ref, st_ref, s_scr, tmp_scr):
    tb = pl.program_id(1)
    nsteps = r_ref.shape[0]

    @pl.when(tb == 0)
    def _():
        s_scr[...] = s0_ref[...]

    def step(t, d_prev):
        k = k_ref[t]
        a = a_ref[t]
        r = r_ref[t]
        kk = k * kkp_ref[...]
        nrm = jnp.sqrt(jnp.sum(kk * kk, axis=0, keepdims=True))
        kk = kk / jnp.maximum(nrm, 1e-12)
        km = k * (1.0 + (a - 1.0) * kap_ref[...])
        d = d_prev * w_ref[t]
        inv_d = 1.0 / d
        tmp_scr[0] = kk * d_prev
        tmp_scr[1] = kk * a * inv_d
        tmp_scr[2] = km * inv_d
        tmp_scr[3] = r * d
        parts = [jnp.zeros((RW_HD, LANES), F32) for _ in range(2)]
        for kx in range(RW_HD):
            parts[kx % 2] = parts[kx % 2] + s_scr[kx] * tmp_scr[0, kx:kx + 1, :]
        sa = -(parts[0] + parts[1])
        v = v_ref[t]
        outs = [jnp.zeros((RW_HD, LANES), F32) for _ in range(2)]
        for kx in range(RW_HD):
            s_new = s_scr[kx] + (sa * tmp_scr[1, kx:kx + 1, :] + v * tmp_scr[2, kx:kx + 1, :])
            s_scr[kx] = s_new
            outs[kx % 2] = outs[kx % 2] + s_new * tmp_scr[3, kx:kx + 1, :]
        o = outs[0] + outs[1]
        mu = jnp.mean(o, axis=0, keepdims=True)
        oc = o - mu
        var = jnp.mean(oc * oc, axis=0, keepdims=True)
        on = oc * lax.rsqrt(var + RW_GN_EPS) * lng_ref[...] + lnb_ref[...]
        bonus = jnp.sum(r * km * rkp_ref[...], axis=0, keepdims=True) * v
        o_ref[t] = on + bonus
        return d
    d_end = lax.fori_loop(0, nsteps, step, jnp.ones((RW_HD, LANES), F32))
    tmp_scr[0] = d_end
    for kx in range(RW_HD):
        s_scr[kx] = s_scr[kx] * tmp_scr[0, kx:kx + 1, :]

    @pl.when(tb == pl.num_programs(1) - 1)
    def _():
        st_ref[...] = s_scr[...]


def _rw_scan(r, k, v, w, a, params, s0, tb):
    seq, _, pairs = r.shape
    assert pairs % LANES == 0, "batch * RW_HEADS must fill whole lane groups"
    seqb = pl.BlockSpec((tb, RW_HD, LANES), lambda g, t: (t, 0, g))
    par = pl.BlockSpec((RW_HD, LANES), lambda g, t: (0, 0))
    st = pl.BlockSpec((RW_HD, RW_HD, LANES), lambda g, t: (0, 0, g))
    return pl.pallas_call(
        _rw_scan_kernel,
        grid=(pairs // LANES, seq // tb),
        in_specs=[seqb] * 5 + [par] * 5 + [st],
        out_specs=[seqb, st],
        out_shape=[jax.ShapeDtypeStruct((seq, RW_HD, pairs), F32),
                   jax.ShapeDtypeStruct((RW_HD, RW_HD, pairs), F32)],
        scratch_shapes=[pltpu.VMEM((RW_HD, RW_HD, LANES), F32),
                        pltpu.VMEM((4, RW_HD, LANES), F32)],
        compiler_params=_cparams(("parallel", "arbitrary")),
        name="rwkv_scan",
    )(r, k, v, w, a, *params, s0)


GLA_SUB = 16


def _gla_kernel(q_ref, k_ref, v_ref, gl_ref, og_ref, w2_ref, gb_ref, ng_ref, s0_ref,
                o_ref, st_ref, s_scr, la_scr, att_scr):
    tblk = pl.program_id(1)
    tc = q_ref.shape[0]

    @pl.when(tblk == 0)
    def _():
        for h in range(GLA_HEADS):
            s_scr[h] = s0_ref[h].T

    la_scr[...] = _log_sigmoid(_dot3(gl_ref[...], w2_ref[...]) + gb_ref[...]) * (1.0 / GLA_GATE_NORM)
    rows = lax.broadcasted_iota(I32, (CHUNK, GLA_DK), 0)
    lane = lax.broadcasted_iota(I32, (GLA_SUB, CHUNK), 1)
    jrow = lax.broadcasted_iota(I32, (GLA_SUB, GLA_DK), 0)
    nsub = CHUNK // GLA_SUB

    def head_chunk(h, rs):
        ks = slice(h * GLA_DK, (h + 1) * GLA_DK)
        vs = slice(h * GLA_DV, (h + 1) * GLA_DV)
        b = la_scr[rs, ks]
        sh = 1
        while sh < CHUNK:
            b = b + jnp.where(rows >= sh, pltpu.roll(b, sh, 0), 0.0)
            sh *= 2
        q = q_ref[rs, ks] * (GLA_DK ** -0.5)
        k = k_ref[rs, ks]
        v = v_ref[rs, vs]
        s_old = s_scr[h]
        inter = _nt_dot(q * jnp.exp(b), s_old)
        b_last = b[CHUNK - 1:CHUNK, :]

        for jb in range(nsub):
            js = slice(jb * GLA_SUB, (jb + 1) * GLA_SUB)
            bj = b[js]
            kj = k[js]
            blk = jnp.zeros((GLA_SUB, CHUNK), F32)
            for ii in range(GLA_SUB):
                i_row = jb * GLA_SUB + ii
                e = jnp.exp(jnp.minimum(b[i_row:i_row + 1, :] - bj, 0.0))
                val = jnp.where(jrow <= ii, q[i_row:i_row + 1, :] * kj * e, 0.0)
                col = jnp.sum(val, axis=1, keepdims=True)
                blk = jnp.where(lane == i_row, col, blk)
            if jb + 1 < nsub:
                i0 = (jb + 1) * GLA_SUB
                be = b[i0 - 1:i0, :]
                kt = kj * jnp.exp(be - bj)
                later = rows >= i0
                qt = jnp.where(later, q * jnp.exp(jnp.where(later, b - be, 0.0)), 0.0)
                blk = blk + _nt_dot(kt, qt)
            att_scr[h, js, :] = blk
        o = inter + _tn_dot(att_scr[h], v)
        khat = k * jnp.exp(b_last - b)
        s_scr[h] = s_old * jnp.exp(b_last) + _tn_dot(v, khat)
        o = o * lax.rsqrt(jnp.mean(o * o, -1, keepdims=True) + GLA_NORM_EPS) * ng_ref[:, vs]
        og = og_ref[rs, vs]
        o_ref[rs, vs] = (o * (og * _sigmoid(og))).astype(BF16)

    def chunk_body(c, carry):
        rs = pl.ds(pl.multiple_of(c * CHUNK, CHUNK), CHUNK)
        for h in range(GLA_HEADS):
            head_chunk(h, rs)
        return carry
    lax.fori_loop(0, tc // CHUNK, chunk_body, 0)

    @pl.when(tblk == pl.num_programs(1) - 1)
    def _():
        for h in range(GLA_HEADS):
            st_ref[h] = s_scr[h].T


def _gla(proj, tail, row0, nb, seq, w2p, gb, ng, s0, tc, out_rows=None):
    ntb = seq // tc
    rb0 = row0 // tc
    cq, ck = _OFF["qc"] // GLA_KW, _OFF["kc"] // GLA_KW
    cv, cg = _OFF["vc"] // GLA_VW, _OFF["og"] // GLA_VW
    cgl = (_OFF["gl"] - _OFF["ka"]) // LANES

    def rows(c0):
        return lambda b, t: (rb0 + b * ntb + t, c0)

    fix = lambda b, t: (0, 0)
    st = pl.BlockSpec((None, GLA_HEADS, GLA_DK, GLA_DV), lambda b, t: (b, 0, 0, 0))
    return pl.pallas_call(
        _gla_kernel,
        grid=(nb, ntb),
        in_specs=[pl.BlockSpec((tc, GLA_KW), rows(cq)), pl.BlockSpec((tc, GLA_KW), rows(ck)),
                  pl.BlockSpec((tc, GLA_VW), rows(cv)), pl.BlockSpec((tc, LANES), rows(cgl)),
                  pl.BlockSpec((tc, GLA_VW), rows(cg)),
                  pl.BlockSpec((LANES, GLA_KW), fix), pl.BlockSpec((1, GLA_KW), fix),
                  pl.BlockSpec((1, GLA_VW), fix), st],
        out_specs=[pl.BlockSpec((tc, GLA_VW), lambda b, t: (b * ntb + t, 0)), st],
        out_shape=[jax.ShapeDtypeStruct((out_rows or nb * seq, GLA_VW), BF16),
                   jax.ShapeDtypeStruct((nb, GLA_HEADS, GLA_DK, GLA_DV), F32)],
        scratch_shapes=[pltpu.VMEM((GLA_HEADS, GLA_DV, GLA_DK), F32), pltpu.VMEM((tc, GLA_KW), F32),
                        pltpu.VMEM((GLA_HEADS, CHUNK, CHUNK), F32)],
        compiler_params=_cparams(("parallel", "arbitrary")),
        name="gla",
    )(proj, proj, proj, tail, proj, w2p, gb, ng, s0)


_MAIN_SEGS = tuple(sg for sg in _SEGS if _OFF[sg[0]] < _OFF["ka"])
_TAIL_SEGS = tuple(sg for sg in _SEGS if _OFF[sg[0]] >= _OFF["ka"])


def _pack_cols(w, segs=_SEGS, dtype=None):
    dtype = dtype or w.dtype
    parts = []
    for _, s, wd, p in segs:
        parts.append(w[..., s:s + wd].astype(dtype))
        if p > wd:
            parts.append(jnp.zeros(w.shape[:-1] + (p - wd,), dtype))
    return jnp.concatenate(parts, axis=-1)


def _pack_rw(vec):
    r, k, v = vec[..., 0:1024], vec[..., 1024:2048], vec[..., 2048:3072]
    z = jnp.zeros(vec.shape[:-1] + (32,), vec.dtype)
    lr = jnp.concatenate([vec[..., 3072:3168], z, vec[..., 3168:3264], z, vec[..., 3264:3520]], -1)
    return r, k, v, lr


def _pad_rows(w, n):
    return jnp.concatenate([w, jnp.zeros((n - w.shape[0],) + w.shape[1:], w.dtype)], 0)


RELAYOUT_T = 128


def _to_scan_kernel(y_ref, o_ref):
    nbat = y_ref.shape[0]
    for n in range(RW_HD):
        z = jnp.concatenate([y_ref[b, pl.ds(n, RW_HEADS, stride=RW_HD), :] for b in range(nbat)], axis=0)
        o_ref[:, n, :] = z.T


def _from_scan_kernel(o_ref, x_ref, y_scr):
    b_step = pl.program_id(1)

    @pl.when(b_step == 0)
    def _():
        for n in range(RW_HD):
            zt = o_ref[:, n, :].T
            for b in range(y_scr.shape[0]):
                y_scr[b, pl.ds(n, RW_HEADS, stride=RW_HD), :] = zt[b * RW_HEADS:(b + 1) * RW_HEADS]

    for c in range(RW_W // LANES):
        x_ref[:, c * LANES:(c + 1) * LANES] = y_scr[b_step, c * LANES:(c + 1) * LANES, :].T


def _relayout_ok(nb, seq):
    return nb * RW_HEADS == LANES and seq % RELAYOUT_T == 0


def _to_scan(x, nb, seq):
    if x.ndim == 2:
        return jnp.transpose(x.reshape(nb, seq, RW_HEADS, RW_HD), (1, 3, 0, 2)).reshape(seq, RW_HD, nb * RW_HEADS)
    return pl.pallas_call(
        _to_scan_kernel,
        grid=(seq // RELAYOUT_T,),
        in_specs=[pl.BlockSpec((nb, RW_W, RELAYOUT_T), lambda t: (0, 0, t))],
        out_specs=pl.BlockSpec((RELAYOUT_T, RW_HD, LANES), lambda t: (t, 0, 0)),
        out_shape=jax.ShapeDtypeStruct((seq, RW_HD, LANES), F32),
        compiler_params=_cparams(("parallel",)),
        name="to_scan_layout",
    )(x)


def _from_scan(o, nb, seq, out_rows=None):
    if not _relayout_ok(nb, seq):
        x = jnp.transpose(o.reshape(seq, RW_HD, nb, RW_HEADS), (2, 0, 3, 1)).reshape(nb * seq, RW_W)
        return x if out_rows is None else jnp.pad(x, ((0, out_rows - nb * seq), (0, 0)))
    ntb = seq // RELAYOUT_T
    return pl.pallas_call(
        _from_scan_kernel,
        grid=(ntb, nb),
        in_specs=[pl.BlockSpec((RELAYOUT_T, RW_HD, LANES), lambda t, b: (t, 0, 0))],
        out_specs=pl.BlockSpec((RELAYOUT_T, RW_W), lambda t, b: (b * ntb + t, 0)),
        out_shape=jax.ShapeDtypeStruct((out_rows or nb * seq, RW_W), F32),
        scratch_shapes=[pltpu.VMEM((nb, RW_W, RELAYOUT_T), F32)],
        compiler_params=_cparams(("parallel", "arbitrary")),
        name="from_scan_layout",
    )(o)


def _group_mixers(proj, tail, row0, nb, seq, caches, lw, out_rows=None):
    m = nb * seq
    cache_k, cache_v, cache_ki, s_rw, s_shift, s_gla = caches
    past = 0 if cache_k is None else cache_k.shape[1]
    l_real = past + seq
    t0 = _OFF["ka"]
    grp = tail[row0:row0 + m].reshape(nb, seq, -1)

    ka = grp[..., _OFF["ka"] - t0:_OFF["ka"] - t0 + 256]
    va = grp[..., _OFF["va"] - t0:_OFF["va"] - t0 + 256]
    ki = grp[..., _OFF["ki"] - t0:_OFF["ki"] - t0 + LANES]
    if past:
        k_all = jnp.concatenate([cache_k.reshape(nb, past, 256), ka], 1)
        v_all = jnp.concatenate([cache_v.reshape(nb, past, 256), va], 1)
        cki = jnp.concatenate([cache_ki, jnp.zeros((nb, past, LANES - IDX_DIM), F32)], -1)
        ki_all = jnp.concatenate([cki, ki], 1)
    else:
        k_all, v_all, ki_all = ka, va, ki
    kblk = 512 if l_real % 512 == 0 else 384
    l_pad = -(-l_real // kblk) * kblk
    if l_pad > l_real:
        padk = lambda t: jnp.concatenate([t, jnp.zeros((nb, l_pad - l_real, t.shape[-1]), t.dtype)], 1)
        k_all, v_all, ki_all = padk(k_all), padk(v_all), padk(ki_all)
    k_bf = k_all.astype(BF16)
    vt_bf = jnp.swapaxes(v_all, 1, 2).astype(BF16)
    qb = DSA_QB if seq % DSA_QB == 0 else LANES
    if seq % qb == 0:
        nqb = seq // qb
        rb0 = row0 // qb
        mk = lambda c: (lambda b, i: (rb0 + b * nqb + i, c))
        q_ops = ((proj, mk(_OFF["qa"] // ATT_W)), (proj, mk(_OFF["qi"] // 1024)),
                 (tail, mk((_OFF["wi"] - t0) // LANES)))
        o_a = _dsa(q_ops, k_bf, vt_bf, ki_all, nb=nb, nqb=nqb, qb=qb, past=past, l_real=l_real, kblk=kblk,
                   out_rows=out_rows)
    else:
        def padq(t):
            return jnp.concatenate([t, jnp.zeros((nb, qb - seq, t.shape[-1]), F32)], 1).reshape(nb * qb, -1)
        qrows = proj[row0:row0 + m, 0:_OFF["vc"]].reshape(nb, seq, -1)
        mk = lambda b, i: (b, 0)
        q_ops = ((padq(qrows[..., 0:ATT_W]), mk), (padq(qrows[..., ATT_W:]), mk),
                 (padq(grp[..., _OFF["wi"] - t0:_OFF["wi"] - t0 + LANES]), mk))
        o_a = _dsa(q_ops, k_bf, vt_bf, ki_all, nb=nb, nqb=1, qb=qb, past=past, l_real=l_real, kblk=kblk)
        o_a = o_a.reshape(nb, qb, ATT_W)[:, :seq].reshape(m, ATT_W)

    if s_shift is None:
        s_shift = jnp.zeros((nb, RW_PROJ), F32)
    shifts = [t.reshape(nb, 1, -1) for t in _pack_rw(s_shift)]
    tm = 256 if seq % 256 == 0 else seq
    r, k, v, w, a, g = _rw_prep(proj, row0, nb, seq, shifts, lw["rw_mus"], lw["rw_w0"], lw["rw_a0"],
                                lw["rw_w2p"], lw["rw_a2p"], lw["rw_g2"], tm, out_rows)
    if s_rw is None:
        s0 = jnp.zeros((RW_HD, RW_HD, nb * RW_HEADS), F32)
    else:
        s0 = jnp.transpose(s_rw, (3, 2, 0, 1)).reshape(RW_HD, RW_HD, nb * RW_HEADS)
    tile = lambda p: jnp.tile(p.reshape(RW_HEADS, RW_HD).T, (1, LANES // RW_HEADS))
    params = [tile(lw[n]) for n in ("rw_kk", "rw_ka", "rw_rk", "rw_ln_g", "rw_ln_b")]
    tb = 32 if seq % 32 == 0 else seq
    o_b, s_t = _rw_scan(*[_to_scan(t, nb, seq) for t in (r, k, v, w, a)], params, s0, tb)
    o_b = _from_scan(o_b, nb, seq, out_rows)
    s_rw_new = jnp.transpose(s_t.reshape(RW_HD, RW_HD, nb, RW_HEADS), (2, 3, 1, 0))
    last = jnp.take(proj, row0 + seq - 1 + seq * jnp.arange(nb), axis=0)[:, _OFF["r"]:_OFF["qc"]]
    base = _OFF["r"]
    shift_new = jnp.concatenate(
        [last[:, 0:3072], last[:, _OFF["xw"] - base:_OFF["xw"] - base + 96],
         last[:, _OFF["xa"] - base:_OFF["xa"] - base + 96], last[:, _OFF["xg"] - base:]], -1)

    if s_gla is None:
        s_gla = jnp.zeros((nb, GLA_HEADS, GLA_DK, GLA_DV), F32)
    tc = 512 if seq % 512 == 0 else seq
    o_c, s_gla_new = _gla(proj, tail, row0, nb, seq, lw["gla_w2p"], lw["gla_b"], lw["gla_ng"], s_gla, tc,
                          out_rows)

    new_state = (ka.reshape(nb, seq, ATT_KV, ATT_HD), va.reshape(nb, seq, ATT_KV, ATT_HD),
                 ki[..., :IDX_DIM], s_rw_new, shift_new, s_gla_new)
    return o_a, o_b, g, o_c, new_state


def kernel(x_prompt, x_sample, cache_k, cache_v, cache_kidx, state_rwkv, state_rwkv_shift, state_gla,
           w_in, rw_mu, rw_w0, rw_w2, rw_a0, rw_a2, rw_g2, rw_kk, rw_ka, rw_rk, rw_ln_g, rw_ln_b,
           gla_w2, gla_b, gla_norm_g, w_branch_a, w_branch_b, w_branch_c, w_out,
           ln1_g, ln1_b, ln2_g, ln2_b, ffn_w1, ffn_w3, ffn_w2):
    bp, tp, _ = x_prompt.shape
    bs, ts, _ = x_sample.shape
    mp, ms = bp * tp, bs * ts
    x = jnp.concatenate([x_prompt.reshape(mp, D_MODEL), x_sample.reshape(ms, D_MODEL)], 0)
    xb = x.astype(BF16)
    m = mp + ms
    tm_big = 1024 if m % 1024 == 0 else 256
    tm_mid = 512 if m % 512 == 0 else 256

    new_p = [[] for _ in range(6)]
    new_s = [[] for _ in range(6)]
    for l in range(DEPTH):
        lw = {
            "rw_mus": [t.reshape(1, -1) for t in _pack_rw(rw_mu[l])],
            "rw_w0": rw_w0[l].reshape(1, RW_W), "rw_a0": rw_a0[l].reshape(1, RW_W),
            "rw_w2p": _pad_rows(rw_w2[l], 128), "rw_a2p": _pad_rows(rw_a2[l], 128), "rw_g2": rw_g2[l],
            "rw_kk": rw_kk[l], "rw_ka": rw_ka[l], "rw_rk": rw_rk[l],
            "rw_ln_g": rw_ln_g[l], "rw_ln_b": rw_ln_b[l],
            "gla_w2p": _pad_rows(gla_w2[l], LANES), "gla_b": gla_b[l].reshape(1, GLA_KW),
            "gla_ng": gla_norm_g[l].reshape(1, GLA_VW),
        }
        proj = _matmul(xb, _pack_cols(w_in[l], _MAIN_SEGS, BF16), tm_big, 1024)
        tail = _matmul(xb, _pack_cols(w_in[l], _TAIL_SEGS, BF16), tm_big, PROJ_W - _OFF["ka"])
        oa_p, ob_p, g_p, oc_p, st_p = _group_mixers(proj, tail, 0, bp, tp, (None,) * 6, lw, out_rows=m)
        oa_s, ob_s, g_s, oc_s, st_s = _group_mixers(
            proj, tail, mp, bs, ts,
            (cache_k[l], cache_v[l], cache_kidx[l], state_rwkv[l], state_rwkv_shift[l], state_gla[l]), lw)
        put = lambda full, part: lax.dynamic_update_slice(full, part, (mp, 0))
        merged = _merge(put(oa_p, oa_s), put(ob_p, ob_s), put(g_p, g_s), put(oc_p, oc_s), proj,
                        w_branch_a[l].astype(BF16), w_branch_b[l].astype(BF16),
                        w_branch_c[l].astype(BF16), 256, D_MODEL)
        h, hb = _hproj(merged, w_out[l].astype(BF16), x, ln1_g[l].reshape(1, -1),
                       ln1_b[l].reshape(1, -1), tm_mid)
        x, xb = _ffn(hb, h, ffn_w1[l].astype(BF16), ffn_w3[l].astype(BF16), ffn_w2[l].astype(BF16),
                     ln2_g[l].reshape(1, -1), ln2_b[l].reshape(1, -1), tm_mid, 512)
        for i in range(6):
            new_p[i].append(st_p[i])
            new_s[i].append(st_s[i])
    sp = [jnp.stack(a) for a in new_p]
    ss = [jnp.stack(a) for a in new_s]
    yp = x[:mp].reshape(bp, tp, D_MODEL)
    ys = x[mp:].reshape(bs, ts, D_MODEL)
    return (yp, ys, sp[0], sp[1], sp[2], sp[3], sp[4], sp[5], ss[0], ss[1], ss[2], ss[3], ss[4], ss[5])
```

```python
import functools

import jax
import jax.numpy as jnp
from jax import lax
from jax.experimental import pallas as pl
from jax.experimental.pallas import tpu as pltpu

F32 = jnp.float32
BF16 = jnp.bfloat16
I32 = jnp.int32
I16 = jnp.int16

D_MODEL = 2048
DEPTH = 4
CHUNK = 64
ATT_HEADS = 8
ATT_HD = 128
ATT_KV = 2
ATT_REP = ATT_HEADS // ATT_KV
ATT_W = ATT_HEADS * ATT_HD
IDX_HEADS = 16
IDX_DIM = 64
IDX_TOPK = 256
IDX_SCALE = (IDX_DIM * IDX_HEADS) ** -0.5
RW_HEADS = 16
RW_HD = 64
RW_W = RW_HEADS * RW_HD
RW_DECAY_R = 96
RW_A_R = 96
RW_G_R = 256
RW_PROJ = 3 * RW_W + RW_DECAY_R + RW_A_R + RW_G_R
RW_DECAY_SCALE = 0.606531
RW_GN_EPS = 64e-5
GLA_HEADS = 4
GLA_DK = 128
GLA_DV = 256
GLA_KW = GLA_HEADS * GLA_DK
GLA_VW = GLA_HEADS * GLA_DV
GLA_GATE_R = 16
GLA_GATE_NORM = 16.0
GLA_NORM_EPS = 1e-5
FFN_HIDDEN = -(-8 * D_MODEL // (3 * 256)) * 256
DN_ALPHA = (2 * DEPTH) ** 0.25
LN_EPS = 1e-5
LOG2E = 1.4426950408889634

LANES = 128
SUBLANES = 8
PACKED_ROWS = 16
VMEM_LIMIT = 56 * 1024 * 1024

_SEGS = (
    ("qa", 0, 1024, 1024), ("qi", 1536, 1024, 1024), ("vc", 7184, 1024, 1024),
    ("og", 8224, 1024, 1024), ("gates", 9248, 6144, 6144),
    ("r", 2640, 1024, 1024), ("k", 3664, 1024, 1024), ("v", 4688, 1024, 1024),
    ("xw", 5712, 96, 128), ("xa", 5808, 96, 128), ("xg", 5904, 256, 256),
    ("qc", 6160, 512, 512), ("kc", 6672, 512, 512),
    ("ka", 1024, 256, 256), ("va", 1280, 256, 256),
    ("ki", 2560, 64, 128), ("wi", 2624, 16, 128), ("gl", 8208, 16, 128),
)
_OFF = {}
_o = 0
for _n, _s, _w, _p in _SEGS:
    _OFF[_n] = _o
    _o += _p
PROJ_W = _o
KEY_NEG_INF = -0x7F800000
MASK_DIST = 1e32
M_FLOOR = -1e29


def _cparams(sem):
    return pltpu.CompilerParams(dimension_semantics=sem, vmem_limit_bytes=VMEM_LIMIT)


def _split2(x):
    hi = x.astype(BF16)
    lo = (x - hi.astype(F32)).astype(BF16)
    return hi, lo


def _dot3(a, b):
    ah, al = _split2(a)
    bh, bl = _split2(b)
    d = functools.partial(jnp.dot, preferred_element_type=F32)
    return d(ah, bh) + (d(al, bh) + d(ah, bl))


def _nt_dot(a, b):
    return lax.dot_general(a.astype(BF16), b.astype(BF16), (((1,), (1,)), ((), ())),
                           preferred_element_type=F32)


def _tn_dot(a, b):
    return lax.dot_general(a.astype(BF16), b.astype(BF16), (((0,), (0,)), ((), ())),
                           preferred_element_type=F32)


def _sigmoid(x):
    return 1.0 / (1.0 + jnp.exp(-x))


def _log_sigmoid(x):
    return jnp.minimum(x, 0.0) - jnp.log(1.0 + jnp.exp(-jnp.abs(x)))


def _layer_norm(x, g, b):
    mu = jnp.mean(x, -1, keepdims=True)
    xc = x - mu
    var = jnp.mean(xc * xc, -1, keepdims=True)
    return xc * lax.rsqrt(var + LN_EPS) * g + b


def _mm_kernel(a_ref, b_ref, o_ref):
    o_ref[...] = jnp.dot(a_ref[...], b_ref[...], preferred_element_type=F32)


def _matmul(a, b, tm, tn):
    m, k = a.shape
    n = b.shape[1]
    return pl.pallas_call(
        _mm_kernel,
        grid=(m // tm, pl.cdiv(n, tn)),
        in_specs=[pl.BlockSpec((tm, k), lambda i, j: (i, 0)),
                  pl.BlockSpec((k, tn), lambda i, j: (0, j))],
        out_specs=pl.BlockSpec((tm, tn), lambda i, j: (i, j)),
        out_shape=jax.ShapeDtypeStruct((m, n), F32),
        compiler_params=_cparams(("parallel", "arbitrary")),
        name="in_proj",
    )(a, b)


def _merge_kernel(oa_ref, ob_ref, g_ref, oc_ref, ga_ref, gb_ref, gc_ref,
                  wa_ref, wb_ref, wc_ref, o_ref):
    ob = (ob_ref[...] * g_ref[...]).astype(BF16)
    d = functools.partial(jnp.dot, preferred_element_type=F32)
    acc = _sigmoid(ga_ref[...]) * d(oa_ref[...], wa_ref[...])
    acc += _sigmoid(gb_ref[...]) * d(ob, wb_ref[...])
    acc += _sigmoid(gc_ref[...]) * d(oc_ref[...], wc_ref[...])
    o_ref[...] = acc.astype(BF16)


def _merge(oa, ob, g, oc, proj, wa, wb, wc, tm, tn):
    m = oa.shape[0]
    gblk = _OFF["gates"] // tn
    nj = D_MODEL // tn
    row = lambda i, j: (i, 0)
    wspec = pl.BlockSpec((ATT_W, tn), lambda i, j: (0, j))
    return pl.pallas_call(
        _merge_kernel,
        grid=(m // tm, nj),
        in_specs=[pl.BlockSpec((tm, ATT_W), row), pl.BlockSpec((tm, RW_W), row),
                  pl.BlockSpec((tm, RW_W), row), pl.BlockSpec((tm, GLA_VW), row),
                  pl.BlockSpec((tm, tn), lambda i, j: (i, gblk + j)),
                  pl.BlockSpec((tm, tn), lambda i, j: (i, gblk + nj + j)),
                  pl.BlockSpec((tm, tn), lambda i, j: (i, gblk + 2 * nj + j)),
                  wspec, wspec, wspec],
        out_specs=pl.BlockSpec((tm, tn), lambda i, j: (i, j)),
        out_shape=jax.ShapeDtypeStruct((m, D_MODEL), BF16),
        compiler_params=_cparams(("parallel", "arbitrary")),
        name="merge",
    )(oa, ob, g, oc, proj, proj, proj, wa, wb, wc)


def _hproj_kernel(m_ref, w_ref, x_ref, g_ref, b_ref, h_ref, hb_ref):
    z = DN_ALPHA * x_ref[...] + jnp.dot(m_ref[...], w_ref[...], preferred_element_type=F32)
    h = _layer_norm(z, g_ref[...], b_ref[...])
    h_ref[...] = h
    hb_ref[...] = h.astype(BF16)


def _hproj(merged, w_out, x, g, b, tm):
    m = x.shape[0]
    row = lambda i: (i, 0)
    fix = lambda i: (0, 0)
    return pl.pallas_call(
        _hproj_kernel,
        grid=(m // tm,),
        in_specs=[pl.BlockSpec((tm, D_MODEL), row), pl.BlockSpec((D_MODEL, D_MODEL), fix),
                  pl.BlockSpec((tm, D_MODEL), row), pl.BlockSpec((1, D_MODEL), fix),
                  pl.BlockSpec((1, D_MODEL), fix)],
        out_specs=[pl.BlockSpec((tm, D_MODEL), row), pl.BlockSpec((tm, D_MODEL), row)],
        out_shape=[jax.ShapeDtypeStruct((m, D_MODEL), F32),
                   jax.ShapeDtypeStruct((m, D_MODEL), BF16)],
        compiler_params=_cparams(("parallel",)),
        name="out_proj_ln",
    )(merged, w_out, x, g, b)


def _ffn_kernel(hb_ref, h_ref, w1_ref, w3_ref, w2_ref, g_ref, b_ref, y_ref, yb_ref, acc_ref):
    j = pl.program_id(1)

    @pl.when(j == 0)
    def _():
        acc_ref[...] = jnp.zeros_like(acc_ref)

    hb = hb_ref[...]
    u1 = jnp.dot(hb, w1_ref[...], preferred_element_type=F32)
    u3 = jnp.dot(hb, w3_ref[...], preferred_element_type=F32)
    u = (u1 * _sigmoid(u1) * u3).astype(BF16)
    acc_ref[...] += jnp.dot(u, w2_ref[...], preferred_element_type=F32)

    @pl.when(j == pl.num_programs(1) - 1)
    def _():
        y = _layer_norm(DN_ALPHA * h_ref[...] + acc_ref[...], g_ref[...], b_ref[...])
        y_ref[...] = y
        yb_ref[...] = y.astype(BF16)


def _ffn(hb, h, w1, w3, w2, g, b, tm, th):
    m = h.shape[0]
    row = lambda i, j: (i, 0)
    fix = lambda i, j: (0, 0)
    return pl.pallas_call(
        _ffn_kernel,
        grid=(m // tm, FFN_HIDDEN // th),
        in_specs=[pl.BlockSpec((tm, D_MODEL), row), pl.BlockSpec((tm, D_MODEL), row),
                  pl.BlockSpec((D_MODEL, th), lambda i, j: (0, j)),
                  pl.BlockSpec((D_MODEL, th), lambda i, j: (0, j)),
                  pl.BlockSpec((th, D_MODEL), lambda i, j: (j, 0)),
                  pl.BlockSpec((1, D_MODEL), fix), pl.BlockSpec((1, D_MODEL), fix)],
        out_specs=[pl.BlockSpec((tm, D_MODEL), row), pl.BlockSpec((tm, D_MODEL), row)],
        out_shape=[jax.ShapeDtypeStruct((m, D_MODEL), F32),
                   jax.ShapeDtypeStruct((m, D_MODEL), BF16)],
        scratch_shapes=[pltpu.VMEM((tm, D_MODEL), F32)],
        compiler_params=_cparams(("parallel", "arbitrary")),
        name="ffn_ln",
    )(hb, h, w1, w3, w2, g, b)


ATT_SUB = 128
DSA_QB = 256


def _dsa_kernel(qa_ref, qi_ref, wi_ref, k_ref, vt_ref, ki_ref, o_ref,
                kcat_ref, keys_ref, hi_ref, lo_ref, wq_ref, qt_ref, m_ref, l_ref, acc_ref,
                lg_ref, pt_ref, dm_ref, al_ref, *, past, l_real, kblk, topk):
    qb = qa_ref.shape[0]
    l_pad = ki_ref.shape[0]
    i = pl.program_id(1)
    kf = float(topk)

    @pl.when(i == 0)
    def _():
        def prep(c, carry):
            r0 = pl.multiple_of(c * kblk, kblk)
            x = ki_ref[pl.ds(r0, kblk), :]
            hi = x.astype(BF16)
            lo = x - hi.astype(F32)
            a = (hi.astype(F32) + pltpu.roll(lo, IDX_DIM, 1)).astype(BF16)
            kcat_ref[pl.ds(r0, kblk), 0:LANES] = a
            kcat_ref[pl.ds(r0, kblk), LANES:2 * LANES] = hi
            return carry
        lax.fori_loop(0, l_pad // kblk, prep, 0)

    qit = qi_ref[...].T
    qhi = qit.astype(BF16)
    qlo = (qit - qhi.astype(F32)).astype(BF16)
    zero = jnp.zeros((IDX_DIM, qb), BF16)
    for h in range(IDX_HEADS):
        hs = slice(h * IDX_DIM, (h + 1) * IDX_DIM)
        cs = slice(h * qb, (h + 1) * qb)
        wq_ref[0:64, cs] = qhi[hs]
        wq_ref[64:128, cs] = qhi[hs]
        wq_ref[128:192, cs] = qlo[hs]
        wq_ref[192:256, cs] = zero
    wit = wi_ref[...].T
    qat = (qa_ref[...] * (ATT_HD ** -0.5 * LOG2E)).T.astype(BF16)
    for h in range(ATT_HEADS):
        g, r = divmod(h, ATT_REP)
        qt_ref[g, :, r * qb:(r + 1) * qb] = qat[h * ATT_HD:(h + 1) * ATT_HD]

    q0 = past + i * qb
    pos = q0 + lax.broadcasted_iota(I32, (1, qb), 1)
    pos_chunk = lax.shift_right_logical(pos, 6)
    adm_keys = jnp.minimum(l_real, (lax.shift_right_logical(q0 + qb - 1, 6) + 1) * CHUNK)
    nkc = lax.div(adm_keys + (kblk - 1), kblk)

    def rows_of(c):
        return pl.ds(pl.multiple_of(c * kblk, kblk), kblk)

    def key_index(c):
        return c * kblk + lax.broadcasted_iota(I32, (kblk, qb), 0)

    def score_body(c, carry):
        kc = kcat_ref[rows_of(c), :]
        acc = jnp.zeros((kblk, qb), F32)
        for hg in range(IDX_HEADS // 4):
            rel = jnp.dot(kc, wq_ref[:, hg * 4 * qb:(hg + 1) * 4 * qb],
                          preferred_element_type=F32)
            for hh in range(4):
                h = hg * 4 + hh
                acc = acc + jnp.maximum(rel[:, hh * qb:(hh + 1) * qb], 0.0) * wit[h:h + 1, :]
        score = acc * IDX_SCALE
        s_idx = key_index(c)
        adm = (lax.shift_right_logical(s_idx, 6) <= pos_chunk) & (s_idx < l_real)
        score = jnp.where(adm, score, -jnp.inf)
        bits = pltpu.bitcast(score, I32)
        key = jnp.where(bits < 0, -(bits & 0x7FFFFFFF), bits)
        keys_ref[rows_of(c), :] = key
        hi_ref[rows_of(c), :] = lax.shift_right_arithmetic(key, 16).astype(I16)
        return carry
    lax.fori_loop(0, nkc, score_body, 0)

    n_acc = 4
    one16 = jnp.ones((), I16)
    zero16 = jnp.zeros((), I16)

    def count16(ref, cand):
        c16 = cand.astype(I16)

        def body(c, accs):
            ind = jnp.where(ref[rows_of(c), :] >= c16, one16, zero16)
            accs = list(accs)
            for s in range(kblk // PACKED_ROWS):
                accs[s % n_acc] = accs[s % n_acc] + ind[s * PACKED_ROWS:(s + 1) * PACKED_ROWS]
            return tuple(accs)
        accs = lax.fori_loop(0, nkc, body, (jnp.zeros((PACKED_ROWS, qb), I16),) * n_acc)
        tot = (accs[0].astype(I32) + accs[1].astype(I32)) + (accs[2].astype(I32) + accs[3].astype(I32))
        return jnp.sum(tot, axis=0, keepdims=True).astype(F32)

    def search16(ref):
        ans = jnp.full((1, qb), -32768, I32)
        cnt = jnp.full((1, qb), 1e9, F32)
        for bit in range(15, -1, -1):
            cand = ans + (1 << bit)
            c = count16(ref, cand)
            ok = c >= kf
            ans = jnp.where(ok, cand, ans)
            cnt = jnp.where(ok, c, cnt)
        return ans, cnt

    thr_hi, _ = search16(hi_ref)

    def low_body(c, carry):
        key = keys_ref[rows_of(c), :]
        hi = lax.shift_right_arithmetic(key, 16)
        lo = (key & 0xFFFF) - 32768
        k2 = jnp.where(hi > thr_hi, 32767, jnp.where(hi < thr_hi, -32768, lo))
        lo_ref[rows_of(c), :] = k2.astype(I16)
        return carry
    lax.fori_loop(0, nkc, low_body, 0)
    thr_lo, n_ge = search16(lo_ref)
    thr = thr_hi * 65536 + (thr_lo + 32768)
    thr_eff = jnp.maximum(thr, KEY_NEG_INF + 1)

    tied = (n_ge > kf) & (thr > KEY_NEG_INF)

    @pl.when(jnp.max(jnp.where(tied, 1.0, 0.0)) > 0.0)
    def _():
        def count32(pred_fn):
            def body(c, acc):
                ind = jnp.where(pred_fn(keys_ref[rows_of(c), :], key_index(c)), 1.0, 0.0)
                return acc + jnp.sum(ind.reshape(kblk // SUBLANES, SUBLANES, qb), axis=0)
            acc = lax.fori_loop(0, nkc, body, jnp.zeros((SUBLANES, qb), F32))
            return jnp.sum(acc, axis=0, keepdims=True)
        need = kf - count32(lambda key, s: key > thr)
        nbits = int(l_pad - 1).bit_length()

        def idx_body(it, p):
            cand = p + lax.shift_left(jnp.int32(1), nbits - 1 - it)
            few = count32(lambda key, s: (key == thr) & (s < cand)) < need
            return jnp.where(few, cand, p)
        p_last = lax.fori_loop(0, nbits, idx_body, jnp.zeros((1, qb), I32))

        def demote(c, carry):
            key = keys_ref[rows_of(c), :]
            drop = tied & (key == thr) & (key_index(c) > p_last)
            keys_ref[rows_of(c), :] = jnp.where(drop, thr - 1, key)
            return carry
        lax.fori_loop(0, nkc, demote, 0)

    m_ref[...] = jnp.full(m_ref.shape, M_FLOOR, F32)
    l_ref[...] = jnp.zeros(l_ref.shape, F32)
    acc_ref[...] = jnp.zeros(acc_ref.shape, F32)
    slopes = [float(2.0 ** (-8.0 * (h + 1) / ATT_HEADS)) * LOG2E for h in range(ATT_HEADS)]
    nsub = kblk // ATT_SUB
    fold = lambda t, op: op(t.reshape(ATT_SUB // SUBLANES, SUBLANES, qb), axis=0)

    def logits(g, c):
        lg_ref[g] = jnp.dot(k_ref[rows_of(c), g * ATT_HD:(g + 1) * ATT_HD], qt_ref[g],
                            preferred_element_type=F32)

    def softmax_block(g):
        for r in range(ATT_REP):
            h = g * ATT_REP + r
            cs = slice(r * qb, (r + 1) * qb)
            mx = jnp.full((SUBLANES, qb), M_FLOOR, F32)
            for sb in range(nsub):
                rs = slice(sb * ATT_SUB, (sb + 1) * ATT_SUB)
                lgs = lg_ref[g, rs, cs] - slopes[h] * dm_ref[rs, :]
                lg_ref[g, rs, cs] = lgs
                mx = jnp.maximum(mx, fold(lgs, jnp.max))
            m_old = m_ref[g, :, cs]
            m_new = jnp.maximum(m_old, jnp.max(mx, axis=0, keepdims=True))
            sm = jnp.zeros((SUBLANES, qb), F32)
            for sb in range(nsub):
                rs = slice(sb * ATT_SUB, (sb + 1) * ATT_SUB)
                pr = jnp.exp2(lg_ref[g, rs, cs] - m_new)
                sm = sm + fold(pr, jnp.sum)
                pt_ref[g, rs, cs] = pr.astype(BF16)
            alpha = jnp.exp2(m_old - m_new)
            l_ref[g, :, cs] = alpha * l_ref[g, :, cs] + jnp.sum(sm, axis=0, keepdims=True)
            m_ref[g, :, cs] = m_new
            al_ref[g, :, cs] = alpha

    def weighted_values(g, c):
        pv = jnp.dot(vt_ref[g * ATT_HD:(g + 1) * ATT_HD, rows_of(c)], pt_ref[g],
                     preferred_element_type=F32)
        acc_ref[g] = acc_ref[g] * al_ref[g] + pv

    logits(0, 0)

    def att_body(c, carry):
        key = keys_ref[rows_of(c), :]
        dist = jnp.abs(pos - key_index(c)).astype(F32)
        dm_ref[...] = jnp.where(key >= thr_eff, dist, MASK_DIST)
        logits(1, c)
        softmax_block(0)
        weighted_values(0, c)
        logits(0, jnp.minimum(c + 1, nkc - 1))
        softmax_block(1)
        weighted_values(1, c)
        return carry
    lax.fori_loop(0, nkc, att_body, 0)

    for h in range(ATT_HEADS):
        g, r = divmod(h, ATT_REP)
        cs = slice(r * qb, (r + 1) * qb)
        ot = acc_ref[g, :, cs] / l_ref[g, :, cs]
        o_ref[:, h * ATT_HD:(h + 1) * ATT_HD] = ot.T.astype(BF16)


def _dsa(q_ops, k_all, vt_all, ki_all, *, nb, nqb, qb, past, l_real, kblk, out_rows=None):
    l_pad = k_all.shape[1]
    topk = min(IDX_TOPK, l_real // 4)
    assert kblk >= topk and kblk % ATT_SUB == 0 and l_pad % kblk == 0
    (qa, qa_map), (qi, qi_map), (wi, wi_map) = q_ops
    kern = functools.partial(_dsa_kernel, past=past, l_real=l_real, kblk=kblk, topk=topk)
    return pl.pallas_call(
        kern,
        grid=(nb, nqb),
        in_specs=[pl.BlockSpec((qb, ATT_W), qa_map), pl.BlockSpec((qb, IDX_HEADS * IDX_DIM), qi_map),
                  pl.BlockSpec((qb, LANES), wi_map),
                  pl.BlockSpec((None, l_pad, ATT_KV * ATT_HD), lambda b, i: (b, 0, 0)),
                  pl.BlockSpec((None, ATT_KV * ATT_HD, l_pad), lambda b, i: (b, 0, 0)),
                  pl.BlockSpec((None, l_pad, LANES), lambda b, i: (b, 0, 0))],
        out_specs=pl.BlockSpec((qb, ATT_W), lambda b, i: (b * nqb + i, 0)),
        out_shape=jax.ShapeDtypeStruct((out_rows or nb * nqb * qb, ATT_W), BF16),
        scratch_shapes=[pltpu.VMEM((l_pad, 2 * LANES), BF16),
                        pltpu.VMEM((l_pad, qb), I32),
                        pltpu.VMEM((l_pad, qb), I16),
                        pltpu.VMEM((l_pad, qb), I16),
                        pltpu.VMEM((2 * LANES, IDX_HEADS * qb), BF16),
                        pltpu.VMEM((ATT_KV, ATT_HD, ATT_REP * qb), BF16),
                        pltpu.VMEM((ATT_KV, 1, ATT_REP * qb), F32),
                        pltpu.VMEM((ATT_KV, 1, ATT_REP * qb), F32),
                        pltpu.VMEM((ATT_KV, ATT_HD, ATT_REP * qb), F32),
                        pltpu.VMEM((ATT_KV, kblk, ATT_REP * qb), F32),
                        pltpu.VMEM((ATT_KV, kblk, ATT_REP * qb), BF16),
                        pltpu.VMEM((kblk, qb), F32),
                        pltpu.VMEM((ATT_KV, 1, ATT_REP * qb), F32)],
        compiler_params=_cparams(("parallel", "arbitrary")),
        name="dsa",
    )(qa, qi, wi, k_all, vt_all, ki_all)


def _rw_prep_kernel(r_ref, k_ref, v_ref, lr_ref, hr_ref, hk_ref, hv_ref, hlr_ref,
                    sr_ref, sk_ref, sv_ref, slr_ref, mr_ref, mk_ref, mv_ref, mlr_ref,
                    w0_ref, a0_ref, w2_ref, a2_ref, g2_ref,
                    ro_ref, ko_ref, vo_ref, wo_ref, ao_ref, go_ref, *, tiles_per_seq, time_minor):
    first_tile = (pl.program_id(0) % tiles_per_seq) == 0
    put = (lambda t: t.T) if time_minor else (lambda t: t)

    def mix(x_ref, halo_ref, shift_ref, mu_ref):
        x = x_ref[...]
        first = jnp.where(first_tile, shift_ref[...], halo_ref[SUBLANES - 1:SUBLANES, :])
        rows = lax.broadcasted_iota(I32, x.shape, 0)
        prev = jnp.where(rows == 0, first, pltpu.roll(x, 1, 0))
        return x + (prev - x) * mu_ref[...]

    ro_ref[...] = put(mix(r_ref, hr_ref, sr_ref, mr_ref))
    ko_ref[...] = put(mix(k_ref, hk_ref, sk_ref, mk_ref))
    vo_ref[...] = put(mix(v_ref, hv_ref, sv_ref, mv_ref))
    lr = mix(lr_ref, hlr_ref, slr_ref, mlr_ref)
    xw = lr[:, 0:128]
    xa = lr[:, 128:256]
    xg = lr[:, 256:512]
    wpre = w0_ref[...] + _dot3(jnp.tanh(xw), w2_ref[...])
    wo_ref[...] = put(jnp.exp(-RW_DECAY_SCALE * _sigmoid(wpre)))
    ao_ref[...] = put(_sigmoid(a0_ref[...] + _dot3(xa, a2_ref[...])))
    go_ref[...] = _dot3(_sigmoid(xg), g2_ref[...])


def _rw_prep(proj, row0, nb, seq, shifts, mus, w0, a0, w2p, a2p, g2, tm, out_rows=None):
    m = nb * seq
    tps = seq // tm
    rb0 = row0 // tm
    hb = tm // SUBLANES
    cr, ck, cv = _OFF["r"] // RW_W, _OFF["k"] // RW_W, _OFF["v"] // RW_W
    clr = _OFF["xw"] // 512

    def main(c):
        return lambda i: (rb0 + i, c)

    def halo(c):
        return lambda i: (jnp.maximum((rb0 + i) * hb - 1, 0), c)

    bidx = lambda i: (i // tps, 0, 0)
    fix = lambda i: (0, 0)
    wide = [pl.BlockSpec((tm, RW_W), main(cr)), pl.BlockSpec((tm, RW_W), main(ck)),
            pl.BlockSpec((tm, RW_W), main(cv)), pl.BlockSpec((tm, 512), main(clr)),
            pl.BlockSpec((SUBLANES, RW_W), halo(cr)), pl.BlockSpec((SUBLANES, RW_W), halo(ck)),
            pl.BlockSpec((SUBLANES, RW_W), halo(cv)), pl.BlockSpec((SUBLANES, 512), halo(clr))]
    sh = [pl.BlockSpec((None, 1, RW_W), bidx)] * 3 + [pl.BlockSpec((None, 1, 512), bidx)]
    mu = [pl.BlockSpec((1, RW_W), fix)] * 3 + [pl.BlockSpec((1, 512), fix)]
    par = [pl.BlockSpec((1, RW_W), fix), pl.BlockSpec((1, RW_W), fix),
           pl.BlockSpec((128, RW_W), fix), pl.BlockSpec((128, RW_W), fix),
           pl.BlockSpec((RW_G_R, RW_W), fix)]
    token_major = pl.BlockSpec((tm, RW_W), lambda i: (i, 0))
    time_minor = _relayout_ok(nb, seq) and tm % LANES == 0
    if time_minor:
        out = jax.ShapeDtypeStruct((nb, RW_W, seq), F32)
        seq_spec = pl.BlockSpec((None, RW_W, tm), lambda i: (i // tps, 0, i % tps))
    else:
        out = jax.ShapeDtypeStruct((m, RW_W), F32)
        seq_spec = token_major
    return pl.pallas_call(
        functools.partial(_rw_prep_kernel, tiles_per_seq=tps, time_minor=time_minor),
        grid=(m // tm,),
        in_specs=wide + sh + mu + par,
        out_specs=[seq_spec] * 5 + [token_major],
        out_shape=[out] * 5 + [jax.ShapeDtypeStruct((out_rows or m, RW_W), F32)],
        compiler_params=_cparams(("parallel",)),
        name="rwkv_prep",
    )(proj, proj, proj, proj, proj, proj, proj, proj, *shifts, *mus, w0, a0, w2p, a2p, g2)


def _rw_scan_kernel(r_ref, k_ref, v_ref, w_ref, a_ref, kkp_ref, kap_ref, rkp_ref,
                    lng_ref, lnb_ref, s0_ref, o_ref, st_ref, s_scr, tmp_scr):
    tb = pl.program_id(1)
    nsteps = r_ref.shape[0]

    @pl.when(tb == 0)
    def _():
        s_scr[...] = s0_ref[...]

    def step(t, d_prev):
        k = k_ref[t]
        a = a_ref[t]
        r = r_ref[t]
        kk = k * kkp_ref[...]
        nrm = jnp.sqrt(jnp.sum(kk * kk, axis=0, keepdims=True))
        kk = kk / jnp.maximum(nrm, 1e-12)
        km = k * (1.0 + (a - 1.0) * kap_ref[...])
        d = d_prev * w_ref[t]
        inv_d = 1.0 / d
        tmp_scr[0] = kk * d_prev
        tmp_scr[1] = kk * a * inv_d
        tmp_scr[2] = km * inv_d
        tmp_scr[3] = r * d
        parts = [jnp.zeros((RW_HD, LANES), F32) for _ in range(2)]
        for kx in range(RW_HD):
            parts[kx % 2] = parts[kx % 2] + s_scr[kx] * tmp_scr[0, kx:kx + 1, :]
        sa = -(parts[0] + parts[1])
        v = v_ref[t]
        outs = [jnp.zeros((RW_HD, LANES), F32) for _ in range(2)]
        for kx in range(RW_HD):
            s_new = s_scr[kx] + (sa * tmp_scr[1, kx:kx + 1, :] + v * tmp_scr[2, kx:kx + 1, :])
            s_scr[kx] = s_new
            outs[kx % 2] = outs[kx % 2] + s_new * tmp_scr[3, kx:kx + 1, :]
        o = outs[0] + outs[1]
        mu = jnp.mean(o, axis=0, keepdims=True)
        oc = o - mu
        var = jnp.mean(oc * oc, axis=0, keepdims=True)
        on = oc * lax.rsqrt(var + RW_GN_EPS) * lng_ref[...] + lnb_ref[...]
        bonus = jnp.sum(r * km * rkp_ref[...], axis=0, keepdims=True) * v
        o_ref[t] = on + bonus
        return d
    d_end = lax.fori_loop(0, nsteps, step, jnp.ones((RW_HD, LANES), F32), unroll=2)
    tmp_scr[0] = d_end
    for kx in range(RW_HD):
        s_scr[kx] = s_scr[kx] * tmp_scr[0, kx:kx + 1, :]

    @pl.when(tb == pl.num_programs(1) - 1)
    def _():
        st_ref[...] = s_scr[...]


def _rw_scan(r, k, v, w, a, params, s0, tb):
    seq, _, pairs = r.shape
    assert pairs % LANES == 0, "batch * RW_HEADS must fill whole lane groups"
    seqb = pl.BlockSpec((tb, RW_HD, LANES), lambda g, t: (t, 0, g))
    par = pl.BlockSpec((RW_HD, LANES), lambda g, t: (0, 0))
    st = pl.BlockSpec((RW_HD, RW_HD, LANES), lambda g, t: (0, 0, g))
    return pl.pallas_call(
        _rw_scan_kernel,
        grid=(pairs // LANES, seq // tb),
        in_specs=[seqb] * 5 + [par] * 5 + [st],
        out_specs=[seqb, st],
        out_shape=[jax.ShapeDtypeStruct((seq, RW_HD, pairs), F32),
                   jax.ShapeDtypeStruct((RW_HD, RW_HD, pairs), F32)],
        scratch_shapes=[pltpu.VMEM((RW_HD, RW_HD, LANES), F32),
                        pltpu.VMEM((4, RW_HD, LANES), F32)],
        compiler_params=_cparams(("parallel", "arbitrary")),
        name="rwkv_scan",
    )(r, k, v, w, a, *params, s0)


GLA_SUB = 16


def _gla_kernel(q_ref, k_ref, v_ref, gl_ref, og_ref, w2_ref, gb_ref, ng_ref, s0_ref,
                o_ref, st_ref, s_scr, la_scr, att_scr):
    tblk = pl.program_id(1)
    tc = q_ref.shape[0]

    @pl.when(tblk == 0)
    def _():
        for h in range(GLA_HEADS):
            s_scr[h] = s0_ref[h].T

    la_scr[...] = _log_sigmoid(_dot3(gl_ref[...], w2_ref[...]) + gb_ref[...]) * (1.0 / GLA_GATE_NORM)
    rows = lax.broadcasted_iota(I32, (CHUNK, GLA_DK), 0)
    lane = lax.broadcasted_iota(I32, (GLA_SUB, CHUNK), 1)
    jrow = lax.broadcasted_iota(I32, (GLA_SUB, GLA_DK), 0)
    nsub = CHUNK // GLA_SUB

    def head_chunk(h, rs):
        ks = slice(h * GLA_DK, (h + 1) * GLA_DK)
        vs = slice(h * GLA_DV, (h + 1) * GLA_DV)
        b = la_scr[rs, ks]
        sh = 1
        while sh < CHUNK:
            b = b + jnp.where(rows >= sh, pltpu.roll(b, sh, 0), 0.0)
            sh *= 2
        q = q_ref[rs, ks] * (GLA_DK ** -0.5)
        k = k_ref[rs, ks]
        v = v_ref[rs, vs]
        s_old = s_scr[h]
        inter = _nt_dot(q * jnp.exp(b), s_old)
        b_last = b[CHUNK - 1:CHUNK, :]

        for jb in range(nsub):
            js = slice(jb * GLA_SUB, (jb + 1) * GLA_SUB)
            bj = b[js]
            kj = k[js]
            blk = jnp.zeros((GLA_SUB, CHUNK), F32)
            for ii in range(GLA_SUB):
                i_row = jb * GLA_SUB + ii
                e = jnp.exp(jnp.minimum(b[i_row:i_row + 1, :] - bj, 0.0))
                val = jnp.where(jrow <= ii, q[i_row:i_row + 1, :] * kj * e, 0.0)
                col = jnp.sum(val, axis=1, keepdims=True)
                blk = jnp.where(lane == i_row, col, blk)
            if jb + 1 < nsub:
                i0 = (jb + 1) * GLA_SUB
                be = b[i0 - 1:i0, :]
                kt = kj * jnp.exp(be - bj)
                later = rows >= i0
                qt = jnp.where(later, q * jnp.exp(jnp.where(later, b - be, 0.0)), 0.0)
                blk = blk + _nt_dot(kt, qt)
            att_scr[h, js, :] = blk
        o = inter + _tn_dot(att_scr[h], v)
        khat = k * jnp.exp(b_last - b)
        s_scr[h] = s_old * jnp.exp(b_last) + _tn_dot(v, khat)
        o = o * lax.rsqrt(jnp.mean(o * o, -1, keepdims=True) + GLA_NORM_EPS) * ng_ref[:, vs]
        og = og_ref[rs, vs]
        o_ref[rs, vs] = (o * (og * _sigmoid(og))).astype(BF16)

    def chunk_body(c, carry):
        rs = pl.ds(pl.multiple_of(c * CHUNK, CHUNK), CHUNK)
        for h in range(GLA_HEADS):
            head_chunk(h, rs)
        return carry
    lax.fori_loop(0, tc // CHUNK, chunk_body, 0)

    @pl.when(tblk == pl.num_programs(1) - 1)
    def _():
        for h in range(GLA_HEADS):
            st_ref[h] = s_scr[h].T


def _gla(proj, tail, row0, nb, seq, w2p, gb, ng, s0, tc, out_rows=None):
    ntb = seq // tc
    rb0 = row0 // tc
    cq, ck = _OFF["qc"] // GLA_KW, _OFF["kc"] // GLA_KW
    cv, cg = _OFF["vc"] // GLA_VW, _OFF["og"] // GLA_VW
    cgl = (_OFF["gl"] - _OFF["ka"]) // LANES

    def rows(c0):
        return lambda b, t: (rb0 + b * ntb + t, c0)

    fix = lambda b, t: (0, 0)
    st = pl.BlockSpec((None, GLA_HEADS, GLA_DK, GLA_DV), lambda b, t: (b, 0, 0, 0))
    return pl.pallas_call(
        _gla_kernel,
        grid=(nb, ntb),
        in_specs=[pl.BlockSpec((tc, GLA_KW), rows(cq)), pl.BlockSpec((tc, GLA_KW), rows(ck)),
                  pl.BlockSpec((tc, GLA_VW), rows(cv)), pl.BlockSpec((tc, LANES), rows(cgl)),
                  pl.BlockSpec((tc, GLA_VW), rows(cg)),
                  pl.BlockSpec((LANES, GLA_KW), fix), pl.BlockSpec((1, GLA_KW), fix),
                  pl.BlockSpec((1, GLA_VW), fix), st],
        out_specs=[pl.BlockSpec((tc, GLA_VW), lambda b, t: (b * ntb + t, 0)), st],
        out_shape=[jax.ShapeDtypeStruct((out_rows or nb * seq, GLA_VW), BF16),
                   jax.ShapeDtypeStruct((nb, GLA_HEADS, GLA_DK, GLA_DV), F32)],
        scratch_shapes=[pltpu.VMEM((GLA_HEADS, GLA_DV, GLA_DK), F32), pltpu.VMEM((tc, GLA_KW), F32),
                        pltpu.VMEM((GLA_HEADS, CHUNK, CHUNK), F32)],
        compiler_params=_cparams(("parallel", "arbitrary")),
        name="gla",
    )(proj, proj, proj, tail, proj, w2p, gb, ng, s0)


_MAIN_SEGS = tuple(sg for sg in _SEGS if _OFF[sg[0]] < _OFF["ka"])
_TAIL_SEGS = tuple(sg for sg in _SEGS if _OFF[sg[0]] >= _OFF["ka"])


def _pack_cols(w, segs=_SEGS, dtype=None):
    dtype = dtype or w.dtype
    parts = []
    for _, s, wd, p in segs:
        parts.append(w[..., s:s + wd].astype(dtype))
        if p > wd:
            parts.append(jnp.zeros(w.shape[:-1] + (p - wd,), dtype))
    return jnp.concatenate(parts, axis=-1)


def _pack_rw(vec):
    r, k, v = vec[..., 0:1024], vec[..., 1024:2048], vec[..., 2048:3072]
    z = jnp.zeros(vec.shape[:-1] + (32,), vec.dtype)
    lr = jnp.concatenate([vec[..., 3072:3168], z, vec[..., 3168:3264], z, vec[..., 3264:3520]], -1)
    return r, k, v, lr


def _pad_rows(w, n):
    return jnp.concatenate([w, jnp.zeros((n - w.shape[0],) + w.shape[1:], w.dtype)], 0)


RELAYOUT_T = 128


def _to_scan_kernel(y_ref, o_ref):
    nbat = y_ref.shape[0]
    for n in range(RW_HD):
        z = jnp.concatenate([y_ref[b, pl.ds(n, RW_HEADS, stride=RW_HD), :] for b in range(nbat)], axis=0)
        o_ref[:, n, :] = z.T


def _from_scan_kernel(o_ref, x_ref, y_scr):
    nbat = x_ref.shape[0]
    for n in range(RW_HD):
        zt = o_ref[:, n, :].T
        for b in range(nbat):
            y_scr[b, pl.ds(n, RW_HEADS, stride=RW_HD), :] = zt[b * RW_HEADS:(b + 1) * RW_HEADS]
    for b in range(nbat):
        for c in range(RW_W // LANES):
            x_ref[b, :, c * LANES:(c + 1) * LANES] = y_scr[b, c * LANES:(c + 1) * LANES, :].T


def _relayout_ok(nb, seq):
    return nb * RW_HEADS == LANES and seq % RELAYOUT_T == 0


def _to_scan(x, nb, seq):
    if x.ndim == 2:
        return jnp.transpose(x.reshape(nb, seq, RW_HEADS, RW_HD), (1, 3, 0, 2)).reshape(seq, RW_HD, nb * RW_HEADS)
    return pl.pallas_call(
        _to_scan_kernel,
        grid=(seq // RELAYOUT_T,),
        in_specs=[pl.BlockSpec((nb, RW_W, RELAYOUT_T), lambda t: (0, 0, t))],
        out_specs=pl.BlockSpec((RELAYOUT_T, RW_HD, LANES), lambda t: (t, 0, 0)),
        out_shape=jax.ShapeDtypeStruct((seq, RW_HD, LANES), F32),
        compiler_params=_cparams(("parallel",)),
        name="to_scan_layout",
    )(x)


def _from_scan(o, nb, seq):
    if not _relayout_ok(nb, seq):
        return jnp.transpose(o.reshape(seq, RW_HD, nb, RW_HEADS), (2, 0, 3, 1)).reshape(nb * seq, RW_W)
    return pl.pallas_call(
        _from_scan_kernel,
        grid=(seq // RELAYOUT_T,),
        in_specs=[pl.BlockSpec((RELAYOUT_T, RW_HD, LANES), lambda t: (t, 0, 0))],
        out_specs=pl.BlockSpec((nb, RELAYOUT_T, RW_W), lambda t: (0, t, 0)),
        out_shape=jax.ShapeDtypeStruct((nb, seq, RW_W), F32),
        scratch_shapes=[pltpu.VMEM((nb, RW_W, RELAYOUT_T), F32)],
        compiler_params=_cparams(("parallel",)),
        name="from_scan_layout",
    )(o).reshape(nb * seq, RW_W)


def _group_mixers(proj, tail, row0, nb, seq, caches, lw, out_rows=None):
    m = nb * seq
    cache_k, cache_v, cache_ki, s_rw, s_shift, s_gla = caches
    past = 0 if cache_k is None else cache_k.shape[1]
    l_real = past + seq
    t0 = _OFF["ka"]
    grp = tail[row0:row0 + m].reshape(nb, seq, -1)

    ka = grp[..., _OFF["ka"] - t0:_OFF["ka"] - t0 + 256]
    va = grp[..., _OFF["va"] - t0:_OFF["va"] - t0 + 256]
    ki = grp[..., _OFF["ki"] - t0:_OFF["ki"] - t0 + LANES]
    if past:
        k_all = jnp.concatenate([cache_k.reshape(nb, past, 256), ka], 1)
        v_all = jnp.concatenate([cache_v.reshape(nb, past, 256), va], 1)
        cki = jnp.concatenate([cache_ki, jnp.zeros((nb, past, LANES - IDX_DIM), F32)], -1)
        ki_all = jnp.concatenate([cki, ki], 1)
    else:
        k_all, v_all, ki_all = ka, va, ki
    kblk = 512 if l_real % 512 == 0 else 384
    l_pad = -(-l_real // kblk) * kblk
    if l_pad > l_real:
        padk = lambda t: jnp.concatenate([t, jnp.zeros((nb, l_pad - l_real, t.shape[-1]), t.dtype)], 1)
        k_all, v_all, ki_all = padk(k_all), padk(v_all), padk(ki_all)
    k_bf = k_all.astype(BF16)
    vt_bf = jnp.swapaxes(v_all, 1, 2).astype(BF16)
    qb = DSA_QB if seq % DSA_QB == 0 else LANES
    if seq % qb == 0:
        nqb = seq // qb
        rb0 = row0 // qb
        mk = lambda c: (lambda b, i: (rb0 + b * nqb + i, c))
        q_ops = ((proj, mk(_OFF["qa"] // ATT_W)), (proj, mk(_OFF["qi"] // 1024)),
                 (tail, mk((_OFF["wi"] - t0) // LANES)))
        o_a = _dsa(q_ops, k_bf, vt_bf, ki_all, nb=nb, nqb=nqb, qb=qb, past=past, l_real=l_real, kblk=kblk,
                   out_rows=out_rows)
    else:
        def padq(t):
            return jnp.concatenate([t, jnp.zeros((nb, qb - seq, t.shape[-1]), F32)], 1).reshape(nb * qb, -1)
        qrows = proj[row0:row0 + m, 0:_OFF["vc"]].reshape(nb, seq, -1)
        mk = lambda b, i: (b, 0)
        q_ops = ((padq(qrows[..., 0:ATT_W]), mk), (padq(qrows[..., ATT_W:]), mk),
                 (padq(grp[..., _OFF["wi"] - t0:_OFF["wi"] - t0 + LANES]), mk))
        o_a = _dsa(q_ops, k_bf, vt_bf, ki_all, nb=nb, nqb=1, qb=qb, past=past, l_real=l_real, kblk=kblk)
        o_a = o_a.reshape(nb, qb, ATT_W)[:, :seq].reshape(m, ATT_W)

    if s_shift is None:
        s_shift = jnp.zeros((nb, RW_PROJ), F32)
    shifts = [t.reshape(nb, 1, -1) for t in _pack_rw(s_shift)]
    tm = 256 if seq % 256 == 0 else seq
    r, k, v, w, a, g = _rw_prep(proj, row0, nb, seq, shifts, lw["rw_mus"], lw["rw_w0"], lw["rw_a0"],
                                lw["rw_w2p"], lw["rw_a2p"], lw["rw_g2"], tm, out_rows)
    if s_rw is None:
        s0 = jnp.zeros((RW_HD, RW_HD, nb * RW_HEADS), F32)
    else:
        s0 = jnp.transpose(s_rw, (3, 2, 0, 1)).reshape(RW_HD, RW_HD, nb * RW_HEADS)
    tile = lambda p: jnp.tile(p.reshape(RW_HEADS, RW_HD).T, (1, LANES // RW_HEADS))
    params = [tile(lw[n]) for n in ("rw_kk", "rw_ka", "rw_rk", "rw_ln_g", "rw_ln_b")]
    tb = 32 if seq % 32 == 0 else seq
    o_b, s_t = _rw_scan(*[_to_scan(t, nb, seq) for t in (r, k, v, w, a)], params, s0, tb)
    o_b = _from_scan(o_b, nb, seq)
    s_rw_new = jnp.transpose(s_t.reshape(RW_HD, RW_HD, nb, RW_HEADS), (2, 3, 1, 0))
    last = jnp.take(proj, row0 + seq - 1 + seq * jnp.arange(nb), axis=0)[:, _OFF["r"]:_OFF["qc"]]
    base = _OFF["r"]
    shift_new = jnp.concatenate(
        [last[:, 0:3072], last[:, _OFF["xw"] - base:_OFF["xw"] - base + 96],
         last[:, _OFF["xa"] - base:_OFF["xa"] - base + 96], last[:, _OFF["xg"] - base:]], -1)

    if s_gla is None:
        s_gla = jnp.zeros((nb, GLA_HEADS, GLA_DK, GLA_DV), F32)
    tc = 512 if seq % 512 == 0 else seq
    o_c, s_gla_new = _gla(proj, tail, row0, nb, seq, lw["gla_w2p"], lw["gla_b"], lw["gla_ng"], s_gla, tc,
                          out_rows)

    new_state = (ka.reshape(nb, seq, ATT_KV, ATT_HD), va.reshape(nb, seq, ATT_KV, ATT_HD),
                 ki[..., :IDX_DIM], s_rw_new, shift_new, s_gla_new)
    return o_a, o_b, g, o_c, new_state


def kernel(x_prompt, x_sample, cache_k, cache_v, cache_kidx, state_rwkv, state_rwkv_shift, state_gla,
           w_in, rw_mu, rw_w0, rw_w2, rw_a0, rw_a2, rw_g2, rw_kk, rw_ka, rw_rk, rw_ln_g, rw_ln_b,
           gla_w2, gla_b, gla_norm_g, w_branch_a, w_branch_b, w_branch_c, w_out,
           ln1_g, ln1_b, ln2_g, ln2_b, ffn_w1, ffn_w3, ffn_w2):
    bp, tp, _ = x_prompt.shape
    bs, ts, _ = x_sample.shape
    mp, ms = bp * tp, bs * ts
    x = jnp.concatenate([x_prompt.reshape(mp, D_MODEL), x_sample.reshape(ms, D_MODEL)], 0)
    xb = x.astype(BF16)
    m = mp + ms
    tm_big = 1024 if m % 1024 == 0 else 256
    tm_mid = 512 if m % 512 == 0 else 256

    new_p = [[] for _ in range(6)]
    new_s = [[] for _ in range(6)]
    for l in range(DEPTH):
        lw = {
            "rw_mus": [t.reshape(1, -1) for t in _pack_rw(rw_mu[l])],
            "rw_w0": rw_w0[l].reshape(1, RW_W), "rw_a0": rw_a0[l].reshape(1, RW_W),
            "rw_w2p": _pad_rows(rw_w2[l], 128), "rw_a2p": _pad_rows(rw_a2[l], 128), "rw_g2": rw_g2[l],
            "rw_kk": rw_kk[l], "rw_ka": rw_ka[l], "rw_rk": rw_rk[l],
            "rw_ln_g": rw_ln_g[l], "rw_ln_b": rw_ln_b[l],
            "gla_w2p": _pad_rows(gla_w2[l], LANES), "gla_b": gla_b[l].reshape(1, GLA_KW),
            "gla_ng": gla_norm_g[l].reshape(1, GLA_VW),
        }
        proj = _matmul(xb, _pack_cols(w_in[l], _MAIN_SEGS, BF16), tm_big, 1024)
        tail = _matmul(xb, _pack_cols(w_in[l], _TAIL_SEGS, BF16), tm_big, PROJ_W - _OFF["ka"])
        oa_p, ob_p, g_p, oc_p, st_p = _group_mixers(proj, tail, 0, bp, tp, (None,) * 6, lw, out_rows=m)
        oa_s, ob_s, g_s, oc_s, st_s = _group_mixers(
            proj, tail, mp, bs, ts,
            (cache_k[l], cache_v[l], cache_kidx[l], state_rwkv[l], state_rwkv_shift[l], state_gla[l]), lw)
        put = lambda full, part: lax.dynamic_update_slice(full, part, (mp, 0))
        merged = _merge(put(oa_p, oa_s), jnp.concatenate([ob_p, ob_s], 0), put(g_p, g_s), put(oc_p, oc_s), proj,
                        w_branch_a[l].astype(BF16), w_branch_b[l].astype(BF16),
                        w_branch_c[l].astype(BF16), 256, D_MODEL)
        h, hb = _hproj(merged, w_out[l].astype(BF16), x, ln1_g[l].reshape(1, -1),
                       ln1_b[l].reshape(1, -1), tm_mid)
        x, xb = _ffn(hb, h, ffn_w1[l].astype(BF16), ffn_w3[l].astype(BF16), ffn_w2[l].astype(BF16),
                     ln2_g[l].reshape(1, -1), ln2_b[l].reshape(1, -1), tm_mid, 512)
        for i in range(6):
            new_p[i].append(st_p[i])
            new_s[i].append(st_s[i])
    sp = [jnp.stack(a) for a in new_p]
    ss = [jnp.stack(a) for a in new_s]
    yp = x[:mp].reshape(bp, tp, D_MODEL)
    ys = x[mp:].reshape(bs, ts, D_MODEL)
    return (yp, ys, sp[0], sp[1], sp[2], sp[3], sp[4], sp[5], ss[0], ss[1], ss[2], ss[3], ss[4], ss[5])
```
